```python
import math
import jax, jax.numpy as jnp
from jax import lax
import numpy as np

D_MODEL = 1024
BATCH = 8
SEQ = 2048
DEPTH = 1

N_HEADS_A = 8
HEAD_DIM_A = 64
V_DIM_A = 2 * HEAD_DIM_A
QK_WIDTH = N_HEADS_A * 2 * HEAD_DIM_A
ATTN_WIDTH = N_HEADS_A * V_DIM_A
Q_BLOCK = 128
N_GROUPS_B = 8
CHUNK = 128
GMLP_WIDTH = 1024
GROUP_DIM_B = GMLP_WIDTH // N_GROUPS_B
N_BUCKETS = 32
MAX_DISTANCE = 128
D_FF = 2816
N_SUBLAYERS = 3
N_BRANCHES = 2
EPS = 1e-6
COL_Q = 0
COL_K = COL_Q + QK_WIDTH
COL_V = COL_K + QK_WIDTH
COL_U = COL_V + ATTN_WIDTH
COL_GV = COL_U + GMLP_WIDTH
COL_GATE = COL_GV + GMLP_WIDTH
IN_COLS = COL_GATE + N_BRANCHES * D_MODEL

kernel_name = "hybrid_diffattn_gmlp_macaron_adaln"


def rmsnorm(x):
    xf = x.astype(jnp.float32)
    return (xf * lax.rsqrt(jnp.mean(xf * xf, axis=-1, keepdims=True) + EPS)).astype(x.dtype)


def layernorm(x, g, b):
    xf = x.astype(jnp.float32)
    mu = jnp.mean(xf, axis=-1, keepdims=True)
    var = jnp.mean(jnp.square(xf - mu), axis=-1, keepdims=True)
    return ((xf - mu) * lax.rsqrt(var + EPS)).astype(x.dtype) * g + b


def modulate(h, shift, scale):
    return h * (1.0 + scale[:, None, :]) + shift[:, None, :]


def swiglu(h, w_gate, w_up, w_down):
    return (jax.nn.silu(h @ w_gate) * (h @ w_up)) @ w_down


def rel_bucket(q_pos, k_pos):
    n = jnp.maximum(q_pos[:, None] - k_pos[None, :], 0)
    max_exact = N_BUCKETS // 2
    nf = jnp.maximum(n, 1).astype(jnp.float32)
    large = max_exact + (jnp.log(nf / max_exact) / math.log(MAX_DISTANCE / max_exact)
                         * (N_BUCKETS - max_exact)).astype(jnp.int32)
    large = jnp.minimum(large, N_BUCKETS - 1)
    return jnp.where(n < max_exact, n, large)


def diff_attention(q, k, v, rel_table, q_g, k_g, lam, lam_init, sub_g):
    b, s = q.shape[0], q.shape[1]
    q = rmsnorm(q) * q_g
    k = rmsnorm(k) * k_g
    scale = HEAD_DIM_A ** -0.5
    outs = []
    for i in range(s // Q_BLOCK):
        q0 = i * Q_BLOCK
        kv_len = q0 + Q_BLOCK
        qb = q[:, q0:kv_len]
        kb = k[:, :kv_len]
        vb = v[:, :kv_len]
        logits = jnp.einsum('bqhmd,bkhmd->bhmqk', qb, kb).astype(jnp.float32) * scale
        q_pos = q0 + jnp.arange(Q_BLOCK)
        k_pos = jnp.arange(kv_len)
        bias = jnp.transpose(rel_table[rel_bucket(q_pos, k_pos)], (2, 0, 1)).astype(jnp.float32)
        logits = logits + bias[None, :, None]
        mask = k_pos[None, :] <= q_pos[:, None]
        logits = jnp.where(mask, logits, -jnp.inf)
        p = jax.nn.softmax(logits, axis=-1)
        attn = p[:, :, 0] - lam * p[:, :, 1]
        outs.append(jnp.einsum('bhqk,bkhe->bqhe', attn.astype(vb.dtype), vb))
    o = jnp.concatenate(outs, axis=1)
    o = rmsnorm(o) * sub_g * (1.0 - lam_init)
    return o.reshape(b, s, ATTN_WIDTH)


def spatial_gating(u, v, ln_g, ln_b, w_s, b_s):
    b, s = u.shape[0], u.shape[1]
    v = layernorm(v, ln_g, ln_b)
    vc = v.reshape(b, s // CHUNK, CHUNK, N_GROUPS_B, GROUP_DIM_B)
    tri = jnp.tril(jnp.ones((CHUNK, CHUNK), dtype=bool))
    w = jnp.where(tri[None], w_s, jnp.zeros_like(w_s))
    f = jnp.einsum('gts,bnsgc->bntgc', w, vc) + jnp.transpose(b_s)[None, None, :, :, None]
    return u * f.reshape(b, s, GMLP_WIDTH)


def setup_inputs(seed: int = 0) -> dict:
    key = jax.random.key(seed)
    ks = jax.random.split(key, 32)
    f32 = jnp.float32
    L, D = DEPTH, D_MODEL

    def nrm(k, shape, fan_in):
        return jax.random.normal(k, shape, f32) * (fan_in ** -0.5)

    def gain(k, shape):
        return 1.0 + 0.05 * jax.random.normal(k, shape, f32)

    return {
        "x": jax.random.normal(ks[0], (BATCH, SEQ, D), f32),
        "c": jax.random.normal(ks[1], (BATCH, D), f32),
        "w_ada": nrm(ks[2], (L, D, N_SUBLAYERS * 3 * D), D) * 0.5,
        "b_ada": 0.02 * jax.random.normal(ks[3], (L, N_SUBLAYERS * 3 * D), f32),
        "w_ffn1_gate": nrm(ks[4], (L, D, D_FF), D),
        "w_ffn1_up": nrm(ks[5], (L, D, D_FF), D),
        "w_ffn1_down": nrm(ks[6], (L, D_FF, D), D_FF),
        "w_in": nrm(ks[7], (L, D, IN_COLS), D),
        "q_norm_g": gain(ks[8], (L, HEAD_DIM_A)),
        "k_norm_g": gain(ks[9], (L, HEAD_DIM_A)),
        "lam_q1": 0.1 * jax.random.normal(ks[10], (L, HEAD_DIM_A), f32),
        "lam_k1": 0.1 * jax.random.normal(ks[11], (L, HEAD_DIM_A), f32),
        "lam_q2": 0.1 * jax.random.normal(ks[12], (L, HEAD_DIM_A), f32),
        "lam_k2": 0.1 * jax.random.normal(ks[13], (L, HEAD_DIM_A), f32),
        "subln_g": gain(ks[14], (L, V_DIM_A)),
        "rel_bias_table": 0.5 * jax.random.normal(ks[15], (N_BUCKETS, N_HEADS_A), f32),
        "gmlp_ln_g": gain(ks[16], (L, GMLP_WIDTH)),
        "gmlp_ln_b": 0.02 * jax.random.normal(ks[17], (L, GMLP_WIDTH), f32),
        "w_spatial": nrm(ks[18], (L, N_GROUPS_B, CHUNK, CHUNK), CHUNK),
        "b_spatial": 1.0 + 0.05 * jax.random.normal(ks[19], (L, N_GROUPS_B, CHUNK), f32),
        "w_a_proj": nrm(ks[20], (L, ATTN_WIDTH, D), ATTN_WIDTH),
        "w_b_proj": nrm(ks[21], (L, GMLP_WIDTH, D), GMLP_WIDTH),
        "w_o": nrm(ks[22], (L, D, D), D),
        "w_ffn2_gate": nrm(ks[23], (L, D, D_FF), D),
        "w_ffn2_up": nrm(ks[24], (L, D, D_FF), D),
        "w_ffn2_down": nrm(ks[25], (L, D_FF, D), D_FF),
    }


def reference(x, c, w_ada, b_ada, w_ffn1_gate, w_ffn1_up, w_ffn1_down, w_in,
              q_norm_g, k_norm_g, lam_q1, lam_k1, lam_q2, lam_k2, subln_g,
              rel_bias_table, gmlp_ln_g, gmlp_ln_b, w_spatial, b_spatial,
              w_a_proj, w_b_proj, w_o, w_ffn2_gate, w_ffn2_up, w_ffn2_down):
    b, s, d = x.shape
    for l in range(DEPTH):
        lam_init = 0.8 - 0.6 * math.exp(-0.3 * l)
        mod = (jax.nn.silu(c) @ w_ada[l] + b_ada[l]).reshape(b, N_SUBLAYERS * 3, d)
        sh0, sc0, g0, sh1, sc1, g1, sh2, sc2, g2 = [mod[:, j] for j in range(N_SUBLAYERS * 3)]

        h = modulate(rmsnorm(x), sh0, sc0)
        x = x + 0.5 * g0[:, None, :] * swiglu(h, w_ffn1_gate[l], w_ffn1_up[l], w_ffn1_down[l])

        h = modulate(rmsnorm(x), sh1, sc1)
        proj = h @ w_in[l]
        q = proj[..., COL_Q:COL_K].reshape(b, s, N_HEADS_A, 2, HEAD_DIM_A)
        k = proj[..., COL_K:COL_V].reshape(b, s, N_HEADS_A, 2, HEAD_DIM_A)
        v = proj[..., COL_V:COL_U].reshape(b, s, N_HEADS_A, V_DIM_A)
        gu = jax.nn.gelu(proj[..., COL_U:COL_GV], approximate=False)
        gv = jax.nn.gelu(proj[..., COL_GV:COL_GATE], approximate=False)
        gates = jax.nn.sigmoid(proj[..., COL_GATE:].astype(jnp.float32)).astype(x.dtype)
        gates = gates.reshape(b, s, N_BRANCHES, d)

        lam = (jnp.exp(jnp.sum(lam_q1[l].astype(jnp.float32) * lam_k1[l].astype(jnp.float32)))
               - jnp.exp(jnp.sum(lam_q2[l].astype(jnp.float32) * lam_k2[l].astype(jnp.float32)))
               + lam_init)
        y_a = diff_attention(q, k, v, rel_bias_table, q_norm_g[l], k_norm_g[l], lam,
                             lam_init, subln_g[l]) @ w_a_proj[l]
        y_b = spatial_gating(gu, gv, gmlp_ln_g[l], gmlp_ln_b[l], w_spatial[l],
                             b_spatial[l]) @ w_b_proj[l]
        merged = gates[:, :, 0] * y_a + gates[:, :, 1] * y_b
        x = x + g1[:, None, :] * (merged @ w_o[l])

        h = modulate(rmsnorm(x), sh2, sc2)
        x = x + 0.5 * g2[:, None, :] * swiglu(h, w_ffn2_gate[l], w_ffn2_up[l], w_ffn2_down[l])
    return x
```

```python
import functools
import math

import numpy as np
import jax
import jax.numpy as jnp
from jax import lax
from jax.experimental import pallas as pl
from jax.experimental.pallas import tpu as pltpu

F32 = jnp.float32
BF16 = jnp.bfloat16

D_MODEL = 1024
BATCH = 8
SEQ = 2048
N_HEADS = 8
HEAD_DIM = 64
V_DIM = 2 * HEAD_DIM
QK_WIDTH = N_HEADS * 2 * HEAD_DIM
ATTN_WIDTH = N_HEADS * V_DIM
N_GROUPS = 8
CHUNK = 128
GMLP_WIDTH = 1024
GROUP_DIM = GMLP_WIDTH // N_GROUPS
N_BUCKETS = 32
MAX_DISTANCE = 128
D_FF = 2816
N_MOD = 9
EPS = 1e-6
LAM_INIT = 0.8 - 0.6 * math.exp(-0.3 * 0)
QKV_COLS = 2 * QK_WIDTH + ATTN_WIDTH
REST_COLS = 2 * GMLP_WIDTH + 2 * D_MODEL

LANES = 128
VMEM_LIMIT = 56 * 1024 * 1024

TM_FFN = 256
TM_PROJ = 256
TM_MIX = 512
TQ = 256
MOD_TN = 1536


def _const_spec(shape):
    nd = len(shape)
    return pl.BlockSpec(shape, lambda *_: (0,) * nd, pipeline_mode=pl.Buffered(1))


def _silu(x):
    return x * jax.nn.sigmoid(x)


def _gelu(x):
    return 0.5 * x * (1.0 + lax.erf(x * (1.0 / math.sqrt(2.0))))


def _modulated_rmsnorm(x, shift, scale):
    ms = jnp.mean(x * x, axis=-1, keepdims=True)
    return (x * lax.rsqrt(ms + EPS)) * (1.0 + scale) + shift


def _mod_kernel(c_ref, w_ref, b_ref, o_ref):
    a = _silu(c_ref[...]).astype(BF16)
    o_ref[...] = jnp.dot(a, w_ref[...].astype(BF16),
                         preferred_element_type=F32) + b_ref[...]


def _adaln_mod(c, w_ada, b_ada):
    n = w_ada.shape[1]
    return pl.pallas_call(
        _mod_kernel,
        grid=(n // MOD_TN,),
        in_specs=[
            pl.BlockSpec((BATCH, D_MODEL), lambda j: (0, 0)),
            pl.BlockSpec((D_MODEL, MOD_TN), lambda j: (0, j)),
            pl.BlockSpec((1, MOD_TN), lambda j: (0, j)),
        ],
        out_specs=pl.BlockSpec((BATCH, MOD_TN), lambda j: (0, j)),
        out_shape=jax.ShapeDtypeStruct((BATCH, n), F32),
        compiler_params=pltpu.CompilerParams(
            dimension_semantics=("arbitrary",), vmem_limit_bytes=VMEM_LIMIT),
        name="adaln_mod",
    )(c, w_ada, b_ada.reshape(1, n))


def _ffn_kernel(x_ref, mod_ref, wg_ref, wu_ref, wd_ref, o_ref, *, sub):
    x = x_ref[...]
    shift = mod_ref[0, 3 * sub + 0:3 * sub + 1, :]
    scale = mod_ref[0, 3 * sub + 1:3 * sub + 2, :]
    gate = mod_ref[0, 3 * sub + 2:3 * sub + 3, :]
    h = _modulated_rmsnorm(x, shift, scale).astype(BF16)
    g = jnp.dot(h, wg_ref[...], preferred_element_type=F32)
    u = jnp.dot(h, wu_ref[...], preferred_element_type=F32)
    a = (_silu(g) * u).astype(BF16)
    y = jnp.dot(a, wd_ref[...], preferred_element_type=F32)
    o_ref[...] = x + (0.5 * gate) * y


def _ffn(x2d, mod, wg, wu, wd, *, sub):
    t = x2d.shape[0]
    tiles_per_batch = SEQ // TM_FFN
    return pl.pallas_call(
        functools.partial(_ffn_kernel, sub=sub),
        grid=(t // TM_FFN,),
        in_specs=[
            pl.BlockSpec((TM_FFN, D_MODEL), lambda i: (i, 0)),
            pl.BlockSpec((1, N_MOD, D_MODEL), lambda i: (i // tiles_per_batch, 0, 0)),
            _const_spec((D_MODEL, D_FF)),
            _const_spec((D_MODEL, D_FF)),
            _const_spec((D_FF, D_MODEL)),
        ],
        out_specs=pl.BlockSpec((TM_FFN, D_MODEL), lambda i: (i, 0)),
        out_shape=jax.ShapeDtypeStruct((t, D_MODEL), F32),
        compiler_params=pltpu.CompilerParams(
            dimension_semantics=("arbitrary",), vmem_limit_bytes=VMEM_LIMIT),
        name=f"ffn{sub}",
    )(x2d, mod, wg, wu, wd)


def _chunk_rmsnorm_rows(xt, gain_rows):
    width, tm = xt.shape
    x3 = xt.reshape(width // HEAD_DIM, HEAD_DIM, tm)
    ms = jnp.mean(x3 * x3, axis=1, keepdims=True)
    xn = (x3 * lax.rsqrt(ms + EPS)).reshape(width, tm)
    return jnp.concatenate(
        [xn[:, j * LANES:(j + 1) * LANES] * gain_rows for j in range(tm // LANES)], axis=1)


def _inproj_kernel(x_ref, mod_ref, wqkvt_ref, wrest_ref, qg_ref, kg_ref,
                   lng_ref, lnb_ref, ws_ref, bs_ref,
                   qt_ref, k_ref, vt_ref, ub_ref, ga_ref, gb_ref):
    x = x_ref[...]
    shift = mod_ref[0, 3:4, :]
    scale = mod_ref[0, 4:5, :]
    h = _modulated_rmsnorm(x, shift, scale).astype(BF16)

    pt = lax.dot_general(wqkvt_ref[...], h, (((1,), (1,)), ((), ())),
                         preferred_element_type=F32)
    qt = _chunk_rmsnorm_rows(pt[0:QK_WIDTH], qg_ref[...])
    qt_ref[0] = qt.astype(BF16)
    kt = _chunk_rmsnorm_rows(pt[QK_WIDTH:2 * QK_WIDTH], kg_ref[...])
    k_ref[0] = kt.T.astype(BF16)
    vt_ref[0] = pt[2 * QK_WIDTH:QKV_COLS].astype(BF16)

    rest = jnp.dot(h, wrest_ref[...], preferred_element_type=F32)
    u = _gelu(rest[:, 0:GMLP_WIDTH])
    gv = _gelu(rest[:, GMLP_WIDTH:2 * GMLP_WIDTH])
    ga_ref[0] = jax.nn.sigmoid(rest[:, 2 * GMLP_WIDTH:2 * GMLP_WIDTH + D_MODEL])
    gb_ref[0] = jax.nn.sigmoid(rest[:, 2 * GMLP_WIDTH + D_MODEL:REST_COLS])

    mu = jnp.mean(gv, axis=-1, keepdims=True)
    cen = gv - mu
    var = jnp.mean(cen * cen, axis=-1, keepdims=True)
    vln = ((cen * lax.rsqrt(var + EPS)) * lng_ref[...] + lnb_ref[...]).astype(BF16)

    tm = x.shape[0]
    row = lax.broadcasted_iota(jnp.int32, (CHUNK, CHUNK), 0)
    col = lax.broadcasted_iota(jnp.int32, (CHUNK, CHUNK), 1)
    causal = row >= col
    for g in range(N_GROUPS):
        w = jnp.where(causal, ws_ref[g], 0.0).astype(BF16)
        cs = slice(g * GROUP_DIM, (g + 1) * GROUP_DIM)
        for n in range(tm // CHUNK):
            rs = slice(n * CHUNK, (n + 1) * CHUNK)
            f = jnp.dot(w, vln[rs, cs], preferred_element_type=F32) + bs_ref[:, cs]
            ub_ref[0, rs, cs] = (u[rs, cs] * f).astype(BF16)


def _inproj(x3d, mod, wqkvt, wrest, qg_rows, kg_rows, ln_g, ln_b, w_spatial, bs_full):
    nt = SEQ // TM_PROJ
    tok_spec = lambda width: pl.BlockSpec((1, TM_PROJ, width), lambda b, i: (b, i, 0))
    feat_spec = lambda width: pl.BlockSpec((1, width, TM_PROJ), lambda b, i: (b, 0, i))
    return pl.pallas_call(
        _inproj_kernel,
        grid=(BATCH, nt),
        in_specs=[
            pl.BlockSpec((None, TM_PROJ, D_MODEL), lambda b, i: (b, i, 0)),
            pl.BlockSpec((1, N_MOD, D_MODEL), lambda b, i: (b, 0, 0)),
            _const_spec((QKV_COLS, D_MODEL)),
            _const_spec((D_MODEL, REST_COLS)),
            _const_spec((QK_WIDTH, LANES)),
            _const_spec((QK_WIDTH, LANES)),
            _const_spec((1, GMLP_WIDTH)),
            _const_spec((1, GMLP_WIDTH)),
            _const_spec((N_GROUPS, CHUNK, CHUNK)),
            _const_spec((CHUNK, GMLP_WIDTH)),
        ],
        out_specs=[
            feat_spec(QK_WIDTH), tok_spec(QK_WIDTH), feat_spec(ATTN_WIDTH),
            tok_spec(GMLP_WIDTH), tok_spec(D_MODEL), tok_spec(D_MODEL),
        ],
        out_shape=[
            jax.ShapeDtypeStruct((BATCH, QK_WIDTH, SEQ), BF16),
            jax.ShapeDtypeStruct((BATCH, SEQ, QK_WIDTH), BF16),
            jax.ShapeDtypeStruct((BATCH, ATTN_WIDTH, SEQ), BF16),
            jax.ShapeDtypeStruct((BATCH, SEQ, GMLP_WIDTH), BF16),
            jax.ShapeDtypeStruct((BATCH, SEQ, D_MODEL), F32),
            jax.ShapeDtypeStruct((BATCH, SEQ, D_MODEL), F32),
        ],
        compiler_params=pltpu.CompilerParams(
            dimension_semantics=("arbitrary", "arbitrary"), vmem_limit_bytes=VMEM_LIMIT),
        name="inproj",
    )(x3d, mod, wqkvt, wrest, qg_rows, kg_rows, ln_g, ln_b, w_spatial, bs_full)


def _rel_buckets(dist):
    n = np.maximum(dist, 0)
    max_exact = N_BUCKETS // 2
    nf = np.maximum(n, 1).astype(np.float32)
    large = max_exact + (np.log(nf / np.float32(max_exact))
                         / np.float32(math.log(MAX_DISTANCE / max_exact))
                         * np.float32(N_BUCKETS - max_exact)).astype(np.int32)
    large = np.minimum(large, N_BUCKETS - 1)
    return np.where(n < max_exact, n, large).astype(np.int32)


def _attn_kernel(table_ref, lam_ref, qt_ref, k_ref, vt_ref, bkt_ref, subg_ref,
                 o_ref, bias_ref, acc_ref, m_ref, l_ref):
    h = pl.program_id(0)
    b = pl.program_id(1)
    qi = pl.program_id(2)

    @pl.when(jnp.logical_and(b == 0, qi == 0))
    def _():
        far = table_ref[N_BUCKETS - 1, h]
        for t in range(2):
            bkt = bkt_ref[t]
            tile = jnp.zeros((TQ, TQ), F32)
            for bucket in range(N_BUCKETS - 1):
                tile = jnp.where(bkt == bucket, table_ref[bucket, h] - far, tile)
            bias_ref[t] = tile

    qt = qt_ref[0]
    zeros = jnp.zeros((HEAD_DIM, TQ), BF16)
    q_pad = (jnp.concatenate([qt[0:HEAD_DIM], zeros], axis=0),
             jnp.concatenate([zeros, qt[HEAD_DIM:2 * HEAD_DIM]], axis=0))

    m_ref[...] = jnp.full(m_ref.shape, -jnp.inf, F32)
    l_ref[...] = jnp.zeros(l_ref.shape, F32)
    acc_ref[...] = jnp.zeros(acc_ref.shape, F32)

    def step(j, bias):
        start = pl.multiple_of(j * TQ, TQ)
        kb = k_ref[0, pl.ds(start, TQ), :]
        vtb = vt_ref[0, :, pl.ds(start, TQ)]
        for mi in range(2):
            s = jnp.dot(kb, q_pad[mi], preferred_element_type=F32)
            if bias is not None:
                s = s + bias
            m_old = m_ref[mi]
            m_new = jnp.maximum(m_old, jnp.max(s, axis=0, keepdims=True))
            alpha = jnp.exp(m_old - m_new)
            p = jnp.exp(s - m_new)
            l_ref[mi] = alpha * l_ref[mi] + jnp.sum(p, axis=0, keepdims=True)
            m_ref[mi] = m_new
            acc_ref[mi] = alpha * acc_ref[mi] + jnp.dot(
                vtb, p.astype(BF16), preferred_element_type=F32)

    def far_body(j, carry):
        step(j, None)
        return carry

    lax.fori_loop(0, jnp.maximum(qi - 1, 0), far_body, 0)

    @pl.when(qi >= 1)
    def _():
        step(qi - 1, bias_ref[1])

    key = lax.broadcasted_iota(jnp.int32, (TQ, TQ), 0)
    qry = lax.broadcasted_iota(jnp.int32, (TQ, TQ), 1)
    step(qi, jnp.where(key <= qry, bias_ref[0], -jnp.inf))

    lp = lam_ref[...]
    lam = (jnp.exp(jnp.sum(lp[0:1] * lp[1:2], axis=1, keepdims=True))
           - jnp.exp(jnp.sum(lp[2:3] * lp[3:4], axis=1, keepdims=True)) + LAM_INIT)
    ot = acc_ref[0] / l_ref[0] - lam * (acc_ref[1] / l_ref[1])
    ms = jnp.mean(ot * ot, axis=0, keepdims=True)
    on = (ot * lax.rsqrt(ms + EPS)).T
    o_ref[0] = (on * subg_ref[...]).astype(BF16)


def _attention(table, lam, qt, k, vt, buckets, subg_row):
    nq = SEQ // TQ
    return pl.pallas_call(
        _attn_kernel,
        grid=(N_HEADS, BATCH, nq),
        in_specs=[
            pl.BlockSpec(memory_space=pltpu.SMEM),
            _const_spec((4, HEAD_DIM)),
            pl.BlockSpec((1, 2 * HEAD_DIM, TQ), lambda h, b, i: (b, h, i)),
            pl.BlockSpec((1, SEQ, 2 * HEAD_DIM), lambda h, b, i: (b, 0, h)),
            pl.BlockSpec((1, V_DIM, SEQ), lambda h, b, i: (b, h, 0)),
            _const_spec((2, TQ, TQ)),
            _const_spec((1, V_DIM)),
        ],
        out_specs=pl.BlockSpec((1, TQ, V_DIM), lambda h, b, i: (b, i, h)),
        out_shape=jax.ShapeDtypeStruct((BATCH, SEQ, ATTN_WIDTH), BF16),
        scratch_shapes=[
            pltpu.VMEM((2, TQ, TQ), F32),
            pltpu.VMEM((2, V_DIM, TQ), F32),
            pltpu.VMEM((2, 1, TQ), F32),
            pltpu.VMEM((2, 1, TQ), F32),
        ],
        compiler_params=pltpu.CompilerParams(
            dimension_semantics=("arbitrary", "arbitrary", "arbitrary"),
            vmem_limit_bytes=VMEM_LIMIT),
        name="diff_attn",
    )(table, lam, qt, k, vt, buckets, subg_row)


def _mix_kernel(x_ref, mod_ref, o_ref, ub_ref, ga_ref, gb_ref, wa_ref, wb_ref, wo_ref,
                out_ref):
    ya = jnp.dot(o_ref[...], wa_ref[...], preferred_element_type=F32)
    yb = jnp.dot(ub_ref[...], wb_ref[...], preferred_element_type=F32)
    merged = (ga_ref[...] * ya + gb_ref[...] * yb).astype(BF16)
    z = jnp.dot(merged, wo_ref[...], preferred_element_type=F32)
    out_ref[...] = x_ref[...] + mod_ref[0, 5:6, :] * z


def _mix(x2d, mod, o2d, ub2d, ga2d, gb2d, wa, wb, wo):
    t = x2d.shape[0]
    tiles_per_batch = SEQ // TM_MIX
    tok = lambda: pl.BlockSpec((TM_MIX, D_MODEL), lambda i: (i, 0))
    return pl.pallas_call(
        _mix_kernel,
        grid=(t // TM_MIX,),
        in_specs=[
            tok(),
            pl.BlockSpec((1, N_MOD, D_MODEL), lambda i: (i // tiles_per_batch, 0, 0)),
            tok(), tok(), tok(), tok(),
            _const_spec((ATTN_WIDTH, D_MODEL)),
            _const_spec((GMLP_WIDTH, D_MODEL)),
            _const_spec((D_MODEL, D_MODEL)),
        ],
        out_specs=tok(),
        out_shape=jax.ShapeDtypeStruct((t, D_MODEL), F32),
        compiler_params=pltpu.CompilerParams(
            dimension_semantics=("arbitrary",), vmem_limit_bytes=VMEM_LIMIT),
        name="mix_out",
    )(x2d, mod, o2d, ub2d, ga2d, gb2d, wa, wb, wo)


def kernel(x, c, w_ada, b_ada, w_ffn1_gate, w_ffn1_up, w_ffn1_down, w_in, q_norm_g, k_norm_g, lam_q1, lam_k1, lam_q2, lam_k2, subln_g, rel_bias_table, gmlp_ln_g, gmlp_ln_b, w_spatial, b_spatial, w_a_proj, w_b_proj, w_o, w_ffn2_gate, w_ffn2_up, w_ffn2_down):
    b, s, d = x.shape
    t = b * s
    bf = lambda w: w.astype(BF16)

    mod = _adaln_mod(c, w_ada[0], b_ada[0]).reshape(b, N_MOD, d)

    x1 = _ffn(x.reshape(t, d), mod, bf(w_ffn1_gate[0]), bf(w_ffn1_up[0]),
              bf(w_ffn1_down[0]), sub=0)

    wqkvt = bf(w_in[0][:, :QKV_COLS].T)
    wrest = bf(w_in[0][:, QKV_COLS:])
    scale = HEAD_DIM ** -0.5
    qg_rows = jnp.broadcast_to(
        jnp.tile(q_norm_g[0] * scale, QK_WIDTH // HEAD_DIM)[:, None], (QK_WIDTH, LANES))
    kg_rows = jnp.broadcast_to(
        jnp.tile(k_norm_g[0], QK_WIDTH // HEAD_DIM)[:, None], (QK_WIDTH, LANES))
    bs_full = jnp.repeat(b_spatial[0].T, GROUP_DIM, axis=1)
    qt, k, vt, ub, ga, gb = _inproj(
        x1.reshape(b, s, d), mod, wqkvt, wrest, qg_rows, kg_rows,
        gmlp_ln_g[0].reshape(1, -1), gmlp_ln_b[0].reshape(1, -1), w_spatial[0], bs_full)

    lam = jnp.stack([lam_q1[0], lam_k1[0], lam_q2[0], lam_k2[0]])
    kk = np.arange(TQ)[:, None]
    qq = np.arange(TQ)[None, :]
    buckets = jnp.asarray(np.stack([_rel_buckets(qq - kk), _rel_buckets(TQ + qq - kk)]))
    subg_row = (subln_g[0] * (1.0 - LAM_INIT)).reshape(1, V_DIM)
    o = _attention(rel_bias_table, lam, qt, k, vt, buckets, subg_row)

    x2 = _mix(x1, mod, o.reshape(t, ATTN_WIDTH), ub.reshape(t, GMLP_WIDTH),
              ga.reshape(t, d), gb.reshape(t, d),
              bf(w_a_proj[0]), bf(w_b_proj[0]), bf(w_o[0]))

    x3 = _ffn(x2, mod, bf(w_ffn2_gate[0]), bf(w_ffn2_up[0]), bf(w_ffn2_down[0]), sub=2)
    return x3.reshape(b, s, d)
```

```python
import functools
import math

import numpy as np
import jax
import jax.numpy as jnp
from jax import lax
from jax.experimental import pallas as pl
from jax.experimental.pallas import tpu as pltpu

F32 = jnp.float32
BF16 = jnp.bfloat16

D_MODEL = 1024
BATCH = 8
SEQ = 2048
N_HEADS = 8
HEAD_DIM = 64
V_DIM = 2 * HEAD_DIM
QK_WIDTH = N_HEADS * 2 * HEAD_DIM
ATTN_WIDTH = N_HEADS * V_DIM
N_GROUPS = 8
CHUNK = 128
GMLP_WIDTH = 1024
GROUP_DIM = GMLP_WIDTH // N_GROUPS
N_BUCKETS = 32
MAX_DISTANCE = 128
D_FF = 2816
N_MOD = 9
EPS = 1e-6
LAM_INIT = 0.8 - 0.6 * math.exp(-0.3 * 0)
LOG2E = math.log2(math.e)
QKV_COLS = 2 * QK_WIDTH + ATTN_WIDTH
REST_COLS = 2 * GMLP_WIDTH + 2 * D_MODEL

LANES = 128
VMEM_LIMIT = 56 * 1024 * 1024

TM_FFN = 256
TM_PROJ = 256
TM_MIX = 512
TQ = 256
MOD_TN = 1536


def _const_spec(shape):
    nd = len(shape)
    return pl.BlockSpec(shape, lambda *_: (0,) * nd, pipeline_mode=pl.Buffered(1))


def _silu(x):
    return x * jax.nn.sigmoid(x)


def _gelu(x):
    return 0.5 * x * (1.0 + lax.erf(x * (1.0 / math.sqrt(2.0))))


def _modulated_rmsnorm(x, shift, scale):
    ms = jnp.mean(x * x, axis=-1, keepdims=True)
    return (x * lax.rsqrt(ms + EPS)) * (1.0 + scale) + shift


def _mod_kernel(c_ref, w_ref, b_ref, o_ref):
    a = _silu(c_ref[...]).astype(BF16)
    o_ref[...] = jnp.dot(a, w_ref[...].astype(BF16),
                         preferred_element_type=F32) + b_ref[...]


def _adaln_mod(c, w_ada, b_ada):
    n = w_ada.shape[1]
    return pl.pallas_call(
        _mod_kernel,
        grid=(n // MOD_TN,),
        in_specs=[
            pl.BlockSpec((BATCH, D_MODEL), lambda j: (0, 0)),
            pl.BlockSpec((D_MODEL, MOD_TN), lambda j: (0, j)),
            pl.BlockSpec((1, MOD_TN), lambda j: (0, j)),
        ],
        out_specs=pl.BlockSpec((BATCH, MOD_TN), lambda j: (0, j)),
        out_shape=jax.ShapeDtypeStruct((BATCH, n), F32),
        compiler_params=pltpu.CompilerParams(
            dimension_semantics=("arbitrary",), vmem_limit_bytes=VMEM_LIMIT),
        name="adaln_mod",
    )(c, w_ada, b_ada.reshape(1, n))


def _ffn_kernel(x_ref, mod_ref, wg_ref, wu_ref, wd_ref, o_ref, *, sub):
    x = x_ref[...]
    shift = mod_ref[0, 3 * sub + 0:3 * sub + 1, :]
    scale = mod_ref[0, 3 * sub + 1:3 * sub + 2, :]
    gate = mod_ref[0, 3 * sub + 2:3 * sub + 3, :]
    h = _modulated_rmsnorm(x, shift, scale).astype(BF16)
    g = jnp.dot(h, wg_ref[...], preferred_element_type=F32)
    u = jnp.dot(h, wu_ref[...], preferred_element_type=F32)
    a = (_silu(g) * u).astype(BF16)
    y = jnp.dot(a, wd_ref[...], preferred_element_type=F32)
    o_ref[...] = x + (0.5 * gate) * y


def _ffn(x2d, mod, wg, wu, wd, *, sub):
    t = x2d.shape[0]
    tiles_per_batch = SEQ // TM_FFN
    return pl.pallas_call(
        functools.partial(_ffn_kernel, sub=sub),
        grid=(t // TM_FFN,),
        in_specs=[
            pl.BlockSpec((TM_FFN, D_MODEL), lambda i: (i, 0)),
            pl.BlockSpec((1, N_MOD, D_MODEL), lambda i: (i // tiles_per_batch, 0, 0)),
            _const_spec((D_MODEL, D_FF)),
            _const_spec((D_MODEL, D_FF)),
            _const_spec((D_FF, D_MODEL)),
        ],
        out_specs=pl.BlockSpec((TM_FFN, D_MODEL), lambda i: (i, 0)),
        out_shape=jax.ShapeDtypeStruct((t, D_MODEL), F32),
        compiler_params=pltpu.CompilerParams(
            dimension_semantics=("arbitrary",), vmem_limit_bytes=VMEM_LIMIT),
        name=f"ffn{sub}",
    )(x2d, mod, wg, wu, wd)


def _chunk_rmsnorm_rows(xt, gain_rows):
    width, tm = xt.shape
    x3 = xt.reshape(width // HEAD_DIM, HEAD_DIM, tm)
    ms = jnp.mean(x3 * x3, axis=1, keepdims=True)
    xn = (x3 * lax.rsqrt(ms + EPS)).reshape(width, tm)
    return jnp.concatenate(
        [xn[:, j * LANES:(j + 1) * LANES] * gain_rows for j in range(tm // LANES)], axis=1)


def _inproj_kernel(x_ref, mod_ref, wqkvt_ref, wrest_ref, qg_ref, kg_ref,
                   lng_ref, lnb_ref, ws_ref, bs_ref,
                   qt_ref, k_ref, vt_ref, ub_ref, ga_ref, gb_ref):
    x = x_ref[...]
    shift = mod_ref[0, 3:4, :]
    scale = mod_ref[0, 4:5, :]
    h = _modulated_rmsnorm(x, shift, scale).astype(BF16)

    pt = lax.dot_general(wqkvt_ref[...], h, (((1,), (1,)), ((), ())),
                         preferred_element_type=F32)
    qt = _chunk_rmsnorm_rows(pt[0:QK_WIDTH], qg_ref[...])
    qt_ref[0] = qt.astype(BF16)
    kt = _chunk_rmsnorm_rows(pt[QK_WIDTH:2 * QK_WIDTH], kg_ref[...])
    k_ref[0] = kt.T.astype(BF16)
    vt_ref[0] = pt[2 * QK_WIDTH:QKV_COLS].astype(BF16)

    rest = jnp.dot(h, wrest_ref[...], preferred_element_type=F32)
    u = _gelu(rest[:, 0:GMLP_WIDTH])
    gv = _gelu(rest[:, GMLP_WIDTH:2 * GMLP_WIDTH])
    ga_ref[0] = jax.nn.sigmoid(rest[:, 2 * GMLP_WIDTH:2 * GMLP_WIDTH + D_MODEL])
    gb_ref[0] = jax.nn.sigmoid(rest[:, 2 * GMLP_WIDTH + D_MODEL:REST_COLS])

    mu = jnp.mean(gv, axis=-1, keepdims=True)
    cen = gv - mu
    var = jnp.mean(cen * cen, axis=-1, keepdims=True)
    vln = ((cen * lax.rsqrt(var + EPS)) * lng_ref[...] + lnb_ref[...]).astype(BF16)

    tm = x.shape[0]
    row = lax.broadcasted_iota(jnp.int32, (CHUNK, CHUNK), 0)
    col = lax.broadcasted_iota(jnp.int32, (CHUNK, CHUNK), 1)
    causal = row >= col
    for g in range(N_GROUPS):
        w = jnp.where(causal, ws_ref[g], 0.0).astype(BF16)
        cs = slice(g * GROUP_DIM, (g + 1) * GROUP_DIM)
        for n in range(tm // CHUNK):
            rs = slice(n * CHUNK, (n + 1) * CHUNK)
            f = jnp.dot(w, vln[rs, cs], preferred_element_type=F32) + bs_ref[:, cs]
            ub_ref[0, rs, cs] = (u[rs, cs] * f).astype(BF16)


def _inproj(x3d, mod, wqkvt, wrest, qg_rows, kg_rows, ln_g, ln_b, w_spatial, bs_full):
    nt = SEQ // TM_PROJ
    tok_spec = lambda width: pl.BlockSpec((1, TM_PROJ, width), lambda b, i: (b, i, 0))
    feat_spec = lambda width: pl.BlockSpec((1, width, TM_PROJ), lambda b, i: (b, 0, i))
    return pl.pallas_call(
        _inproj_kernel,
        grid=(BATCH, nt),
        in_specs=[
            pl.BlockSpec((None, TM_PROJ, D_MODEL), lambda b, i: (b, i, 0)),
            pl.BlockSpec((1, N_MOD, D_MODEL), lambda b, i: (b, 0, 0)),
            _const_spec((QKV_COLS, D_MODEL)),
            _const_spec((D_MODEL, REST_COLS)),
            _const_spec((QK_WIDTH, LANES)),
            _const_spec((QK_WIDTH, LANES)),
            _const_spec((1, GMLP_WIDTH)),
            _const_spec((1, GMLP_WIDTH)),
            _const_spec((N_GROUPS, CHUNK, CHUNK)),
            _const_spec((CHUNK, GMLP_WIDTH)),
        ],
        out_specs=[
            feat_spec(QK_WIDTH), tok_spec(QK_WIDTH), feat_spec(ATTN_WIDTH),
            tok_spec(GMLP_WIDTH), tok_spec(D_MODEL), tok_spec(D_MODEL),
        ],
        out_shape=[
            jax.ShapeDtypeStruct((BATCH, QK_WIDTH, SEQ), BF16),
            jax.ShapeDtypeStruct((BATCH, SEQ, QK_WIDTH), BF16),
            jax.ShapeDtypeStruct((BATCH, ATTN_WIDTH, SEQ), BF16),
            jax.ShapeDtypeStruct((BATCH, SEQ, GMLP_WIDTH), BF16),
            jax.ShapeDtypeStruct((BATCH, SEQ, D_MODEL), F32),
            jax.ShapeDtypeStruct((BATCH, SEQ, D_MODEL), F32),
        ],
        compiler_params=pltpu.CompilerParams(
            dimension_semantics=("arbitrary", "arbitrary"), vmem_limit_bytes=VMEM_LIMIT),
        name="inproj",
    )(x3d, mod, wqkvt, wrest, qg_rows, kg_rows, ln_g, ln_b, w_spatial, bs_full)


def _rel_buckets(dist):
    n = np.maximum(dist, 0)
    max_exact = N_BUCKETS // 2
    nf = np.maximum(n, 1).astype(np.float32)
    large = max_exact + (np.log(nf / np.float32(max_exact))
                         / np.float32(math.log(MAX_DISTANCE / max_exact))
                         * np.float32(N_BUCKETS - max_exact)).astype(np.int32)
    large = np.minimum(large, N_BUCKETS - 1)
    return np.where(n < max_exact, n, large).astype(np.int32)


def _attn_kernel(table_ref, lam_ref, qt_ref, k_ref, vt_ref, bkt_ref, subg_ref,
                 o_ref, bias_ref):
    h = pl.program_id(0)
    b = pl.program_id(1)

    @pl.when(b == 0)
    def _():
        far = table_ref[N_BUCKETS - 1, h]
        key = lax.broadcasted_iota(jnp.int32, (TQ, TQ), 0)
        qry = lax.broadcasted_iota(jnp.int32, (TQ, TQ), 1)
        for t in range(2):
            bkt = bkt_ref[t]
            tile = jnp.zeros((TQ, TQ), F32)
            for bucket in range(N_BUCKETS - 1):
                tile = jnp.where(bkt == bucket, (table_ref[bucket, h] - far) * LOG2E, tile)
            if t == 0:
                tile = jnp.where(key <= qry, tile, -jnp.inf)
            bias_ref[t] = jnp.concatenate([tile, tile], axis=1)

    lp = lam_ref[...]
    lam = (jnp.exp(jnp.sum(lp[0:1] * lp[1:2], axis=1, keepdims=True))
           - jnp.exp(jnp.sum(lp[2:3] * lp[3:4], axis=1, keepdims=True)) + LAM_INIT)
    subg = subg_ref[...]
    zeros = jnp.zeros((HEAD_DIM, TQ), BF16)

    def scores(qpad, j):
        return jnp.dot(k_ref[0, j * TQ:(j + 1) * TQ, :], qpad, preferred_element_type=F32)

    for qi in range(SEQ // TQ):
        qt = qt_ref[0, :, qi * TQ:(qi + 1) * TQ]
        qpad = jnp.concatenate(
            [jnp.concatenate([qt[0:HEAD_DIM], zeros], axis=0),
             jnp.concatenate([zeros, qt[HEAD_DIM:2 * HEAD_DIM]], axis=0)], axis=1)
        m = l = acc = None
        s_next = scores(qpad, 0)
        for j in range(qi + 1):
            s = s_next
            if j < qi:
                s_next = scores(qpad, j + 1)
            if j == qi:
                s = s + bias_ref[0]
            elif j == qi - 1:
                s = s + bias_ref[1]
            blk_max = jnp.max(s, axis=0, keepdims=True)
            m_new = blk_max if j == 0 else jnp.maximum(m, blk_max)
            p = jnp.exp2(s - m_new)
            psum = jnp.sum(p, axis=0, keepdims=True)
            pv = jnp.dot(vt_ref[0, :, j * TQ:(j + 1) * TQ], p.astype(BF16),
                         preferred_element_type=F32)
            if j == 0:
                l, acc = psum, pv
            else:
                alpha = jnp.exp2(m - m_new)
                l = alpha * l + psum
                acc = alpha * acc + pv
            m = m_new
        inv = 1.0 / l
        ot = acc[:, 0:TQ] * inv[:, 0:TQ] - acc[:, TQ:2 * TQ] * (lam * inv[:, TQ:2 * TQ])
        ms = jnp.mean(ot * ot, axis=0, keepdims=True)
        on = (ot * lax.rsqrt(ms + EPS)).T
        o_ref[0, qi * TQ:(qi + 1) * TQ, :] = (on * subg).astype(BF16)


def _attention(table, lam, qt, k, vt, buckets, subg_row):
    return pl.pallas_call(
        _attn_kernel,
        grid=(N_HEADS, BATCH),
        in_specs=[
            pl.BlockSpec(memory_space=pltpu.SMEM),
            _const_spec((4, HEAD_DIM)),
            pl.BlockSpec((1, 2 * HEAD_DIM, SEQ), lambda h, b: (b, h, 0)),
            pl.BlockSpec((1, SEQ, 2 * HEAD_DIM), lambda h, b: (b, 0, h)),
            pl.BlockSpec((1, V_DIM, SEQ), lambda h, b: (b, h, 0)),
            _const_spec((2, TQ, TQ)),
            _const_spec((1, V_DIM)),
        ],
        out_specs=pl.BlockSpec((1, SEQ, V_DIM), lambda h, b: (b, 0, h)),
        out_shape=jax.ShapeDtypeStruct((BATCH, SEQ, ATTN_WIDTH), BF16),
        scratch_shapes=[pltpu.VMEM((2, TQ, 2 * TQ), F32)],
        compiler_params=pltpu.CompilerParams(
            dimension_semantics=("arbitrary", "arbitrary"),
            vmem_limit_bytes=VMEM_LIMIT),
        name="diff_attn",
    )(table, lam, qt, k, vt, buckets, subg_row)


def _mix_kernel(x_ref, mod_ref, o_ref, ub_ref, ga_ref, gb_ref, wa_ref, wb_ref, wo_ref,
                out_ref):
    ya = jnp.dot(o_ref[...], wa_ref[...], preferred_element_type=F32)
    yb = jnp.dot(ub_ref[...], wb_ref[...], preferred_element_type=F32)
    merged = (ga_ref[...] * ya + gb_ref[...] * yb).astype(BF16)
    z = jnp.dot(merged, wo_ref[...], preferred_element_type=F32)
    out_ref[...] = x_ref[...] + mod_ref[0, 5:6, :] * z


def _mix(x2d, mod, o2d, ub2d, ga2d, gb2d, wa, wb, wo):
    t = x2d.shape[0]
    tiles_per_batch = SEQ // TM_MIX
    tok = lambda: pl.BlockSpec((TM_MIX, D_MODEL), lambda i: (i, 0))
    return pl.pallas_call(
        _mix_kernel,
        grid=(t // TM_MIX,),
        in_specs=[
            tok(),
            pl.BlockSpec((1, N_MOD, D_MODEL), lambda i: (i // tiles_per_batch, 0, 0)),
            tok(), tok(), tok(), tok(),
            _const_spec((ATTN_WIDTH, D_MODEL)),
            _const_spec((GMLP_WIDTH, D_MODEL)),
            _const_spec((D_MODEL, D_MODEL)),
        ],
        out_specs=tok(),
        out_shape=jax.ShapeDtypeStruct((t, D_MODEL), F32),
        compiler_params=pltpu.CompilerParams(
            dimension_semantics=("arbitrary",), vmem_limit_bytes=VMEM_LIMIT),
        name="mix_out",
    )(x2d, mod, o2d, ub2d, ga2d, gb2d, wa, wb, wo)


def kernel(x, c, w_ada, b_ada, w_ffn1_gate, w_ffn1_up, w_ffn1_down, w_in, q_norm_g, k_norm_g, lam_q1, lam_k1, lam_q2, lam_k2, subln_g, rel_bias_table, gmlp_ln_g, gmlp_ln_b, w_spatial, b_spatial, w_a_proj, w_b_proj, w_o, w_ffn2_gate, w_ffn2_up, w_ffn2_down):
    b, s, d = x.shape
    t = b * s
    bf = lambda w: w.astype(BF16)

    mod = _adaln_mod(c, w_ada[0], b_ada[0]).reshape(b, N_MOD, d)

    x1 = _ffn(x.reshape(t, d), mod, bf(w_ffn1_gate[0]), bf(w_ffn1_up[0]),
              bf(w_ffn1_down[0]), sub=0)

    wqkvt = bf(w_in[0][:, :QKV_COLS].T)
    wrest = bf(w_in[0][:, QKV_COLS:])
    scale = HEAD_DIM ** -0.5 * LOG2E
    qg_rows = jnp.broadcast_to(
        jnp.tile(q_norm_g[0] * scale, QK_WIDTH // HEAD_DIM)[:, None], (QK_WIDTH, LANES))
    kg_rows = jnp.broadcast_to(
        jnp.tile(k_norm_g[0], QK_WIDTH // HEAD_DIM)[:, None], (QK_WIDTH, LANES))
    bs_full = jnp.repeat(b_spatial[0].T, GROUP_DIM, axis=1)
    qt, k, vt, ub, ga, gb = _inproj(
        x1.reshape(b, s, d), mod, wqkvt, wrest, qg_rows, kg_rows,
        gmlp_ln_g[0].reshape(1, -1), gmlp_ln_b[0].reshape(1, -1), w_spatial[0], bs_full)

    lam = jnp.stack([lam_q1[0], lam_k1[0], lam_q2[0], lam_k2[0]])
    kk = np.arange(TQ)[:, None]
    qq = np.arange(TQ)[None, :]
    buckets = jnp.asarray(np.stack([_rel_buckets(qq - kk), _rel_buckets(TQ + qq - kk)]))
    subg_row = (subln_g[0] * (1.0 - LAM_INIT)).reshape(1, V_DIM)
    o = _attention(rel_bias_table, lam, qt, k, vt, buckets, subg_row)

    x2 = _mix(x1, mod, o.reshape(t, ATTN_WIDTH), ub.reshape(t, GMLP_WIDTH),
              ga.reshape(t, d), gb.reshape(t, d),
              bf(w_a_proj[0]), bf(w_b_proj[0]), bf(w_o[0]))

    x3 = _ffn(x2, mod, bf(w_ffn2_gate[0]), bf(w_ffn2_up[0]), bf(w_ffn2_down[0]), sub=2)
    return x3.reshape(b, s, d)
```

```python
import functools
import math

import numpy as np
import jax
import jax.numpy as jnp
from jax import lax
from jax.experimental import pallas as pl
from jax.experimental.pallas import tpu as pltpu

F32 = jnp.float32
BF16 = jnp.bfloat16

D_MODEL = 1024
BATCH = 8
SEQ = 2048
N_HEADS = 8
HEAD_DIM = 64
V_DIM = 2 * HEAD_DIM
QK_WIDTH = N_HEADS * 2 * HEAD_DIM
ATTN_WIDTH = N_HEADS * V_DIM
N_GROUPS = 8
CHUNK = 128
GMLP_WIDTH = 1024
GROUP_DIM = GMLP_WIDTH // N_GROUPS
N_BUCKETS = 32
MAX_DISTANCE = 128
D_FF = 2816
N_MOD = 9
EPS = 1e-6
LAM_INIT = 0.8 - 0.6 * math.exp(-0.3 * 0)
LOG2E = math.log2(math.e)
QKV_COLS = 2 * QK_WIDTH + ATTN_WIDTH
REST_COLS = 2 * GMLP_WIDTH + 2 * D_MODEL

LANES = 128
VMEM_LIMIT = 56 * 1024 * 1024

TM_FFN = 256
TM_PROJ = 256
TM_MIX = 512
TQ = 256
MOD_TN = 1536
SUM_ROWS = 16
QK_AHEAD = 2


def _const_spec(shape):
    nd = len(shape)
    return pl.BlockSpec(shape, lambda *_: (0,) * nd, pipeline_mode=pl.Buffered(1))


def _silu(x):
    return x * jax.nn.sigmoid(x)


def _gelu(x):
    return 0.5 * x * (1.0 + lax.erf(x * (1.0 / math.sqrt(2.0))))


def _modulated_rmsnorm(x, shift, scale):
    ms = jnp.mean(x * x, axis=-1, keepdims=True)
    return (x * lax.rsqrt(ms + EPS)) * (1.0 + scale) + shift


def _mod_kernel(c_ref, w_ref, b_ref, o_ref):
    a = _silu(c_ref[...]).astype(BF16)
    o_ref[...] = jnp.dot(a, w_ref[...].astype(BF16),
                         preferred_element_type=F32) + b_ref[...]


def _adaln_mod(c, w_ada, b_ada):
    n = w_ada.shape[1]
    return pl.pallas_call(
        _mod_kernel,
        grid=(n // MOD_TN,),
        in_specs=[
            pl.BlockSpec((BATCH, D_MODEL), lambda j: (0, 0)),
            pl.BlockSpec((D_MODEL, MOD_TN), lambda j: (0, j)),
            pl.BlockSpec((1, MOD_TN), lambda j: (0, j)),
        ],
        out_specs=pl.BlockSpec((BATCH, MOD_TN), lambda j: (0, j)),
        out_shape=jax.ShapeDtypeStruct((BATCH, n), F32),
        compiler_params=pltpu.CompilerParams(
            dimension_semantics=("arbitrary",), vmem_limit_bytes=VMEM_LIMIT),
        name="adaln_mod",
    )(c, w_ada, b_ada.reshape(1, n))


def _ffn_kernel(x_ref, mod_ref, wg_ref, wu_ref, wd_ref, o_ref, *, sub):
    x = x_ref[...]
    shift = mod_ref[0, 3 * sub + 0:3 * sub + 1, :]
    scale = mod_ref[0, 3 * sub + 1:3 * sub + 2, :]
    gate = mod_ref[0, 3 * sub + 2:3 * sub + 3, :]
    h = _modulated_rmsnorm(x, shift, scale).astype(BF16)
    g = jnp.dot(h, wg_ref[...], preferred_element_type=F32)
    u = jnp.dot(h, wu_ref[...], preferred_element_type=F32)
    a = (_silu(g) * u).astype(BF16)
    y = jnp.dot(a, wd_ref[...], preferred_element_type=F32)
    o_ref[...] = x + (0.5 * gate) * y


def _ffn(x2d, mod, wg, wu, wd, *, sub):
    t = x2d.shape[0]
    tiles_per_batch = SEQ // TM_FFN
    return pl.pallas_call(
        functools.partial(_ffn_kernel, sub=sub),
        grid=(t // TM_FFN,),
        in_specs=[
            pl.BlockSpec((TM_FFN, D_MODEL), lambda i: (i, 0)),
            pl.BlockSpec((1, N_MOD, D_MODEL), lambda i: (i // tiles_per_batch, 0, 0)),
            _const_spec((D_MODEL, D_FF)),
            _const_spec((D_MODEL, D_FF)),
            _const_spec((D_FF, D_MODEL)),
        ],
        out_specs=pl.BlockSpec((TM_FFN, D_MODEL), lambda i: (i, 0)),
        out_shape=jax.ShapeDtypeStruct((t, D_MODEL), F32),
        compiler_params=pltpu.CompilerParams(
            dimension_semantics=("arbitrary",), vmem_limit_bytes=VMEM_LIMIT),
        name=f"ffn{sub}",
    )(x2d, mod, wg, wu, wd)


def _chunk_rmsnorm_rows(xt, gain_rows):
    width, tm = xt.shape
    x3 = xt.reshape(width // HEAD_DIM, HEAD_DIM, tm)
    ms = jnp.mean(x3 * x3, axis=1, keepdims=True)
    xn = (x3 * lax.rsqrt(ms + EPS)).reshape(width, tm)
    return jnp.concatenate(
        [xn[:, j * LANES:(j + 1) * LANES] * gain_rows for j in range(tm // LANES)], axis=1)


def _inproj_kernel(x_ref, mod_ref, wqkvt_ref, wrest_ref, qg_ref, kg_ref,
                   lng_ref, lnb_ref, ws_ref, bs_ref,
                   qt_ref, k_ref, vt_ref, ub_ref, ga_ref, gb_ref):
    x = x_ref[...]
    shift = mod_ref[0, 3:4, :]
    scale = mod_ref[0, 4:5, :]
    h = _modulated_rmsnorm(x, shift, scale).astype(BF16)

    pt = lax.dot_general(wqkvt_ref[...], h, (((1,), (1,)), ((), ())),
                         preferred_element_type=F32)
    qt = _chunk_rmsnorm_rows(pt[0:QK_WIDTH], qg_ref[...])
    qt_ref[0] = qt.astype(BF16)
    kt = _chunk_rmsnorm_rows(pt[QK_WIDTH:2 * QK_WIDTH], kg_ref[...])
    k_ref[0] = kt.T.astype(BF16)
    vt_ref[0] = pt[2 * QK_WIDTH:QKV_COLS].astype(BF16)

    rest = jnp.dot(h, wrest_ref[...], preferred_element_type=F32)
    u = _gelu(rest[:, 0:GMLP_WIDTH])
    gv = _gelu(rest[:, GMLP_WIDTH:2 * GMLP_WIDTH])
    ga_ref[0] = jax.nn.sigmoid(rest[:, 2 * GMLP_WIDTH:2 * GMLP_WIDTH + D_MODEL])
    gb_ref[0] = jax.nn.sigmoid(rest[:, 2 * GMLP_WIDTH + D_MODEL:REST_COLS])

    mu = jnp.mean(gv, axis=-1, keepdims=True)
    cen = gv - mu
    var = jnp.mean(cen * cen, axis=-1, keepdims=True)
    vln = ((cen * lax.rsqrt(var + EPS)) * lng_ref[...] + lnb_ref[...]).astype(BF16)

    tm = x.shape[0]
    row = lax.broadcasted_iota(jnp.int32, (CHUNK, CHUNK), 0)
    col = lax.broadcasted_iota(jnp.int32, (CHUNK, CHUNK), 1)
    causal = row >= col
    for g in range(N_GROUPS):
        w = jnp.where(causal, ws_ref[g], 0.0).astype(BF16)
        cs = slice(g * GROUP_DIM, (g + 1) * GROUP_DIM)
        for n in range(tm // CHUNK):
            rs = slice(n * CHUNK, (n + 1) * CHUNK)
            f = jnp.dot(w, vln[rs, cs], preferred_element_type=F32) + bs_ref[:, cs]
            ub_ref[0, rs, cs] = (u[rs, cs] * f).astype(BF16)


def _inproj(x3d, mod, wqkvt, wrest, qg_rows, kg_rows, ln_g, ln_b, w_spatial, bs_full):
    nt = SEQ // TM_PROJ
    tok_spec = lambda width: pl.BlockSpec((1, TM_PROJ, width), lambda b, i: (b, i, 0))
    feat_spec = lambda width: pl.BlockSpec((1, width, TM_PROJ), lambda b, i: (b, 0, i))
    return pl.pallas_call(
        _inproj_kernel,
        grid=(BATCH, nt),
        in_specs=[
            pl.BlockSpec((None, TM_PROJ, D_MODEL), lambda b, i: (b, i, 0)),
            pl.BlockSpec((1, N_MOD, D_MODEL), lambda b, i: (b, 0, 0)),
            _const_spec((QKV_COLS, D_MODEL)),
            _const_spec((D_MODEL, REST_COLS)),
            _const_spec((QK_WIDTH, LANES)),
            _const_spec((QK_WIDTH, LANES)),
            _const_spec((1, GMLP_WIDTH)),
            _const_spec((1, GMLP_WIDTH)),
            _const_spec((N_GROUPS, CHUNK, CHUNK)),
            _const_spec((CHUNK, GMLP_WIDTH)),
        ],
        out_specs=[
            feat_spec(QK_WIDTH), tok_spec(QK_WIDTH), feat_spec(ATTN_WIDTH),
            tok_spec(GMLP_WIDTH), tok_spec(D_MODEL), tok_spec(D_MODEL),
        ],
        out_shape=[
            jax.ShapeDtypeStruct((BATCH, QK_WIDTH, SEQ), BF16),
            jax.ShapeDtypeStruct((BATCH, SEQ, QK_WIDTH), BF16),
            jax.ShapeDtypeStruct((BATCH, ATTN_WIDTH, SEQ), BF16),
            jax.ShapeDtypeStruct((BATCH, SEQ, GMLP_WIDTH), BF16),
            jax.ShapeDtypeStruct((BATCH, SEQ, D_MODEL), F32),
            jax.ShapeDtypeStruct((BATCH, SEQ, D_MODEL), F32),
        ],
        compiler_params=pltpu.CompilerParams(
            dimension_semantics=("arbitrary", "arbitrary"), vmem_limit_bytes=VMEM_LIMIT),
        name="inproj",
    )(x3d, mod, wqkvt, wrest, qg_rows, kg_rows, ln_g, ln_b, w_spatial, bs_full)


def _rel_buckets(dist):
    n = np.maximum(dist, 0)
    max_exact = N_BUCKETS // 2
    nf = np.maximum(n, 1).astype(np.float32)
    large = max_exact + (np.log(nf / np.float32(max_exact))
                         / np.float32(math.log(MAX_DISTANCE / max_exact))
                         * np.float32(N_BUCKETS - max_exact)).astype(np.int32)
    large = np.minimum(large, N_BUCKETS - 1)
    return np.where(n < max_exact, n, large).astype(np.int32)


def _attn_kernel(table_ref, lam_ref, qt_ref, k_ref, vt_ref, bkt_ref, subg_ref,
                 o_ref, bias_ref):
    h = pl.program_id(0)
    b = pl.program_id(1)

    @pl.when(b == 0)
    def _():
        far = table_ref[N_BUCKETS - 1, h]
        key = lax.broadcasted_iota(jnp.int32, (TQ, TQ), 0)
        qry = lax.broadcasted_iota(jnp.int32, (TQ, TQ), 1)
        for t in range(2):
            bkt = bkt_ref[t]
            tile = jnp.zeros((TQ, TQ), F32)
            for bucket in range(N_BUCKETS - 1):
                tile = jnp.where(bkt == bucket, (table_ref[bucket, h] - far) * LOG2E, tile)
            if t == 0:
                tile = jnp.where(key <= qry, tile, -jnp.inf)
            bias_ref[t] = jnp.concatenate([tile, tile], axis=1)

    lp = lam_ref[...]
    lam = (jnp.exp(jnp.sum(lp[0:1] * lp[1:2], axis=1, keepdims=True))
           - jnp.exp(jnp.sum(lp[2:3] * lp[3:4], axis=1, keepdims=True)) + LAM_INIT)
    subg = subg_ref[...]
    zeros = jnp.zeros((HEAD_DIM, TQ), BF16)

    ones = jnp.ones((SUM_ROWS, TQ), BF16)
    n_strips = 2 * TQ // LANES
    nq = SEQ // TQ
    blocks = [(qi, j) for qi in range(nq) for j in range(qi + 1)]
    qpads = {}

    def scores(qi, j):
        if qi not in qpads:
            qt = qt_ref[0, :, qi * TQ:(qi + 1) * TQ]
            qpads[qi] = jnp.concatenate(
                [jnp.concatenate([qt[0:HEAD_DIM], zeros], axis=0),
                 jnp.concatenate([zeros, qt[HEAD_DIM:2 * HEAD_DIM]], axis=0)], axis=1)
        return jnp.dot(k_ref[0, j * TQ:(j + 1) * TQ, :], qpads[qi],
                       preferred_element_type=F32)

    pending = [scores(*blk) for blk in blocks[:QK_AHEAD]]
    m = acc = None
    for n, (qi, j) in enumerate(blocks):
        s = pending.pop(0)
        if n + QK_AHEAD < len(blocks):
            pending.append(scores(*blocks[n + QK_AHEAD]))
        if j == 0:
            m = [None] * n_strips
            acc = [None] * n_strips
        p, alpha = [], []
        for c in range(n_strips):
            sc = s[:, c * LANES:(c + 1) * LANES]
            if j == qi:
                sc = sc + bias_ref[0, :, c * LANES:(c + 1) * LANES]
            elif j == qi - 1:
                sc = sc + bias_ref[1, :, c * LANES:(c + 1) * LANES]
            blk_max = jnp.max(sc, axis=0, keepdims=True)
            m_new = blk_max if j == 0 else jnp.maximum(m[c], blk_max)
            p.append(jnp.exp2(sc - m_new).astype(BF16))
            alpha.append(None if j == 0 else jnp.exp2(m[c] - m_new))
            m[c] = m_new
        vt_aug = jnp.concatenate([vt_ref[0, :, j * TQ:(j + 1) * TQ], ones], axis=0)
        pv = jnp.dot(vt_aug, jnp.concatenate(p, axis=1),
                     preferred_element_type=F32)
        for c in range(n_strips):
            pvc = pv[0:V_DIM + 8, c * LANES:(c + 1) * LANES]
            acc[c] = pvc if j == 0 else alpha[c] * acc[c] + pvc
        if j == qi:
            a = jnp.concatenate(acc, axis=1)
            inv = 1.0 / a[V_DIM:V_DIM + 1]
            ot = a[0:V_DIM, 0:TQ] * inv[:, 0:TQ] - a[0:V_DIM, TQ:2 * TQ] * (lam * inv[:, TQ:2 * TQ])
            ms = jnp.mean(ot * ot, axis=0, keepdims=True)
            on = (ot * lax.rsqrt(ms + EPS)).T
            o_ref[0, qi * TQ:(qi + 1) * TQ, :] = (on * subg).astype(BF16)


def _attention(table, lam, qt, k, vt, buckets, subg_row):
    return pl.pallas_call(
        _attn_kernel,
        grid=(N_HEADS, BATCH),
        in_specs=[
            pl.BlockSpec(memory_space=pltpu.SMEM),
            _const_spec((4, HEAD_DIM)),
            pl.BlockSpec((1, 2 * HEAD_DIM, SEQ), lambda h, b: (b, h, 0)),
            pl.BlockSpec((1, SEQ, 2 * HEAD_DIM), lambda h, b: (b, 0, h)),
            pl.BlockSpec((1, V_DIM, SEQ), lambda h, b: (b, h, 0)),
            _const_spec((2, TQ, TQ)),
            _const_spec((1, V_DIM)),
        ],
        out_specs=pl.BlockSpec((1, SEQ, V_DIM), lambda h, b: (b, 0, h)),
        out_shape=jax.ShapeDtypeStruct((BATCH, SEQ, ATTN_WIDTH), BF16),
        scratch_shapes=[pltpu.VMEM((2, TQ, 2 * TQ), F32)],
        compiler_params=pltpu.CompilerParams(
            dimension_semantics=("arbitrary", "arbitrary"),
            vmem_limit_bytes=VMEM_LIMIT),
        name="diff_attn",
    )(table, lam, qt, k, vt, buckets, subg_row)


def _mix_kernel(x_ref, mod_ref, o_ref, ub_ref, ga_ref, gb_ref, wa_ref, wb_ref, wo_ref,
                out_ref):
    ya = jnp.dot(o_ref[...], wa_ref[...], preferred_element_type=F32)
    yb = jnp.dot(ub_ref[...], wb_ref[...], preferred_element_type=F32)
    merged = (ga_ref[...] * ya + gb_ref[...] * yb).astype(BF16)
    z = jnp.dot(merged, wo_ref[...], preferred_element_type=F32)
    out_ref[...] = x_ref[...] + mod_ref[0, 5:6, :] * z


def _mix(x2d, mod, o2d, ub2d, ga2d, gb2d, wa, wb, wo):
    t = x2d.shape[0]
    tiles_per_batch = SEQ // TM_MIX
    tok = lambda: pl.BlockSpec((TM_MIX, D_MODEL), lambda i: (i, 0))
    return pl.pallas_call(
        _mix_kernel,
        grid=(t // TM_MIX,),
        in_specs=[
            tok(),
            pl.BlockSpec((1, N_MOD, D_MODEL), lambda i: (i // tiles_per_batch, 0, 0)),
            tok(), tok(), tok(), tok(),
            _const_spec((ATTN_WIDTH, D_MODEL)),
            _const_spec((GMLP_WIDTH, D_MODEL)),
            _const_spec((D_MODEL, D_MODEL)),
        ],
        out_specs=tok(),
        out_shape=jax.ShapeDtypeStruct((t, D_MODEL), F32),
        compiler_params=pltpu.CompilerParams(
            dimension_semantics=("arbitrary",), vmem_limit_bytes=VMEM_LIMIT),
        name="mix_out",
    )(x2d, mod, o2d, ub2d, ga2d, gb2d, wa, wb, wo)


def kernel(x, c, w_ada, b_ada, w_ffn1_gate, w_ffn1_up, w_ffn1_down, w_in, q_norm_g, k_norm_g, lam_q1, lam_k1, lam_q2, lam_k2, subln_g, rel_bias_table, gmlp_ln_g, gmlp_ln_b, w_spatial, b_spatial, w_a_proj, w_b_proj, w_o, w_ffn2_gate, w_ffn2_up, w_ffn2_down):
    b, s, d = x.shape
    t = b * s
    bf = lambda w: w.astype(BF16)

    mod = _adaln_mod(c, w_ada[0], b_ada[0]).reshape(b, N_MOD, d)

    x1 = _ffn(x.reshape(t, d), mod, bf(w_ffn1_gate[0]), bf(w_ffn1_up[0]),
              bf(w_ffn1_down[0]), sub=0)

    wqkvt = bf(w_in[0][:, :QKV_COLS].T)
    wrest = bf(w_in[0][:, QKV_COLS:])
    scale = HEAD_DIM ** -0.5 * LOG2E
    qg_rows = jnp.broadcast_to(
        jnp.tile(q_norm_g[0] * scale, QK_WIDTH // HEAD_DIM)[:, None], (QK_WIDTH, LANES))
    kg_rows = jnp.broadcast_to(
        jnp.tile(k_norm_g[0], QK_WIDTH // HEAD_DIM)[:, None], (QK_WIDTH, LANES))
    bs_full = jnp.repeat(b_spatial[0].T, GROUP_DIM, axis=1)
    qt, k, vt, ub, ga, gb = _inproj(
        x1.reshape(b, s, d), mod, wqkvt, wrest, qg_rows, kg_rows,
        gmlp_ln_g[0].reshape(1, -1), gmlp_ln_b[0].reshape(1, -1), w_spatial[0], bs_full)

    lam = jnp.stack([lam_q1[0], lam_k1[0], lam_q2[0], lam_k2[0]])
    kk = np.arange(TQ)[:, None]
    qq = np.arange(TQ)[None, :]
    buckets = jnp.asarray(np.stack([_rel_buckets(qq - kk), _rel_buckets(TQ + qq - kk)]))
    subg_row = (subln_g[0] * (1.0 - LAM_INIT)).reshape(1, V_DIM)
    o = _attention(rel_bias_table, lam, qt, k, vt, buckets, subg_row)

    x2 = _mix(x1, mod, o.reshape(t, ATTN_WIDTH), ub.reshape(t, GMLP_WIDTH),
              ga.reshape(t, d), gb.reshape(t, d),
              bf(w_a_proj[0]), bf(w_b_proj[0]), bf(w_o[0]))

    x3 = _ffn(x2, mod, bf(w_ffn2_gate[0]), bf(w_ffn2_up[0]), bf(w_ffn2_down[0]), sub=2)
    return x3.reshape(b, s, d)
```

```python
import functools
import math

import numpy as np
import jax
import jax.numpy as jnp
from jax import lax
from jax.experimental import pallas as pl
from jax.experimental.pallas import tpu as pltpu

F32 = jnp.float32
BF16 = jnp.bfloat16

D_MODEL = 1024
BATCH = 8
SEQ = 2048
N_HEADS = 8
HEAD_DIM = 64
V_DIM = 2 * HEAD_DIM
QK_WIDTH = N_HEADS * 2 * HEAD_DIM
ATTN_WIDTH = N_HEADS * V_DIM
N_GROUPS = 8
CHUNK = 128
GMLP_WIDTH = 1024
GROUP_DIM = GMLP_WIDTH // N_GROUPS
N_BUCKETS = 32
MAX_DISTANCE = 128
D_FF = 2816
N_MOD = 9
EPS = 1e-6
LAM_INIT = 0.8 - 0.6 * math.exp(-0.3 * 0)
LOG2E = math.log2(math.e)
QKV_COLS = 2 * QK_WIDTH + ATTN_WIDTH
REST_COLS = 2 * GMLP_WIDTH + 2 * D_MODEL

LANES = 128
VMEM_LIMIT = 56 * 1024 * 1024

TM_FFN = 256
TM_PROJ = 256
TM_MIX = 512
TQ = 256
MOD_TN = 1536
SUM_ROWS = 16
W_STEPS = 8
QK_AHEAD = 2


def _const_spec(shape):
    nd = len(shape)
    return pl.BlockSpec(shape, lambda *_: (0,) * nd, pipeline_mode=pl.Buffered(1))


def _weight_rows_spec(shape):
    rows, cols = shape
    return pl.BlockSpec((rows // W_STEPS, cols),
                        lambda i: (jnp.minimum(i, W_STEPS - 1), 0))


def _stash_rows(dst_ref, chunk_ref, i):
    rows = chunk_ref.shape[0]
    dst_ref[pl.ds(pl.multiple_of(i * rows, rows), rows), :] = chunk_ref[...].astype(BF16)


def _silu(x):
    return x * jax.nn.sigmoid(x)


def _gelu(x):
    return 0.5 * x * (1.0 + lax.erf(x * (1.0 / math.sqrt(2.0))))


def _modulated_rmsnorm(x, shift, scale):
    ms = jnp.mean(x * x, axis=-1, keepdims=True)
    return (x * lax.rsqrt(ms + EPS)) * (1.0 + scale) + shift


def _mod_kernel(c_ref, w_ref, b_ref, o_ref):
    a = _silu(c_ref[...]).astype(BF16)
    o_ref[...] = jnp.dot(a, w_ref[...].astype(BF16),
                         preferred_element_type=F32) + b_ref[...]


def _adaln_mod(c, w_ada, b_ada):
    n = w_ada.shape[1]
    return pl.pallas_call(
        _mod_kernel,
        grid=(n // MOD_TN,),
        in_specs=[
            pl.BlockSpec((BATCH, D_MODEL), lambda j: (0, 0)),
            pl.BlockSpec((D_MODEL, MOD_TN), lambda j: (0, j)),
            pl.BlockSpec((1, MOD_TN), lambda j: (0, j)),
        ],
        out_specs=pl.BlockSpec((BATCH, MOD_TN), lambda j: (0, j)),
        out_shape=jax.ShapeDtypeStruct((BATCH, n), F32),
        compiler_params=pltpu.CompilerParams(
            dimension_semantics=("arbitrary",), vmem_limit_bytes=VMEM_LIMIT),
        name="adaln_mod",
    )(c, w_ada, b_ada.reshape(1, n))


def _ffn_kernel(x_ref, mod_ref, wg_ref, wu_ref, wd_ref, o_ref, wg_s, wu_s, wd_s, *, sub):
    i = pl.program_id(0)

    @pl.when(i < W_STEPS)
    def _():
        _stash_rows(wg_s, wg_ref, i)
        _stash_rows(wu_s, wu_ref, i)
        _stash_rows(wd_s, wd_ref, i)

    @pl.when(i >= W_STEPS)
    def _():
        x = x_ref[...]
        shift = mod_ref[0, 3 * sub + 0:3 * sub + 1, :]
        scale = mod_ref[0, 3 * sub + 1:3 * sub + 2, :]
        gate = mod_ref[0, 3 * sub + 2:3 * sub + 3, :]
        h = _modulated_rmsnorm(x, shift, scale).astype(BF16)
        g = jnp.dot(h, wg_s[...], preferred_element_type=F32)
        u = jnp.dot(h, wu_s[...], preferred_element_type=F32)
        a = (_silu(g) * u).astype(BF16)
        y = jnp.dot(a, wd_s[...], preferred_element_type=F32)
        o_ref[...] = x + (0.5 * gate) * y


def _ffn(x2d, mod, wg, wu, wd, *, sub):
    t = x2d.shape[0]
    tiles_per_batch = SEQ // TM_FFN
    tile = lambda i: jnp.maximum(i - W_STEPS, 0)
    return pl.pallas_call(
        functools.partial(_ffn_kernel, sub=sub),
        grid=(W_STEPS + t // TM_FFN,),
        in_specs=[
            pl.BlockSpec((TM_FFN, D_MODEL), lambda i: (tile(i), 0)),
            pl.BlockSpec((1, N_MOD, D_MODEL), lambda i: (tile(i) // tiles_per_batch, 0, 0)),
            _weight_rows_spec(wg.shape),
            _weight_rows_spec(wu.shape),
            _weight_rows_spec(wd.shape),
        ],
        out_specs=pl.BlockSpec((TM_FFN, D_MODEL), lambda i: (tile(i), 0)),
        out_shape=jax.ShapeDtypeStruct((t, D_MODEL), F32),
        scratch_shapes=[pltpu.VMEM(wg.shape, BF16), pltpu.VMEM(wu.shape, BF16),
                        pltpu.VMEM(wd.shape, BF16)],
        compiler_params=pltpu.CompilerParams(
            dimension_semantics=("arbitrary",), vmem_limit_bytes=VMEM_LIMIT),
        name=f"ffn{sub}",
    )(x2d, mod, wg, wu, wd)


def _chunk_rmsnorm_rows(xt, gain_rows):
    width, tm = xt.shape
    x3 = xt.reshape(width // HEAD_DIM, HEAD_DIM, tm)
    ms = jnp.mean(x3 * x3, axis=1, keepdims=True)
    xn = (x3 * lax.rsqrt(ms + EPS)).reshape(width, tm)
    return jnp.concatenate(
        [xn[:, j * LANES:(j + 1) * LANES] * gain_rows for j in range(tm // LANES)], axis=1)


def _inproj_tile(x_ref, mod_ref, wqkvt_ref, wrest_ref, qg_ref, kg_ref,
                 lng_ref, lnb_ref, ws_ref, bs_ref,
                 qt_ref, k_ref, vt_ref, ub_ref, ga_ref, gb_ref):
    x = x_ref[...]
    shift = mod_ref[0, 3:4, :]
    scale = mod_ref[0, 4:5, :]
    h = _modulated_rmsnorm(x, shift, scale).astype(BF16)

    pt = lax.dot_general(wqkvt_ref[...], h, (((1,), (1,)), ((), ())),
                         preferred_element_type=F32)
    qt = _chunk_rmsnorm_rows(pt[0:QK_WIDTH], qg_ref[...])
    qt_ref[0] = qt.astype(BF16)
    kt = _chunk_rmsnorm_rows(pt[QK_WIDTH:2 * QK_WIDTH], kg_ref[...])
    k_ref[0] = kt.T.astype(BF16)
    vt_ref[0] = pt[2 * QK_WIDTH:QKV_COLS].astype(BF16)

    rest = jnp.dot(h, wrest_ref[...], preferred_element_type=F32)
    u = _gelu(rest[:, 0:GMLP_WIDTH])
    gv = _gelu(rest[:, GMLP_WIDTH:2 * GMLP_WIDTH])
    ga_ref[0] = jax.nn.sigmoid(rest[:, 2 * GMLP_WIDTH:2 * GMLP_WIDTH + D_MODEL])
    gb_ref[0] = jax.nn.sigmoid(rest[:, 2 * GMLP_WIDTH + D_MODEL:REST_COLS])

    mu = jnp.mean(gv, axis=-1, keepdims=True)
    cen = gv - mu
    var = jnp.mean(cen * cen, axis=-1, keepdims=True)
    vln = ((cen * lax.rsqrt(var + EPS)) * lng_ref[...] + lnb_ref[...]).astype(BF16)

    tm = x.shape[0]
    row = lax.broadcasted_iota(jnp.int32, (CHUNK, CHUNK), 0)
    col = lax.broadcasted_iota(jnp.int32, (CHUNK, CHUNK), 1)
    causal = row >= col
    for g in range(N_GROUPS):
        w = jnp.where(causal, ws_ref[g], 0.0).astype(BF16)
        cs = slice(g * GROUP_DIM, (g + 1) * GROUP_DIM)
        for n in range(tm // CHUNK):
            rs = slice(n * CHUNK, (n + 1) * CHUNK)
            f = jnp.dot(w, vln[rs, cs], preferred_element_type=F32) + bs_ref[:, cs]
            ub_ref[0, rs, cs] = (u[rs, cs] * f).astype(BF16)


def _inproj_kernel(x_ref, mod_ref, win_ref, qg_ref, kg_ref, lng_ref, lnb_ref, ws_ref, bs_ref,
                   qt_ref, k_ref, vt_ref, ub_ref, ga_ref, gb_ref, wqkvt_s, wrest_s):
    i = pl.program_id(0)

    @pl.when(i < W_STEPS)
    def _():
        rows = win_ref.shape[0]
        r = pl.multiple_of(i * rows, rows)
        chunk = win_ref[...]
        wqkvt_s[:, pl.ds(r, rows)] = chunk[:, 0:QKV_COLS].T.astype(BF16)
        wrest_s[pl.ds(r, rows), :] = chunk[:, QKV_COLS:QKV_COLS + REST_COLS].astype(BF16)

    @pl.when(i >= W_STEPS)
    def _():
        _inproj_tile(x_ref, mod_ref, wqkvt_s, wrest_s, qg_ref, kg_ref, lng_ref, lnb_ref,
                     ws_ref, bs_ref, qt_ref, k_ref, vt_ref, ub_ref, ga_ref, gb_ref)


def _inproj(x3d, mod, w_in, qg_rows, kg_rows, ln_g, ln_b, w_spatial, bs_full):
    nt = SEQ // TM_PROJ
    tile = lambda i: jnp.maximum(i - W_STEPS, 0)
    tok_spec = lambda width: pl.BlockSpec(
        (1, TM_PROJ, width), lambda i: (tile(i) // nt, tile(i) % nt, 0))
    feat_spec = lambda width: pl.BlockSpec(
        (1, width, TM_PROJ), lambda i: (tile(i) // nt, 0, tile(i) % nt))
    return pl.pallas_call(
        _inproj_kernel,
        grid=(W_STEPS + BATCH * nt,),
        in_specs=[
            pl.BlockSpec((None, TM_PROJ, D_MODEL), lambda i: (tile(i) // nt, tile(i) % nt, 0)),
            pl.BlockSpec((1, N_MOD, D_MODEL), lambda i: (tile(i) // nt, 0, 0)),
            _weight_rows_spec(w_in.shape),
            _const_spec((QK_WIDTH, LANES)),
            _const_spec((QK_WIDTH, LANES)),
            _const_spec((1, GMLP_WIDTH)),
            _const_spec((1, GMLP_WIDTH)),
            _const_spec((N_GROUPS, CHUNK, CHUNK)),
            _const_spec((CHUNK, GMLP_WIDTH)),
        ],
        out_specs=[
            feat_spec(QK_WIDTH), tok_spec(QK_WIDTH), feat_spec(ATTN_WIDTH),
            tok_spec(GMLP_WIDTH), tok_spec(D_MODEL), tok_spec(D_MODEL),
        ],
        out_shape=[
            jax.ShapeDtypeStruct((BATCH, QK_WIDTH, SEQ), BF16),
            jax.ShapeDtypeStruct((BATCH, SEQ, QK_WIDTH), BF16),
            jax.ShapeDtypeStruct((BATCH, ATTN_WIDTH, SEQ), BF16),
            jax.ShapeDtypeStruct((BATCH, SEQ, GMLP_WIDTH), BF16),
            jax.ShapeDtypeStruct((BATCH, SEQ, D_MODEL), F32),
            jax.ShapeDtypeStruct((BATCH, SEQ, D_MODEL), F32),
        ],
        scratch_shapes=[pltpu.VMEM((QKV_COLS, D_MODEL), BF16),
                        pltpu.VMEM((D_MODEL, REST_COLS), BF16)],
        compiler_params=pltpu.CompilerParams(
            dimension_semantics=("arbitrary",), vmem_limit_bytes=VMEM_LIMIT),
        name="inproj",
    )(x3d, mod, w_in, qg_rows, kg_rows, ln_g, ln_b, w_spatial, bs_full)


def _rel_buckets(dist):
    n = np.maximum(dist, 0)
    max_exact = N_BUCKETS // 2
    nf = np.maximum(n, 1).astype(np.float32)
    large = max_exact + (np.log(nf / np.float32(max_exact))
                         / np.float32(math.log(MAX_DISTANCE / max_exact))
                         * np.float32(N_BUCKETS - max_exact)).astype(np.int32)
    large = np.minimum(large, N_BUCKETS - 1)
    return np.where(n < max_exact, n, large).astype(np.int32)


def _attn_kernel(table_ref, lam_ref, qt_ref, k_ref, vt_ref, bkt_ref, subg_ref,
                 o_ref, bias_ref):
    h = pl.program_id(0)
    b = pl.program_id(1)

    @pl.when(b == 0)
    def _():
        far = table_ref[N_BUCKETS - 1, h]
        key = lax.broadcasted_iota(jnp.int32, (TQ, TQ), 0)
        qry = lax.broadcasted_iota(jnp.int32, (TQ, TQ), 1)
        for t in range(2):
            bkt = bkt_ref[t]
            tile = jnp.zeros((TQ, TQ), F32)
            for bucket in range(N_BUCKETS - 1):
                tile = jnp.where(bkt == bucket, (table_ref[bucket, h] - far) * LOG2E, tile)
            if t == 0:
                tile = jnp.where(key <= qry, tile, -jnp.inf)
            bias_ref[t] = jnp.concatenate([tile, tile], axis=1)

    lp = lam_ref[...]
    lam = (jnp.exp(jnp.sum(lp[0:1] * lp[1:2], axis=1, keepdims=True))
           - jnp.exp(jnp.sum(lp[2:3] * lp[3:4], axis=1, keepdims=True)) + LAM_INIT)
    subg = subg_ref[...]
    zeros = jnp.zeros((HEAD_DIM, TQ), BF16)

    ones = jnp.ones((SUM_ROWS, TQ), BF16)
    n_strips = 2 * TQ // LANES
    nq = SEQ // TQ
    blocks = [(qi, j) for qi in range(nq) for j in range(qi + 1)]
    qpads = {}

    def scores(qi, j):
        if qi not in qpads:
            qt = qt_ref[0, :, qi * TQ:(qi + 1) * TQ]
            qpads[qi] = jnp.concatenate(
                [jnp.concatenate([qt[0:HEAD_DIM], zeros], axis=0),
                 jnp.concatenate([zeros, qt[HEAD_DIM:2 * HEAD_DIM]], axis=0)], axis=1)
        return jnp.dot(k_ref[0, j * TQ:(j + 1) * TQ, :], qpads[qi],
                       preferred_element_type=F32)

    pending = [scores(*blk) for blk in blocks[:QK_AHEAD]]
    m = acc = None
    for n, (qi, j) in enumerate(blocks):
        s = pending.pop(0)
        if n + QK_AHEAD < len(blocks):
            pending.append(scores(*blocks[n + QK_AHEAD]))
        if j == 0:
            m = [None] * n_strips
            acc = [None] * n_strips
        p, alpha = [], []
        for c in range(n_strips):
            sc = s[:, c * LANES:(c + 1) * LANES]
            if j == qi:
                sc = sc + bias_ref[0, :, c * LANES:(c + 1) * LANES]
            elif j == qi - 1:
                sc = sc + bias_ref[1, :, c * LANES:(c + 1) * LANES]
            blk_max = jnp.max(sc, axis=0, keepdims=True)
            m_new = blk_max if j == 0 else jnp.maximum(m[c], blk_max)
            p.append(jnp.exp2(sc - m_new).astype(BF16))
            alpha.append(None if j == 0 else jnp.exp2(m[c] - m_new))
            m[c] = m_new
        vt_aug = jnp.concatenate([vt_ref[0, :, j * TQ:(j + 1) * TQ], ones], axis=0)
        pv = jnp.dot(vt_aug, jnp.concatenate(p, axis=1),
                     preferred_element_type=F32)
        for c in range(n_strips):
            pvc = pv[0:V_DIM + 8, c * LANES:(c + 1) * LANES]
            acc[c] = pvc if j == 0 else alpha[c] * acc[c] + pvc
        if j == qi:
            a = jnp.concatenate(acc, axis=1)
            inv = 1.0 / a[V_DIM:V_DIM + 1]
            ot = a[0:V_DIM, 0:TQ] * inv[:, 0:TQ] - a[0:V_DIM, TQ:2 * TQ] * (lam * inv[:, TQ:2 * TQ])
            ms = jnp.mean(ot * ot, axis=0, keepdims=True)
            on = (ot * lax.rsqrt(ms + EPS)).T
            o_ref[0, qi * TQ:(qi + 1) * TQ, :] = (on * subg).astype(BF16)


def _attention(table, lam, qt, k, vt, buckets, subg_row):
    return pl.pallas_call(
        _attn_kernel,
        grid=(N_HEADS, BATCH),
        in_specs=[
            pl.BlockSpec(memory_space=pltpu.SMEM),
            _const_spec((4, HEAD_DIM)),
            pl.BlockSpec((1, 2 * HEAD_DIM, SEQ), lambda h, b: (b, h, 0)),
            pl.BlockSpec((1, SEQ, 2 * HEAD_DIM), lambda h, b: (b, 0, h)),
            pl.BlockSpec((1, V_DIM, SEQ), lambda h, b: (b, h, 0)),
            _const_spec((2, TQ, TQ)),
            _const_spec((1, V_DIM)),
        ],
        out_specs=pl.BlockSpec((1, SEQ, V_DIM), lambda h, b: (b, 0, h)),
        out_shape=jax.ShapeDtypeStruct((BATCH, SEQ, ATTN_WIDTH), BF16),
        scratch_shapes=[pltpu.VMEM((2, TQ, 2 * TQ), F32)],
        compiler_params=pltpu.CompilerParams(
            dimension_semantics=("arbitrary", "arbitrary"),
            vmem_limit_bytes=VMEM_LIMIT),
        name="diff_attn",
    )(table, lam, qt, k, vt, buckets, subg_row)


def _mix_kernel(x_ref, mod_ref, o_ref, ub_ref, ga_ref, gb_ref, wa_ref, wb_ref, wo_ref,
                out_ref, wa_s, wb_s, wo_s):
    i = pl.program_id(0)

    @pl.when(i < W_STEPS)
    def _():
        _stash_rows(wa_s, wa_ref, i)
        _stash_rows(wb_s, wb_ref, i)
        _stash_rows(wo_s, wo_ref, i)

    @pl.when(i >= W_STEPS)
    def _():
        ya = jnp.dot(o_ref[...], wa_s[...], preferred_element_type=F32)
        yb = jnp.dot(ub_ref[...], wb_s[...], preferred_element_type=F32)
        merged = (ga_ref[...] * ya + gb_ref[...] * yb).astype(BF16)
        z = jnp.dot(merged, wo_s[...], preferred_element_type=F32)
        out_ref[...] = x_ref[...] + mod_ref[0, 5:6, :] * z


def _mix(x2d, mod, o2d, ub2d, ga2d, gb2d, wa, wb, wo):
    t = x2d.shape[0]
    tiles_per_batch = SEQ // TM_MIX
    tile = lambda i: jnp.maximum(i - W_STEPS, 0)
    tok = lambda: pl.BlockSpec((TM_MIX, D_MODEL), lambda i: (tile(i), 0))
    return pl.pallas_call(
        _mix_kernel,
        grid=(W_STEPS + t // TM_MIX,),
        in_specs=[
            tok(),
            pl.BlockSpec((1, N_MOD, D_MODEL), lambda i: (tile(i) // tiles_per_batch, 0, 0)),
            tok(), tok(), tok(), tok(),
            _weight_rows_spec(wa.shape),
            _weight_rows_spec(wb.shape),
            _weight_rows_spec(wo.shape),
        ],
        out_specs=tok(),
        out_shape=jax.ShapeDtypeStruct((t, D_MODEL), F32),
        scratch_shapes=[pltpu.VMEM(wa.shape, BF16), pltpu.VMEM(wb.shape, BF16),
                        pltpu.VMEM(wo.shape, BF16)],
        compiler_params=pltpu.CompilerParams(
            dimension_semantics=("arbitrary",), vmem_limit_bytes=VMEM_LIMIT),
        name="mix_out",
    )(x2d, mod, o2d, ub2d, ga2d, gb2d, wa, wb, wo)


def kernel(x, c, w_ada, b_ada, w_ffn1_gate, w_ffn1_up, w_ffn1_down, w_in, q_norm_g, k_norm_g, lam_q1, lam_k1, lam_q2, lam_k2, subln_g, rel_bias_table, gmlp_ln_g, gmlp_ln_b, w_spatial, b_spatial, w_a_proj, w_b_proj, w_o, w_ffn2_gate, w_ffn2_up, w_ffn2_down):
    b, s, d = x.shape
    t = b * s

    mod = _adaln_mod(c, w_ada[0], b_ada[0]).reshape(b, N_MOD, d)

    x1 = _ffn(x.reshape(t, d), mod, w_ffn1_gate[0], w_ffn1_up[0], w_ffn1_down[0], sub=0)

    scale = HEAD_DIM ** -0.5 * LOG2E
    qg_rows = jnp.broadcast_to(
        jnp.tile(q_norm_g[0] * scale, QK_WIDTH // HEAD_DIM)[:, None], (QK_WIDTH, LANES))
    kg_rows = jnp.broadcast_to(
        jnp.tile(k_norm_g[0], QK_WIDTH // HEAD_DIM)[:, None], (QK_WIDTH, LANES))
    bs_full = jnp.repeat(b_spatial[0].T, GROUP_DIM, axis=1)
    qt, k, vt, ub, ga, gb = _inproj(
        x1.reshape(b, s, d), mod, w_in[0], qg_rows, kg_rows,
        gmlp_ln_g[0].reshape(1, -1), gmlp_ln_b[0].reshape(1, -1), w_spatial[0], bs_full)

    lam = jnp.stack([lam_q1[0], lam_k1[0], lam_q2[0], lam_k2[0]])
    kk = np.arange(TQ)[:, None]
    qq = np.arange(TQ)[None, :]
    buckets = jnp.asarray(np.stack([_rel_buckets(qq - kk), _rel_buckets(TQ + qq - kk)]))
    subg_row = (subln_g[0] * (1.0 - LAM_INIT)).reshape(1, V_DIM)
    o = _attention(rel_bias_table, lam, qt, k, vt, buckets, subg_row)

    x2 = _mix(x1, mod, o.reshape(t, ATTN_WIDTH), ub.reshape(t, GMLP_WIDTH),
              ga.reshape(t, d), gb.reshape(t, d), w_a_proj[0], w_b_proj[0], w_o[0])

    x3 = _ffn(x2, mod, w_ffn2_gate[0], w_ffn2_up[0], w_ffn2_down[0], sub=2)
    return x3.reshape(b, s, d)
```

```python
import functools
import math

import numpy as np
import jax
import jax.numpy as jnp
from jax import lax
from jax.experimental import pallas as pl
from jax.experimental.pallas import tpu as pltpu

F32 = jnp.float32
BF16 = jnp.bfloat16

D_MODEL = 1024
BATCH = 8
SEQ = 2048
N_HEADS = 8
HEAD_DIM = 64
V_DIM = 2 * HEAD_DIM
QK_WIDTH = N_HEADS * 2 * HEAD_DIM
ATTN_WIDTH = N_HEADS * V_DIM
N_GROUPS = 8
CHUNK = 128
GMLP_WIDTH = 1024
GROUP_DIM = GMLP_WIDTH // N_GROUPS
N_BUCKETS = 32
MAX_DISTANCE = 128
D_FF = 2816
N_MOD = 9
EPS = 1e-6
LAM_INIT = 0.8 - 0.6 * math.exp(-0.3 * 0)
LOG2E = math.log2(math.e)
QKV_COLS = 2 * QK_WIDTH + ATTN_WIDTH
REST_COLS = 2 * GMLP_WIDTH + 2 * D_MODEL

LANES = 128
VMEM_LIMIT = 56 * 1024 * 1024

TM_FFN = 512
TM_PROJ = 256
TM_MIX = 512
TQ = 256
MOD_TN = 1536
SUM_ROWS = 16
W_STEPS = 8
QK_AHEAD = 2


def _const_spec(shape):
    nd = len(shape)
    return pl.BlockSpec(shape, lambda *_: (0,) * nd, pipeline_mode=pl.Buffered(1))


def _weight_rows_spec(shape):
    rows, cols = shape
    return pl.BlockSpec((rows // W_STEPS, cols),
                        lambda i: (jnp.minimum(i, W_STEPS - 1), 0))


def _stash_rows(dst_ref, chunk_ref, i):
    rows = chunk_ref.shape[0]
    dst_ref[pl.ds(pl.multiple_of(i * rows, rows), rows), :] = chunk_ref[...].astype(BF16)


def _silu(x):
    return x * jax.nn.sigmoid(x)


def _gelu(x):
    return 0.5 * x * (1.0 + lax.erf(x * (1.0 / math.sqrt(2.0))))


def _modulated_rmsnorm(x, shift, scale):
    ms = jnp.mean(x * x, axis=-1, keepdims=True)
    return (x * lax.rsqrt(ms + EPS)) * (1.0 + scale) + shift


def _mod_kernel(c_ref, w_ref, b_ref, o_ref):
    a = _silu(c_ref[...]).astype(BF16)
    o_ref[...] = jnp.dot(a, w_ref[...].astype(BF16),
                         preferred_element_type=F32) + b_ref[...]


def _adaln_mod(c, w_ada, b_ada):
    n = w_ada.shape[1]
    return pl.pallas_call(
        _mod_kernel,
        grid=(n // MOD_TN,),
        in_specs=[
            pl.BlockSpec((BATCH, D_MODEL), lambda j: (0, 0)),
            pl.BlockSpec((D_MODEL, MOD_TN), lambda j: (0, j)),
            pl.BlockSpec((1, MOD_TN), lambda j: (0, j)),
        ],
        out_specs=pl.BlockSpec((BATCH, MOD_TN), lambda j: (0, j)),
        out_shape=jax.ShapeDtypeStruct((BATCH, n), F32),
        compiler_params=pltpu.CompilerParams(
            dimension_semantics=("arbitrary",), vmem_limit_bytes=VMEM_LIMIT),
        name="adaln_mod",
    )(c, w_ada, b_ada.reshape(1, n))


def _ffn_kernel(x_ref, mod_ref, wg_ref, wu_ref, wd_ref, o_ref, wg_s, wu_s, wd_s, *, sub):
    i = pl.program_id(0)

    @pl.when(i < W_STEPS)
    def _():
        _stash_rows(wg_s, wg_ref, i)
        _stash_rows(wu_s, wu_ref, i)
        _stash_rows(wd_s, wd_ref, i)

    @pl.when(i >= W_STEPS)
    def _():
        x = x_ref[...]
        shift = mod_ref[0, 3 * sub + 0:3 * sub + 1, :]
        scale = mod_ref[0, 3 * sub + 1:3 * sub + 2, :]
        gate = mod_ref[0, 3 * sub + 2:3 * sub + 3, :]
        h = _modulated_rmsnorm(x, shift, scale).astype(BF16)
        g = jnp.dot(h, wg_s[...], preferred_element_type=F32)
        u = jnp.dot(h, wu_s[...], preferred_element_type=F32)
        a = (_silu(g) * u).astype(BF16)
        y = jnp.dot(a, wd_s[...], preferred_element_type=F32)
        o_ref[...] = x + (0.5 * gate) * y


def _ffn(x2d, mod, wg, wu, wd, *, sub):
    t = x2d.shape[0]
    tiles_per_batch = SEQ // TM_FFN
    tile = lambda i: jnp.maximum(i - W_STEPS, 0)
    return pl.pallas_call(
        functools.partial(_ffn_kernel, sub=sub),
        grid=(W_STEPS + t // TM_FFN,),
        in_specs=[
            pl.BlockSpec((TM_FFN, D_MODEL), lambda i: (tile(i), 0)),
            pl.BlockSpec((1, N_MOD, D_MODEL), lambda i: (tile(i) // tiles_per_batch, 0, 0)),
            _weight_rows_spec(wg.shape),
            _weight_rows_spec(wu.shape),
            _weight_rows_spec(wd.shape),
        ],
        out_specs=pl.BlockSpec((TM_FFN, D_MODEL), lambda i: (tile(i), 0)),
        out_shape=jax.ShapeDtypeStruct((t, D_MODEL), F32),
        scratch_shapes=[pltpu.VMEM(wg.shape, BF16), pltpu.VMEM(wu.shape, BF16),
                        pltpu.VMEM(wd.shape, BF16)],
        compiler_params=pltpu.CompilerParams(
            dimension_semantics=("arbitrary",), vmem_limit_bytes=VMEM_LIMIT),
        name=f"ffn{sub}",
    )(x2d, mod, wg, wu, wd)


def _chunk_rmsnorm_rows(xt, gain_rows):
    width, tm = xt.shape
    x3 = xt.reshape(width // HEAD_DIM, HEAD_DIM, tm)
    ms = jnp.mean(x3 * x3, axis=1, keepdims=True)
    xn = (x3 * lax.rsqrt(ms + EPS)).reshape(width, tm)
    return jnp.concatenate(
        [xn[:, j * LANES:(j + 1) * LANES] * gain_rows for j in range(tm // LANES)], axis=1)


def _inproj_tile(x_ref, mod_ref, wqkvt_ref, wrest_ref, qg_ref, kg_ref,
                 lng_ref, lnb_ref, ws_ref, bs_ref,
                 qt_ref, k_ref, vt_ref, ub_ref, ga_ref, gb_ref):
    x = x_ref[...]
    shift = mod_ref[0, 3:4, :]
    scale = mod_ref[0, 4:5, :]
    h = _modulated_rmsnorm(x, shift, scale).astype(BF16)

    pt = lax.dot_general(wqkvt_ref[...], h, (((1,), (1,)), ((), ())),
                         preferred_element_type=F32)
    qt = _chunk_rmsnorm_rows(pt[0:QK_WIDTH], qg_ref[...])
    qt_ref[0] = qt.astype(BF16)
    kt = _chunk_rmsnorm_rows(pt[QK_WIDTH:2 * QK_WIDTH], kg_ref[...])
    k_ref[0] = kt.T.astype(BF16)
    vt_ref[0] = pt[2 * QK_WIDTH:QKV_COLS].astype(BF16)

    rest = jnp.dot(h, wrest_ref[...], preferred_element_type=F32)
    u = _gelu(rest[:, 0:GMLP_WIDTH])
    gv = _gelu(rest[:, GMLP_WIDTH:2 * GMLP_WIDTH])
    ga_ref[0] = jax.nn.sigmoid(rest[:, 2 * GMLP_WIDTH:2 * GMLP_WIDTH + D_MODEL]).astype(BF16)
    gb_ref[0] = jax.nn.sigmoid(rest[:, 2 * GMLP_WIDTH + D_MODEL:REST_COLS]).astype(BF16)

    mu = jnp.mean(gv, axis=-1, keepdims=True)
    cen = gv - mu
    var = jnp.mean(cen * cen, axis=-1, keepdims=True)
    vln = ((cen * lax.rsqrt(var + EPS)) * lng_ref[...] + lnb_ref[...]).astype(BF16)

    tm = x.shape[0]
    row = lax.broadcasted_iota(jnp.int32, (CHUNK, CHUNK), 0)
    col = lax.broadcasted_iota(jnp.int32, (CHUNK, CHUNK), 1)
    causal = row >= col
    for g in range(N_GROUPS):
        w = jnp.where(causal, ws_ref[g], 0.0).astype(BF16)
        cs = slice(g * GROUP_DIM, (g + 1) * GROUP_DIM)
        for n in range(tm // CHUNK):
            rs = slice(n * CHUNK, (n + 1) * CHUNK)
            f = jnp.dot(w, vln[rs, cs], preferred_element_type=F32) + bs_ref[:, cs]
            ub_ref[0, rs, cs] = (u[rs, cs] * f).astype(BF16)


def _inproj_kernel(x_ref, mod_ref, win_ref, qg_ref, kg_ref, lng_ref, lnb_ref, ws_ref, bs_ref,
                   qt_ref, k_ref, vt_ref, ub_ref, ga_ref, gb_ref, wqkvt_s, wrest_s):
    i = pl.program_id(0)

    @pl.when(i < W_STEPS)
    def _():
        rows = win_ref.shape[0]
        r = pl.multiple_of(i * rows, rows)
        chunk = win_ref[...]
        wqkvt_s[:, pl.ds(r, rows)] = chunk[:, 0:QKV_COLS].T.astype(BF16)
        wrest_s[pl.ds(r, rows), :] = chunk[:, QKV_COLS:QKV_COLS + REST_COLS].astype(BF16)

    @pl.when(i >= W_STEPS)
    def _():
        _inproj_tile(x_ref, mod_ref, wqkvt_s, wrest_s, qg_ref, kg_ref, lng_ref, lnb_ref,
                     ws_ref, bs_ref, qt_ref, k_ref, vt_ref, ub_ref, ga_ref, gb_ref)


def _inproj(x3d, mod, w_in, qg_rows, kg_rows, ln_g, ln_b, w_spatial, bs_full):
    nt = SEQ // TM_PROJ
    tile = lambda i: jnp.maximum(i - W_STEPS, 0)
    tok_spec = lambda width: pl.BlockSpec(
        (1, TM_PROJ, width), lambda i: (tile(i) // nt, tile(i) % nt, 0))
    feat_spec = lambda width: pl.BlockSpec(
        (1, width, TM_PROJ), lambda i: (tile(i) // nt, 0, tile(i) % nt))
    return pl.pallas_call(
        _inproj_kernel,
        grid=(W_STEPS + BATCH * nt,),
        in_specs=[
            pl.BlockSpec((None, TM_PROJ, D_MODEL), lambda i: (tile(i) // nt, tile(i) % nt, 0)),
            pl.BlockSpec((1, N_MOD, D_MODEL), lambda i: (tile(i) // nt, 0, 0)),
            _weight_rows_spec(w_in.shape),
            _const_spec((QK_WIDTH, LANES)),
            _const_spec((QK_WIDTH, LANES)),
            _const_spec((1, GMLP_WIDTH)),
            _const_spec((1, GMLP_WIDTH)),
            _const_spec((N_GROUPS, CHUNK, CHUNK)),
            _const_spec((CHUNK, GMLP_WIDTH)),
        ],
        out_specs=[
            feat_spec(QK_WIDTH), tok_spec(QK_WIDTH), feat_spec(ATTN_WIDTH),
            tok_spec(GMLP_WIDTH), tok_spec(D_MODEL), tok_spec(D_MODEL),
        ],
        out_shape=[
            jax.ShapeDtypeStruct((BATCH, QK_WIDTH, SEQ), BF16),
            jax.ShapeDtypeStruct((BATCH, SEQ, QK_WIDTH), BF16),
            jax.ShapeDtypeStruct((BATCH, ATTN_WIDTH, SEQ), BF16),
            jax.ShapeDtypeStruct((BATCH, SEQ, GMLP_WIDTH), BF16),
            jax.ShapeDtypeStruct((BATCH, SEQ, D_MODEL), BF16),
            jax.ShapeDtypeStruct((BATCH, SEQ, D_MODEL), BF16),
        ],
        scratch_shapes=[pltpu.VMEM((QKV_COLS, D_MODEL), BF16),
                        pltpu.VMEM((D_MODEL, REST_COLS), BF16)],
        compiler_params=pltpu.CompilerParams(
            dimension_semantics=("arbitrary",), vmem_limit_bytes=VMEM_LIMIT),
        name="inproj",
    )(x3d, mod, w_in, qg_rows, kg_rows, ln_g, ln_b, w_spatial, bs_full)


def _rel_buckets(dist):
    n = np.maximum(dist, 0)
    max_exact = N_BUCKETS // 2
    nf = np.maximum(n, 1).astype(np.float32)
    large = max_exact + (np.log(nf / np.float32(max_exact))
                         / np.float32(math.log(MAX_DISTANCE / max_exact))
                         * np.float32(N_BUCKETS - max_exact)).astype(np.int32)
    large = np.minimum(large, N_BUCKETS - 1)
    return np.where(n < max_exact, n, large).astype(np.int32)


def _attn_kernel(table_ref, lam_ref, qt_ref, k_ref, vt_ref, bkt_ref, subg_ref,
                 o_ref, bias_ref):
    h = pl.program_id(0)
    b = pl.program_id(1)

    @pl.when(b == 0)
    def _():
        far = table_ref[N_BUCKETS - 1, h]
        key = lax.broadcasted_iota(jnp.int32, (TQ, TQ), 0)
        qry = lax.broadcasted_iota(jnp.int32, (TQ, TQ), 1)
        for t in range(2):
            bkt = bkt_ref[t]
            tile = jnp.zeros((TQ, TQ), F32)
            for bucket in range(N_BUCKETS - 1):
                tile = jnp.where(bkt == bucket, (table_ref[bucket, h] - far) * LOG2E, tile)
            if t == 0:
                tile = jnp.where(key <= qry, tile, -jnp.inf)
            bias_ref[t] = jnp.concatenate([tile, tile], axis=1)

    lp = lam_ref[...]
    lam = (jnp.exp(jnp.sum(lp[0:1] * lp[1:2], axis=1, keepdims=True))
           - jnp.exp(jnp.sum(lp[2:3] * lp[3:4], axis=1, keepdims=True)) + LAM_INIT)
    subg = subg_ref[...]
    zeros = jnp.zeros((HEAD_DIM, TQ), BF16)

    ones = jnp.ones((SUM_ROWS, TQ), BF16)
    n_strips = 2 * TQ // LANES
    nq = SEQ // TQ
    blocks = [(qi, j) for qi in range(nq) for j in range(qi + 1)]
    qpads = {}

    def scores(qi, j):
        if qi not in qpads:
            qt = qt_ref[0, :, qi * TQ:(qi + 1) * TQ]
            qpads[qi] = jnp.concatenate(
                [jnp.concatenate([qt[0:HEAD_DIM], zeros], axis=0),
                 jnp.concatenate([zeros, qt[HEAD_DIM:2 * HEAD_DIM]], axis=0)], axis=1)
        return jnp.dot(k_ref[0, j * TQ:(j + 1) * TQ, :], qpads[qi],
                       preferred_element_type=F32)

    pending = [scores(*blk) for blk in blocks[:QK_AHEAD]]
    m = acc = None
    for n, (qi, j) in enumerate(blocks):
        s = pending.pop(0)
        if n + QK_AHEAD < len(blocks):
            pending.append(scores(*blocks[n + QK_AHEAD]))
        if j == 0:
            m = [None] * n_strips
            acc = [None] * n_strips
        p, alpha = [], []
        for c in range(n_strips):
            sc = s[:, c * LANES:(c + 1) * LANES]
            if j == qi:
                sc = sc + bias_ref[0, :, c * LANES:(c + 1) * LANES]
            elif j == qi - 1:
                sc = sc + bias_ref[1, :, c * LANES:(c + 1) * LANES]
            blk_max = jnp.max(sc, axis=0, keepdims=True)
            m_new = blk_max if j == 0 else jnp.maximum(m[c], blk_max)
            p.append(jnp.exp2(sc - m_new).astype(BF16))
            alpha.append(None if j == 0 else jnp.exp2(m[c] - m_new))
            m[c] = m_new
        vt_aug = jnp.concatenate([vt_ref[0, :, j * TQ:(j + 1) * TQ], ones], axis=0)
        pv = jnp.dot(vt_aug, jnp.concatenate(p, axis=1),
                     preferred_element_type=F32)
        for c in range(n_strips):
            pvc = pv[0:V_DIM + 8, c * LANES:(c + 1) * LANES]
            acc[c] = pvc if j == 0 else alpha[c] * acc[c] + pvc
        if j == qi:
            a = jnp.concatenate(acc, axis=1)
            inv = 1.0 / a[V_DIM:V_DIM + 1]
            ot = a[0:V_DIM, 0:TQ] * inv[:, 0:TQ] - a[0:V_DIM, TQ:2 * TQ] * (lam * inv[:, TQ:2 * TQ])
            ms = jnp.mean(ot * ot, axis=0, keepdims=True)
            on = (ot * lax.rsqrt(ms + EPS)).T
            o_ref[0, qi * TQ:(qi + 1) * TQ, :] = (on * subg).astype(BF16)


def _attention(table, lam, qt, k, vt, buckets, subg_row):
    return pl.pallas_call(
        _attn_kernel,
        grid=(N_HEADS, BATCH),
        in_specs=[
            pl.BlockSpec(memory_space=pltpu.SMEM),
            _const_spec((4, HEAD_DIM)),
            pl.BlockSpec((1, 2 * HEAD_DIM, SEQ), lambda h, b: (b, h, 0)),
            pl.BlockSpec((1, SEQ, 2 * HEAD_DIM), lambda h, b: (b, 0, h)),
            pl.BlockSpec((1, V_DIM, SEQ), lambda h, b: (b, h, 0)),
            _const_spec((2, TQ, TQ)),
            _const_spec((1, V_DIM)),
        ],
        out_specs=pl.BlockSpec((1, SEQ, V_DIM), lambda h, b: (b, 0, h)),
        out_shape=jax.ShapeDtypeStruct((BATCH, SEQ, ATTN_WIDTH), BF16),
        scratch_shapes=[pltpu.VMEM((2, TQ, 2 * TQ), F32)],
        compiler_params=pltpu.CompilerParams(
            dimension_semantics=("arbitrary", "arbitrary"),
            vmem_limit_bytes=VMEM_LIMIT),
        name="diff_attn",
    )(table, lam, qt, k, vt, buckets, subg_row)


def _mix_kernel(x_ref, mod_ref, o_ref, ub_ref, ga_ref, gb_ref, wa_ref, wb_ref, wo_ref,
                out_ref, wa_s, wb_s, wo_s):
    i = pl.program_id(0)

    @pl.when(i < W_STEPS)
    def _():
        _stash_rows(wa_s, wa_ref, i)
        _stash_rows(wb_s, wb_ref, i)
        _stash_rows(wo_s, wo_ref, i)

    @pl.when(i >= W_STEPS)
    def _():
        ya = jnp.dot(o_ref[...], wa_s[...], preferred_element_type=F32)
        yb = jnp.dot(ub_ref[...], wb_s[...], preferred_element_type=F32)
        merged = (ga_ref[...].astype(F32) * ya + gb_ref[...].astype(F32) * yb).astype(BF16)
        z = jnp.dot(merged, wo_s[...], preferred_element_type=F32)
        out_ref[...] = x_ref[...] + mod_ref[0, 5:6, :] * z


def _mix(x2d, mod, o2d, ub2d, ga2d, gb2d, wa, wb, wo):
    t = x2d.shape[0]
    tiles_per_batch = SEQ // TM_MIX
    tile = lambda i: jnp.maximum(i - W_STEPS, 0)
    tok = lambda: pl.BlockSpec((TM_MIX, D_MODEL), lambda i: (tile(i), 0))
    return pl.pallas_call(
        _mix_kernel,
        grid=(W_STEPS + t // TM_MIX,),
        in_specs=[
            tok(),
            pl.BlockSpec((1, N_MOD, D_MODEL), lambda i: (tile(i) // tiles_per_batch, 0, 0)),
            tok(), tok(), tok(), tok(),
            _weight_rows_spec(wa.shape),
            _weight_rows_spec(wb.shape),
            _weight_rows_spec(wo.shape),
        ],
        out_specs=tok(),
        out_shape=jax.ShapeDtypeStruct((t, D_MODEL), F32),
        scratch_shapes=[pltpu.VMEM(wa.shape, BF16), pltpu.VMEM(wb.shape, BF16),
                        pltpu.VMEM(wo.shape, BF16)],
        compiler_params=pltpu.CompilerParams(
            dimension_semantics=("arbitrary",), vmem_limit_bytes=VMEM_LIMIT),
        name="mix_out",
    )(x2d, mod, o2d, ub2d, ga2d, gb2d, wa, wb, wo)


def kernel(x, c, w_ada, b_ada, w_ffn1_gate, w_ffn1_up, w_ffn1_down, w_in, q_norm_g, k_norm_g, lam_q1, lam_k1, lam_q2, lam_k2, subln_g, rel_bias_table, gmlp_ln_g, gmlp_ln_b, w_spatial, b_spatial, w_a_proj, w_b_proj, w_o, w_ffn2_gate, w_ffn2_up, w_ffn2_down):
    b, s, d = x.shape
    t = b * s

    mod = _adaln_mod(c, w_ada[0], b_ada[0]).reshape(b, N_MOD, d)

    x1 = _ffn(x.reshape(t, d), mod, w_ffn1_gate[0], w_ffn1_up[0], w_ffn1_down[0], sub=0)

    scale = HEAD_DIM ** -0.5 * LOG2E
    qg_rows = jnp.broadcast_to(
        jnp.tile(q_norm_g[0] * scale, QK_WIDTH // HEAD_DIM)[:, None], (QK_WIDTH, LANES))
    kg_rows = jnp.broadcast_to(
        jnp.tile(k_norm_g[0], QK_WIDTH // HEAD_DIM)[:, None], (QK_WIDTH, LANES))
    bs_full = jnp.repeat(b_spatial[0].T, GROUP_DIM, axis=1)
    qt, k, vt, ub, ga, gb = _inproj(
        x1.reshape(b, s, d), mod, w_in[0], qg_rows, kg_rows,
        gmlp_ln_g[0].reshape(1, -1), gmlp_ln_b[0].reshape(1, -1), w_spatial[0], bs_full)

    lam = jnp.stack([lam_q1[0], lam_k1[0], lam_q2[0], lam_k2[0]])
    kk = np.arange(TQ)[:, None]
    qq = np.arange(TQ)[None, :]
    buckets = jnp.asarray(np.stack([_rel_buckets(qq - kk), _rel_buckets(TQ + qq - kk)]))
    subg_row = (subln_g[0] * (1.0 - LAM_INIT)).reshape(1, V_DIM)
    o = _attention(rel_bias_table, lam, qt, k, vt, buckets, subg_row)

    x2 = _mix(x1, mod, o.reshape(t, ATTN_WIDTH), ub.reshape(t, GMLP_WIDTH),
              ga.reshape(t, d), gb.reshape(t, d), w_a_proj[0], w_b_proj[0], w_o[0])

    x3 = _ffn(x2, mod, w_ffn2_gate[0], w_ffn2_up[0], w_ffn2_down[0], sub=2)
    return x3.reshape(b, s, d)
```

```python
import functools
import math

import numpy as np
import jax
import jax.numpy as jnp
from jax import lax
from jax.experimental import pallas as pl
from jax.experimental.pallas import tpu as pltpu

F32 = jnp.float32
BF16 = jnp.bfloat16

D_MODEL = 1024
BATCH = 8
SEQ = 2048
N_HEADS = 8
HEAD_DIM = 64
V_DIM = 2 * HEAD_DIM
QK_WIDTH = N_HEADS * 2 * HEAD_DIM
ATTN_WIDTH = N_HEADS * V_DIM
N_GROUPS = 8
CHUNK = 128
GMLP_WIDTH = 1024
GROUP_DIM = GMLP_WIDTH // N_GROUPS
N_BUCKETS = 32
MAX_DISTANCE = 128
D_FF = 2816
N_MOD = 9
EPS = 1e-6
LAM_INIT = 0.8 - 0.6 * math.exp(-0.3 * 0)
LOG2E = math.log2(math.e)
QKV_COLS = 2 * QK_WIDTH + ATTN_WIDTH
REST_COLS = 2 * GMLP_WIDTH + 2 * D_MODEL

LANES = 128
VMEM_LIMIT = 56 * 1024 * 1024

TM_FFN = 512
TM_PROJ = 256
TM_MIX = 512
TQ = 256
MOD_TN = 1536
SUM_ROWS = 16
W_STEPS = 8
MAX_SAFE_LOG2 = 64.0
BF16_SLACK = 1.02
QK_AHEAD_TRACKED = 2
QK_AHEAD_BOUNDED = 3
S_SLOTS = QK_AHEAD_BOUNDED + 1


def _const_spec(shape):
    nd = len(shape)
    return pl.BlockSpec(shape, lambda *_: (0,) * nd, pipeline_mode=pl.Buffered(1))


def _weight_rows_spec(shape):
    rows, cols = shape
    return pl.BlockSpec((rows // W_STEPS, cols),
                        lambda i: (jnp.minimum(i, W_STEPS - 1), 0))


def _stash_rows(dst_ref, chunk_ref, i):
    rows = chunk_ref.shape[0]
    dst_ref[pl.ds(pl.multiple_of(i * rows, rows), rows), :] = chunk_ref[...].astype(BF16)


def _silu(x):
    return x * jax.nn.sigmoid(x)


def _gelu(x):
    return 0.5 * x * (1.0 + lax.erf(x * (1.0 / math.sqrt(2.0))))


def _modulated_rmsnorm(x, shift, scale):
    ms = jnp.mean(x * x, axis=-1, keepdims=True)
    return (x * lax.rsqrt(ms + EPS)) * (1.0 + scale) + shift


def _mod_kernel(c_ref, w_ref, b_ref, o_ref):
    a = _silu(c_ref[...]).astype(BF16)
    o_ref[...] = jnp.dot(a, w_ref[...].astype(BF16),
                         preferred_element_type=F32) + b_ref[...]


def _adaln_mod(c, w_ada, b_ada):
    n = w_ada.shape[1]
    return pl.pallas_call(
        _mod_kernel,
        grid=(n // MOD_TN,),
        in_specs=[
            pl.BlockSpec((BATCH, D_MODEL), lambda j: (0, 0)),
            pl.BlockSpec((D_MODEL, MOD_TN), lambda j: (0, j)),
            pl.BlockSpec((1, MOD_TN), lambda j: (0, j)),
        ],
        out_specs=pl.BlockSpec((BATCH, MOD_TN), lambda j: (0, j)),
        out_shape=jax.ShapeDtypeStruct((BATCH, n), F32),
        compiler_params=pltpu.CompilerParams(
            dimension_semantics=("arbitrary",), vmem_limit_bytes=VMEM_LIMIT),
        name="adaln_mod",
    )(c, w_ada, b_ada.reshape(1, n))


def _ffn_kernel(x_ref, mod_ref, wg_ref, wu_ref, wd_ref, o_ref, wg_s, wu_s, wd_s, *, sub):
    i = pl.program_id(0)

    @pl.when(i < W_STEPS)
    def _():
        _stash_rows(wg_s, wg_ref, i)
        _stash_rows(wu_s, wu_ref, i)
        _stash_rows(wd_s, wd_ref, i)

    @pl.when(i >= W_STEPS)
    def _():
        x = x_ref[...]
        shift = mod_ref[0, 3 * sub + 0:3 * sub + 1, :]
        scale = mod_ref[0, 3 * sub + 1:3 * sub + 2, :]
        gate = mod_ref[0, 3 * sub + 2:3 * sub + 3, :]
        h = _modulated_rmsnorm(x, shift, scale).astype(BF16)
        g = jnp.dot(h, wg_s[...], preferred_element_type=F32)
        u = jnp.dot(h, wu_s[...], preferred_element_type=F32)
        a = (_silu(g) * u).astype(BF16)
        y = jnp.dot(a, wd_s[...], preferred_element_type=F32)
        o_ref[...] = x + (0.5 * gate) * y


def _ffn(x2d, mod, wg, wu, wd, *, sub):
    t = x2d.shape[0]
    tiles_per_batch = SEQ // TM_FFN
    tile = lambda i: jnp.maximum(i - W_STEPS, 0)
    return pl.pallas_call(
        functools.partial(_ffn_kernel, sub=sub),
        grid=(W_STEPS + t // TM_FFN,),
        in_specs=[
            pl.BlockSpec((TM_FFN, D_MODEL), lambda i: (tile(i), 0)),
            pl.BlockSpec((1, N_MOD, D_MODEL), lambda i: (tile(i) // tiles_per_batch, 0, 0)),
            _weight_rows_spec(wg.shape),
            _weight_rows_spec(wu.shape),
            _weight_rows_spec(wd.shape),
        ],
        out_specs=pl.BlockSpec((TM_FFN, D_MODEL), lambda i: (tile(i), 0)),
        out_shape=jax.ShapeDtypeStruct((t, D_MODEL), F32),
        scratch_shapes=[pltpu.VMEM(wg.shape, BF16), pltpu.VMEM(wu.shape, BF16),
                        pltpu.VMEM(wd.shape, BF16)],
        compiler_params=pltpu.CompilerParams(
            dimension_semantics=("arbitrary",), vmem_limit_bytes=VMEM_LIMIT),
        name=f"ffn{sub}",
    )(x2d, mod, wg, wu, wd)


def _chunk_rmsnorm_rows(xt, gain_rows):
    width, tm = xt.shape
    x3 = xt.reshape(width // HEAD_DIM, HEAD_DIM, tm)
    ms = jnp.mean(x3 * x3, axis=1, keepdims=True)
    xn = (x3 * lax.rsqrt(ms + EPS)).reshape(width, tm)
    return jnp.concatenate(
        [xn[:, j * LANES:(j + 1) * LANES] * gain_rows for j in range(tm // LANES)], axis=1)


def _inproj_tile(x_ref, mod_ref, wqkvt_ref, wrest_ref, qg_ref, kg_ref,
                 lng_ref, lnb_ref, ws_ref, bs_ref,
                 qt_ref, k_ref, vt_ref, ub_ref, ga_ref, gb_ref):
    x = x_ref[...]
    shift = mod_ref[0, 3:4, :]
    scale = mod_ref[0, 4:5, :]
    h = _modulated_rmsnorm(x, shift, scale).astype(BF16)

    pt = lax.dot_general(wqkvt_ref[...], h, (((1,), (1,)), ((), ())),
                         preferred_element_type=F32)
    qt = _chunk_rmsnorm_rows(pt[0:QK_WIDTH], qg_ref[...])
    qt_ref[0] = qt.astype(BF16)
    kt = _chunk_rmsnorm_rows(pt[QK_WIDTH:2 * QK_WIDTH], kg_ref[...])
    k_ref[0] = kt.T.astype(BF16)
    vt_ref[0] = pt[2 * QK_WIDTH:QKV_COLS].astype(BF16)

    rest = jnp.dot(h, wrest_ref[...], preferred_element_type=F32)
    u = _gelu(rest[:, 0:GMLP_WIDTH])
    gv = _gelu(rest[:, GMLP_WIDTH:2 * GMLP_WIDTH])
    ga_ref[0] = jax.nn.sigmoid(rest[:, 2 * GMLP_WIDTH:2 * GMLP_WIDTH + D_MODEL]).astype(BF16)
    gb_ref[0] = jax.nn.sigmoid(rest[:, 2 * GMLP_WIDTH + D_MODEL:REST_COLS]).astype(BF16)

    mu = jnp.mean(gv, axis=-1, keepdims=True)
    cen = gv - mu
    var = jnp.mean(cen * cen, axis=-1, keepdims=True)
    vln = ((cen * lax.rsqrt(var + EPS)) * lng_ref[...] + lnb_ref[...]).astype(BF16)

    tm = x.shape[0]
    row = lax.broadcasted_iota(jnp.int32, (CHUNK, CHUNK), 0)
    col = lax.broadcasted_iota(jnp.int32, (CHUNK, CHUNK), 1)
    causal = row >= col
    for g in range(N_GROUPS):
        w = jnp.where(causal, ws_ref[g], 0.0).astype(BF16)
        cs = slice(g * GROUP_DIM, (g + 1) * GROUP_DIM)
        for n in range(tm // CHUNK):
            rs = slice(n * CHUNK, (n + 1) * CHUNK)
            f = jnp.dot(w, vln[rs, cs], preferred_element_type=F32) + bs_ref[:, cs]
            ub_ref[0, rs, cs] = (u[rs, cs] * f).astype(BF16)


def _inproj_kernel(x_ref, mod_ref, win_ref, qg_ref, kg_ref, lng_ref, lnb_ref, ws_ref, bs_ref,
                   qt_ref, k_ref, vt_ref, ub_ref, ga_ref, gb_ref, wqkvt_s, wrest_s):
    i = pl.program_id(0)

    @pl.when(i < W_STEPS)
    def _():
        rows = win_ref.shape[0]
        r = pl.multiple_of(i * rows, rows)
        chunk = win_ref[...]
        wqkvt_s[:, pl.ds(r, rows)] = chunk[:, 0:QKV_COLS].T.astype(BF16)
        wrest_s[pl.ds(r, rows), :] = chunk[:, QKV_COLS:QKV_COLS + REST_COLS].astype(BF16)

    @pl.when(i >= W_STEPS)
    def _():
        _inproj_tile(x_ref, mod_ref, wqkvt_s, wrest_s, qg_ref, kg_ref, lng_ref, lnb_ref,
                     ws_ref, bs_ref, qt_ref, k_ref, vt_ref, ub_ref, ga_ref, gb_ref)


def _inproj(x3d, mod, w_in, qg_rows, kg_rows, ln_g, ln_b, w_spatial, bs_full):
    nt = SEQ // TM_PROJ
    tile = lambda i: jnp.maximum(i - W_STEPS, 0)
    tok_spec = lambda width: pl.BlockSpec(
        (1, TM_PROJ, width), lambda i: (tile(i) // nt, tile(i) % nt, 0))
    feat_spec = lambda width: pl.BlockSpec(
        (1, width, TM_PROJ), lambda i: (tile(i) // nt, 0, tile(i) % nt))
    return pl.pallas_call(
        _inproj_kernel,
        grid=(W_STEPS + BATCH * nt,),
        in_specs=[
            pl.BlockSpec((None, TM_PROJ, D_MODEL), lambda i: (tile(i) // nt, tile(i) % nt, 0)),
            pl.BlockSpec((1, N_MOD, D_MODEL), lambda i: (tile(i) // nt, 0, 0)),
            _weight_rows_spec(w_in.shape),
            _const_spec((QK_WIDTH, LANES)),
            _const_spec((QK_WIDTH, LANES)),
            _const_spec((1, GMLP_WIDTH)),
            _const_spec((1, GMLP_WIDTH)),
            _const_spec((N_GROUPS, CHUNK, CHUNK)),
            _const_spec((CHUNK, GMLP_WIDTH)),
        ],
        out_specs=[
            feat_spec(QK_WIDTH), tok_spec(QK_WIDTH), feat_spec(ATTN_WIDTH),
            tok_spec(GMLP_WIDTH), tok_spec(D_MODEL), tok_spec(D_MODEL),
        ],
        out_shape=[
            jax.ShapeDtypeStruct((BATCH, QK_WIDTH, SEQ), BF16),
            jax.ShapeDtypeStruct((BATCH, SEQ, QK_WIDTH), BF16),
            jax.ShapeDtypeStruct((BATCH, ATTN_WIDTH, SEQ), BF16),
            jax.ShapeDtypeStruct((BATCH, SEQ, GMLP_WIDTH), BF16),
            jax.ShapeDtypeStruct((BATCH, SEQ, D_MODEL), BF16),
            jax.ShapeDtypeStruct((BATCH, SEQ, D_MODEL), BF16),
        ],
        scratch_shapes=[pltpu.VMEM((QKV_COLS, D_MODEL), BF16),
                        pltpu.VMEM((D_MODEL, REST_COLS), BF16)],
        compiler_params=pltpu.CompilerParams(
            dimension_semantics=("arbitrary",), vmem_limit_bytes=VMEM_LIMIT),
        name="inproj",
    )(x3d, mod, w_in, qg_rows, kg_rows, ln_g, ln_b, w_spatial, bs_full)


def _rel_buckets(dist):
    n = np.maximum(dist, 0)
    max_exact = N_BUCKETS // 2
    nf = np.maximum(n, 1).astype(np.float32)
    large = max_exact + (np.log(nf / np.float32(max_exact))
                         / np.float32(math.log(MAX_DISTANCE / max_exact))
                         * np.float32(N_BUCKETS - max_exact)).astype(np.int32)
    large = np.minimum(large, N_BUCKETS - 1)
    return np.where(n < max_exact, n, large).astype(np.int32)


def _attn_kernel(table_ref, bound_ref, lam_ref, qt_ref, k_ref, vt_ref, bkt_ref, subg_ref,
                 o_ref, bias_ref, s_scr):
    h = pl.program_id(0)
    b = pl.program_id(1)

    @pl.when(b == 0)
    def _():
        far = table_ref[N_BUCKETS - 1, h]
        key = lax.broadcasted_iota(jnp.int32, (TQ, TQ), 0)
        qry = lax.broadcasted_iota(jnp.int32, (TQ, TQ), 1)
        for t in range(2):
            bkt = bkt_ref[t]
            tile = jnp.zeros((TQ, TQ), F32)
            for bucket in range(N_BUCKETS - 1):
                tile = jnp.where(bkt == bucket, (table_ref[bucket, h] - far) * LOG2E, tile)
            if t == 0:
                tile = jnp.where(key <= qry, tile, -jnp.inf)
            bias_ref[t] = jnp.concatenate([tile, tile], axis=1)

    @pl.when(bound_ref[0] <= MAX_SAFE_LOG2)
    def _():
        _attn_program(lam_ref, qt_ref, k_ref, vt_ref, subg_ref, o_ref, bias_ref, s_scr,
                      track_max=False)

    @pl.when(jnp.logical_not(bound_ref[0] <= MAX_SAFE_LOG2))
    def _():
        _attn_program(lam_ref, qt_ref, k_ref, vt_ref, subg_ref, o_ref, bias_ref, s_scr,
                      track_max=True)


def _attn_program(lam_ref, qt_ref, k_ref, vt_ref, subg_ref, o_ref, bias_ref, s_scr, *, track_max):
    lp = lam_ref[...]
    lam = (jnp.exp(jnp.sum(lp[0:1] * lp[1:2], axis=1, keepdims=True))
           - jnp.exp(jnp.sum(lp[2:3] * lp[3:4], axis=1, keepdims=True)) + LAM_INIT)
    subg = subg_ref[...]
    zeros = jnp.zeros((HEAD_DIM, TQ), BF16)
    ones = jnp.ones((SUM_ROWS, TQ), BF16)
    n_strips = 2 * TQ // LANES
    nq = SEQ // TQ
    blocks = [(qi, j) for qi in range(nq) for j in range(qi + 1)]
    qpads = {}

    def scores(qi, j):
        if qi not in qpads:
            qt = qt_ref[0, :, qi * TQ:(qi + 1) * TQ]
            qpads[qi] = jnp.concatenate(
                [jnp.concatenate([qt[0:HEAD_DIM], zeros], axis=0),
                 jnp.concatenate([zeros, qt[HEAD_DIM:2 * HEAD_DIM]], axis=0)], axis=1)
        return jnp.dot(k_ref[0, j * TQ:(j + 1) * TQ, :], qpads[qi],
                       preferred_element_type=F32)

    def probs_tracked(s, qi, j, m):
        p, alpha, m_out = [], [], []
        for c in range(n_strips):
            sc = s[:, c * LANES:(c + 1) * LANES]
            if j == qi:
                sc = sc + bias_ref[0, :, c * LANES:(c + 1) * LANES]
            elif j == qi - 1:
                sc = sc + bias_ref[1, :, c * LANES:(c + 1) * LANES]
            blk_max = jnp.max(sc, axis=0, keepdims=True)
            m_new = blk_max if j == 0 else jnp.maximum(m[c], blk_max)
            p.append(jnp.exp2(sc - m_new).astype(BF16))
            alpha.append(None if j == 0 else jnp.exp2(m[c] - m_new))
            m_out.append(m_new)
        return jnp.concatenate(p, axis=1), alpha, m_out

    def probs_bounded(s, qi, j):
        if j == qi:
            s = s + bias_ref[0]
        elif j == qi - 1:
            s = s + bias_ref[1]
        return jnp.exp2(s).astype(BF16)

    ahead = QK_AHEAD_TRACKED if track_max else QK_AHEAD_BOUNDED

    def issue(n):
        if track_max:
            return scores(*blocks[n])
        s_scr[n % S_SLOTS] = scores(*blocks[n])
        return None

    pending = [issue(n) for n in range(ahead)]
    m = acc = None
    for n, (qi, j) in enumerate(blocks):
        s = pending.pop(0)
        if n + ahead < len(blocks):
            pending.append(issue(n + ahead))
        if not track_max:
            s = s_scr[n % S_SLOTS]
        if track_max:
            p, alpha, m = probs_tracked(s, qi, j, m)
        else:
            p = probs_bounded(s, qi, j)
        vt_aug = jnp.concatenate([vt_ref[0, :, j * TQ:(j + 1) * TQ], ones], axis=0)
        pv = jnp.dot(vt_aug, p, preferred_element_type=F32)[0:V_DIM + 8]
        if j == 0:
            acc = pv
        elif track_max:
            acc = jnp.concatenate(
                [alpha[c] * acc[:, c * LANES:(c + 1) * LANES] + pv[:, c * LANES:(c + 1) * LANES]
                 for c in range(n_strips)], axis=1)
        else:
            acc = acc + pv
        if j == qi:
            inv = 1.0 / acc[V_DIM:V_DIM + 1]
            ot = (acc[0:V_DIM, 0:TQ] * inv[:, 0:TQ]
                  - acc[0:V_DIM, TQ:2 * TQ] * (lam * inv[:, TQ:2 * TQ]))
            ms = jnp.mean(ot * ot, axis=0, keepdims=True)
            on = (ot * lax.rsqrt(ms + EPS)).T
            o_ref[0, qi * TQ:(qi + 1) * TQ, :] = (on * subg).astype(BF16)


def _attention(table, bound, lam, qt, k, vt, buckets, subg_row):
    return pl.pallas_call(
        _attn_kernel,
        grid=(N_HEADS, BATCH),
        in_specs=[
            pl.BlockSpec(memory_space=pltpu.SMEM),
            pl.BlockSpec(memory_space=pltpu.SMEM),
            _const_spec((4, HEAD_DIM)),
            pl.BlockSpec((1, 2 * HEAD_DIM, SEQ), lambda h, b: (b, h, 0)),
            pl.BlockSpec((1, SEQ, 2 * HEAD_DIM), lambda h, b: (b, 0, h)),
            pl.BlockSpec((1, V_DIM, SEQ), lambda h, b: (b, h, 0)),
            _const_spec((2, TQ, TQ)),
            _const_spec((1, V_DIM)),
        ],
        out_specs=pl.BlockSpec((1, SEQ, V_DIM), lambda h, b: (b, 0, h)),
        out_shape=jax.ShapeDtypeStruct((BATCH, SEQ, ATTN_WIDTH), BF16),
        scratch_shapes=[pltpu.VMEM((2, TQ, 2 * TQ), F32),
                        pltpu.VMEM((S_SLOTS, TQ, 2 * TQ), F32)],
        compiler_params=pltpu.CompilerParams(
            dimension_semantics=("arbitrary", "arbitrary"),
            vmem_limit_bytes=VMEM_LIMIT),
        name="diff_attn",
    )(table, bound, lam, qt, k, vt, buckets, subg_row)


def _mix_kernel(x_ref, mod_ref, o_ref, ub_ref, ga_ref, gb_ref, wa_ref, wb_ref, wo_ref,
                out_ref, wa_s, wb_s, wo_s):
    i = pl.program_id(0)

    @pl.when(i < W_STEPS)
    def _():
        _stash_rows(wa_s, wa_ref, i)
        _stash_rows(wb_s, wb_ref, i)
        _stash_rows(wo_s, wo_ref, i)

    @pl.when(i >= W_STEPS)
    def _():
        ya = jnp.dot(o_ref[...], wa_s[...], preferred_element_type=F32)
        yb = jnp.dot(ub_ref[...], wb_s[...], preferred_element_type=F32)
        merged = (ga_ref[...].astype(F32) * ya + gb_ref[...].astype(F32) * yb).astype(BF16)
        z = jnp.dot(merged, wo_s[...], preferred_element_type=F32)
        out_ref[...] = x_ref[...] + mod_ref[0, 5:6, :] * z


def _mix(x2d, mod, o2d, ub2d, ga2d, gb2d, wa, wb, wo):
    t = x2d.shape[0]
    tiles_per_batch = SEQ // TM_MIX
    tile = lambda i: jnp.maximum(i - W_STEPS, 0)
    tok = lambda: pl.BlockSpec((TM_MIX, D_MODEL), lambda i: (tile(i), 0))
    return pl.pallas_call(
        _mix_kernel,
        grid=(W_STEPS + t // TM_MIX,),
        in_specs=[
            tok(),
            pl.BlockSpec((1, N_MOD, D_MODEL), lambda i: (tile(i) // tiles_per_batch, 0, 0)),
            tok(), tok(), tok(), tok(),
            _weight_rows_spec(wa.shape),
            _weight_rows_spec(wb.shape),
            _weight_rows_spec(wo.shape),
        ],
        out_specs=tok(),
        out_shape=jax.ShapeDtypeStruct((t, D_MODEL), F32),
        scratch_shapes=[pltpu.VMEM(wa.shape, BF16), pltpu.VMEM(wb.shape, BF16),
                        pltpu.VMEM(wo.shape, BF16)],
        compiler_params=pltpu.CompilerParams(
            dimension_semantics=("arbitrary",), vmem_limit_bytes=VMEM_LIMIT),
        name="mix_out",
    )(x2d, mod, o2d, ub2d, ga2d, gb2d, wa, wb, wo)


def kernel(x, c, w_ada, b_ada, w_ffn1_gate, w_ffn1_up, w_ffn1_down, w_in, q_norm_g, k_norm_g, lam_q1, lam_k1, lam_q2, lam_k2, subln_g, rel_bias_table, gmlp_ln_g, gmlp_ln_b, w_spatial, b_spatial, w_a_proj, w_b_proj, w_o, w_ffn2_gate, w_ffn2_up, w_ffn2_down):
    b, s, d = x.shape
    t = b * s

    mod = _adaln_mod(c, w_ada[0], b_ada[0]).reshape(b, N_MOD, d)

    x1 = _ffn(x.reshape(t, d), mod, w_ffn1_gate[0], w_ffn1_up[0], w_ffn1_down[0], sub=0)

    scale = HEAD_DIM ** -0.5 * LOG2E
    qg_rows = jnp.broadcast_to(
        jnp.tile(q_norm_g[0] * scale, QK_WIDTH // HEAD_DIM)[:, None], (QK_WIDTH, LANES))
    kg_rows = jnp.broadcast_to(
        jnp.tile(k_norm_g[0], QK_WIDTH // HEAD_DIM)[:, None], (QK_WIDTH, LANES))
    bs_full = jnp.repeat(b_spatial[0].T, GROUP_DIM, axis=1)
    qt, k, vt, ub, ga, gb = _inproj(
        x1.reshape(b, s, d), mod, w_in[0], qg_rows, kg_rows,
        gmlp_ln_g[0].reshape(1, -1), gmlp_ln_b[0].reshape(1, -1), w_spatial[0], bs_full)

    lam = jnp.stack([lam_q1[0], lam_k1[0], lam_q2[0], lam_k2[0]])
    kk = np.arange(TQ)[:, None]
    qq = np.arange(TQ)[None, :]
    buckets = jnp.asarray(np.stack([_rel_buckets(qq - kk), _rel_buckets(TQ + qq - kk)]))
    subg_row = (subln_g[0] * (1.0 - LAM_INIT)).reshape(1, V_DIM)
    far_bias = rel_bias_table[N_BUCKETS - 1:N_BUCKETS]
    score_bound = (HEAD_DIM * jnp.max(jnp.abs(q_norm_g[0] * scale)) * jnp.max(jnp.abs(k_norm_g[0]))
                   * BF16_SLACK + LOG2E * jnp.max(jnp.abs(rel_bias_table - far_bias))).reshape(1)
    o = _attention(rel_bias_table, score_bound, lam, qt, k, vt, buckets, subg_row)

    x2 = _mix(x1, mod, o.reshape(t, ATTN_WIDTH), ub.reshape(t, GMLP_WIDTH),
              ga.reshape(t, d), gb.reshape(t, d), w_a_proj[0], w_b_proj[0], w_o[0])

    x3 = _ffn(x2, mod, w_ffn2_gate[0], w_ffn2_up[0], w_ffn2_down[0], sub=2)
    return x3.reshape(b, s, d)
```

```python
import functools
import math

import numpy as np
import jax
import jax.numpy as jnp
from jax import lax
from jax.experimental import pallas as pl
from jax.experimental.pallas import tpu as pltpu

F32 = jnp.float32
BF16 = jnp.bfloat16

D_MODEL = 1024
BATCH = 8
SEQ = 2048
N_HEADS = 8
HEAD_DIM = 64
V_DIM = 2 * HEAD_DIM
QK_WIDTH = N_HEADS * 2 * HEAD_DIM
ATTN_WIDTH = N_HEADS * V_DIM
N_GROUPS = 8
CHUNK = 128
GMLP_WIDTH = 1024
GROUP_DIM = GMLP_WIDTH // N_GROUPS
N_BUCKETS = 32
MAX_DISTANCE = 128
D_FF = 2816
N_MOD = 9
EPS = 1e-6
LAM_INIT = 0.8 - 0.6 * math.exp(-0.3 * 0)
LOG2E = math.log2(math.e)
QKV_COLS = 2 * QK_WIDTH + ATTN_WIDTH
REST_COLS = 2 * GMLP_WIDTH + 2 * D_MODEL

LANES = 128
VMEM_LIMIT = 56 * 1024 * 1024

TM_FFN = 512
TM_PROJ = 256
TM_MIX = 512
TQ = 256
MOD_TN = 1536
SUM_ROWS = 16
W_STEPS = 8
MAX_SAFE_LOG2 = 64.0
BF16_SLACK = 1.02
QK_AHEAD_TRACKED = 2
QK_AHEAD_BOUNDED = 3
S_SLOTS = QK_AHEAD_BOUNDED + 1


def _const_spec(shape):
    nd = len(shape)
    return pl.BlockSpec(shape, lambda *_: (0,) * nd, pipeline_mode=pl.Buffered(1))


def _weight_rows_spec(shape):
    rows, cols = shape
    return pl.BlockSpec((rows // W_STEPS, cols),
                        lambda i: (jnp.minimum(i, W_STEPS - 1), 0))


def _stash_rows(dst_ref, chunk_ref, i):
    rows = chunk_ref.shape[0]
    dst_ref[pl.ds(pl.multiple_of(i * rows, rows), rows), :] = chunk_ref[...].astype(BF16)


def _silu(x):
    return x * jax.nn.sigmoid(x)


def _gelu(x):
    return 0.5 * x * (1.0 + lax.erf(x * (1.0 / math.sqrt(2.0))))


def _modulated_rmsnorm(x, shift, scale):
    ms = jnp.mean(x * x, axis=-1, keepdims=True)
    return (x * lax.rsqrt(ms + EPS)) * (1.0 + scale) + shift


def _mod_kernel(c_ref, w_ref, b_ref, o_ref):
    a = _silu(c_ref[...]).astype(BF16)
    o_ref[...] = jnp.dot(a, w_ref[...].astype(BF16),
                         preferred_element_type=F32) + b_ref[...]


def _adaln_mod(c, w_ada, b_ada):
    n = w_ada.shape[1]
    return pl.pallas_call(
        _mod_kernel,
        grid=(n // MOD_TN,),
        in_specs=[
            pl.BlockSpec((BATCH, D_MODEL), lambda j: (0, 0)),
            pl.BlockSpec((D_MODEL, MOD_TN), lambda j: (0, j)),
            pl.BlockSpec((1, MOD_TN), lambda j: (0, j)),
        ],
        out_specs=pl.BlockSpec((BATCH, MOD_TN), lambda j: (0, j)),
        out_shape=jax.ShapeDtypeStruct((BATCH, n), F32),
        compiler_params=pltpu.CompilerParams(
            dimension_semantics=("arbitrary",), vmem_limit_bytes=VMEM_LIMIT),
        name="adaln_mod",
    )(c, w_ada, b_ada.reshape(1, n))


def _ffn_kernel(x_ref, mod_ref, wg_ref, wu_ref, wd_ref, o_ref, wg_s, wu_s, wd_s, *, sub):
    i = pl.program_id(0)

    @pl.when(i < W_STEPS)
    def _():
        _stash_rows(wg_s, wg_ref, i)
        _stash_rows(wu_s, wu_ref, i)
        _stash_rows(wd_s, wd_ref, i)

    @pl.when(i >= W_STEPS)
    def _():
        x = x_ref[...]
        shift = mod_ref[0, 3 * sub + 0:3 * sub + 1, :]
        scale = mod_ref[0, 3 * sub + 1:3 * sub + 2, :]
        gate = mod_ref[0, 3 * sub + 2:3 * sub + 3, :]
        h = _modulated_rmsnorm(x, shift, scale).astype(BF16)
        g = jnp.dot(h, wg_s[...], preferred_element_type=F32)
        u = jnp.dot(h, wu_s[...], preferred_element_type=F32)
        a = (_silu(g) * u).astype(BF16)
        y = jnp.dot(a, wd_s[...], preferred_element_type=F32)
        o_ref[...] = x + (0.5 * gate) * y


def _ffn(x2d, mod, wg, wu, wd, *, sub):
    t = x2d.shape[0]
    tiles_per_batch = SEQ // TM_FFN
    tile = lambda i: jnp.maximum(i - W_STEPS, 0)
    return pl.pallas_call(
        functools.partial(_ffn_kernel, sub=sub),
        grid=(W_STEPS + t // TM_FFN,),
        in_specs=[
            pl.BlockSpec((TM_FFN, D_MODEL), lambda i: (tile(i), 0)),
            pl.BlockSpec((1, N_MOD, D_MODEL), lambda i: (tile(i) // tiles_per_batch, 0, 0)),
            _weight_rows_spec(wg.shape),
            _weight_rows_spec(wu.shape),
            _weight_rows_spec(wd.shape),
        ],
        out_specs=pl.BlockSpec((TM_FFN, D_MODEL), lambda i: (tile(i), 0)),
        out_shape=jax.ShapeDtypeStruct((t, D_MODEL), F32),
        scratch_shapes=[pltpu.VMEM(wg.shape, BF16), pltpu.VMEM(wu.shape, BF16),
                        pltpu.VMEM(wd.shape, BF16)],
        compiler_params=pltpu.CompilerParams(
            dimension_semantics=("arbitrary",), vmem_limit_bytes=VMEM_LIMIT),
        name=f"ffn{sub}",
    )(x2d, mod, wg, wu, wd)


def _chunk_rmsnorm_rows(xt, gain_rows):
    width, tm = xt.shape
    x3 = xt.reshape(width // HEAD_DIM, HEAD_DIM, tm)
    ms = jnp.mean(x3 * x3, axis=1, keepdims=True)
    xn = (x3 * lax.rsqrt(ms + EPS)).reshape(width, tm)
    return jnp.concatenate(
        [xn[:, j * LANES:(j + 1) * LANES] * gain_rows for j in range(tm // LANES)], axis=1)


def _inproj_tile(x_ref, mod_ref, wqkvt_ref, wrest_ref, qg_ref, kg_ref,
                 lng_ref, lnb_ref, ws_ref, bs_ref,
                 qt_ref, k_ref, vt_ref, ub_ref, ga_ref, gb_ref):
    x = x_ref[...]
    shift = mod_ref[0, 3:4, :]
    scale = mod_ref[0, 4:5, :]
    h = _modulated_rmsnorm(x, shift, scale).astype(BF16)

    pt = lax.dot_general(wqkvt_ref[...], h, (((1,), (1,)), ((), ())),
                         preferred_element_type=F32)
    qt = _chunk_rmsnorm_rows(pt[0:QK_WIDTH], qg_ref[...])
    qt_ref[0, 0] = qt.astype(BF16)
    kt = _chunk_rmsnorm_rows(pt[QK_WIDTH:2 * QK_WIDTH], kg_ref[...])
    k_ref[0] = kt.T.astype(BF16)
    vt_ref[0, 0] = pt[2 * QK_WIDTH:QKV_COLS].astype(BF16)

    rest = jnp.dot(h, wrest_ref[...], preferred_element_type=F32)
    u = _gelu(rest[:, 0:GMLP_WIDTH])
    gv = _gelu(rest[:, GMLP_WIDTH:2 * GMLP_WIDTH])
    ga_ref[0] = jax.nn.sigmoid(rest[:, 2 * GMLP_WIDTH:2 * GMLP_WIDTH + D_MODEL]).astype(BF16)
    gb_ref[0] = jax.nn.sigmoid(rest[:, 2 * GMLP_WIDTH + D_MODEL:REST_COLS]).astype(BF16)

    mu = jnp.mean(gv, axis=-1, keepdims=True)
    cen = gv - mu
    var = jnp.mean(cen * cen, axis=-1, keepdims=True)
    vln = ((cen * lax.rsqrt(var + EPS)) * lng_ref[...] + lnb_ref[...]).astype(BF16)

    tm = x.shape[0]
    row = lax.broadcasted_iota(jnp.int32, (CHUNK, CHUNK), 0)
    col = lax.broadcasted_iota(jnp.int32, (CHUNK, CHUNK), 1)
    causal = row >= col
    for g in range(N_GROUPS):
        w = jnp.where(causal, ws_ref[g], 0.0).astype(BF16)
        cs = slice(g * GROUP_DIM, (g + 1) * GROUP_DIM)
        for n in range(tm // CHUNK):
            rs = slice(n * CHUNK, (n + 1) * CHUNK)
            f = jnp.dot(w, vln[rs, cs], preferred_element_type=F32) + bs_ref[:, cs]
            ub_ref[0, rs, cs] = (u[rs, cs] * f).astype(BF16)


def _inproj_kernel(x_ref, mod_ref, win_ref, qg_ref, kg_ref, lng_ref, lnb_ref, ws_ref, bs_ref,
                   qt_ref, k_ref, vt_ref, ub_ref, ga_ref, gb_ref, wqkvt_s, wrest_s):
    i = pl.program_id(0)

    @pl.when(i < W_STEPS)
    def _():
        rows = win_ref.shape[0]
        r = pl.multiple_of(i * rows, rows)
        chunk = win_ref[...]
        wqkvt_s[:, pl.ds(r, rows)] = chunk[:, 0:QKV_COLS].T.astype(BF16)
        wrest_s[pl.ds(r, rows), :] = chunk[:, QKV_COLS:QKV_COLS + REST_COLS].astype(BF16)

    @pl.when(i >= W_STEPS)
    def _():
        _inproj_tile(x_ref, mod_ref, wqkvt_s, wrest_s, qg_ref, kg_ref, lng_ref, lnb_ref,
                     ws_ref, bs_ref, qt_ref, k_ref, vt_ref, ub_ref, ga_ref, gb_ref)


def _inproj(x3d, mod, w_in, qg_rows, kg_rows, ln_g, ln_b, w_spatial, bs_full):
    nt = SEQ // TM_PROJ
    tile = lambda i: jnp.maximum(i - W_STEPS, 0)
    tok_spec = lambda width: pl.BlockSpec(
        (1, TM_PROJ, width), lambda i: (tile(i) // nt, tile(i) % nt, 0))
    feat_spec = lambda width: pl.BlockSpec(
        (1, 1, width, TM_PROJ), lambda i: (tile(i) // nt, tile(i) % nt, 0, 0))
    return pl.pallas_call(
        _inproj_kernel,
        grid=(W_STEPS + BATCH * nt,),
        in_specs=[
            pl.BlockSpec((None, TM_PROJ, D_MODEL), lambda i: (tile(i) // nt, tile(i) % nt, 0)),
            pl.BlockSpec((1, N_MOD, D_MODEL), lambda i: (tile(i) // nt, 0, 0)),
            _weight_rows_spec(w_in.shape),
            _const_spec((QK_WIDTH, LANES)),
            _const_spec((QK_WIDTH, LANES)),
            _const_spec((1, GMLP_WIDTH)),
            _const_spec((1, GMLP_WIDTH)),
            _const_spec((N_GROUPS, CHUNK, CHUNK)),
            _const_spec((CHUNK, GMLP_WIDTH)),
        ],
        out_specs=[
            feat_spec(QK_WIDTH), tok_spec(QK_WIDTH), feat_spec(ATTN_WIDTH),
            tok_spec(GMLP_WIDTH), tok_spec(D_MODEL), tok_spec(D_MODEL),
        ],
        out_shape=[
            jax.ShapeDtypeStruct((BATCH, nt, QK_WIDTH, TM_PROJ), BF16),
            jax.ShapeDtypeStruct((BATCH, SEQ, QK_WIDTH), BF16),
            jax.ShapeDtypeStruct((BATCH, nt, ATTN_WIDTH, TM_PROJ), BF16),
            jax.ShapeDtypeStruct((BATCH, SEQ, GMLP_WIDTH), BF16),
            jax.ShapeDtypeStruct((BATCH, SEQ, D_MODEL), BF16),
            jax.ShapeDtypeStruct((BATCH, SEQ, D_MODEL), BF16),
        ],
        scratch_shapes=[pltpu.VMEM((QKV_COLS, D_MODEL), BF16),
                        pltpu.VMEM((D_MODEL, REST_COLS), BF16)],
        compiler_params=pltpu.CompilerParams(
            dimension_semantics=("arbitrary",), vmem_limit_bytes=VMEM_LIMIT),
        name="inproj",
    )(x3d, mod, w_in, qg_rows, kg_rows, ln_g, ln_b, w_spatial, bs_full)


def _rel_buckets(dist):
    n = np.maximum(dist, 0)
    max_exact = N_BUCKETS // 2
    nf = np.maximum(n, 1).astype(np.float32)
    large = max_exact + (np.log(nf / np.float32(max_exact))
                         / np.float32(math.log(MAX_DISTANCE / max_exact))
                         * np.float32(N_BUCKETS - max_exact)).astype(np.int32)
    large = np.minimum(large, N_BUCKETS - 1)
    return np.where(n < max_exact, n, large).astype(np.int32)


def _attn_kernel(table_ref, bound_ref, lam_ref, qt_ref, k_ref, vt_ref, bkt_ref, subg_ref,
                 o_ref, bias_ref, s_scr):
    h = pl.program_id(0)
    b = pl.program_id(1)

    @pl.when(b == 0)
    def _():
        far = table_ref[N_BUCKETS - 1, h]
        key = lax.broadcasted_iota(jnp.int32, (TQ, TQ), 0)
        qry = lax.broadcasted_iota(jnp.int32, (TQ, TQ), 1)
        for t in range(2):
            bkt = bkt_ref[t]
            tile = jnp.zeros((TQ, TQ), F32)
            for bucket in range(N_BUCKETS - 1):
                tile = jnp.where(bkt == bucket, (table_ref[bucket, h] - far) * LOG2E, tile)
            if t == 0:
                tile = jnp.where(key <= qry, tile, -jnp.inf)
            bias_ref[t] = jnp.concatenate([tile, tile], axis=1)

    @pl.when(bound_ref[0] <= MAX_SAFE_LOG2)
    def _():
        _attn_program(lam_ref, qt_ref, k_ref, vt_ref, subg_ref, o_ref, bias_ref, s_scr,
                      track_max=False)

    @pl.when(jnp.logical_not(bound_ref[0] <= MAX_SAFE_LOG2))
    def _():
        _attn_program(lam_ref, qt_ref, k_ref, vt_ref, subg_ref, o_ref, bias_ref, s_scr,
                      track_max=True)


def _attn_program(lam_ref, qt_ref, k_ref, vt_ref, subg_ref, o_ref, bias_ref, s_scr, *, track_max):
    lp = lam_ref[...]
    lam = (jnp.exp(jnp.sum(lp[0:1] * lp[1:2], axis=1, keepdims=True))
           - jnp.exp(jnp.sum(lp[2:3] * lp[3:4], axis=1, keepdims=True)) + LAM_INIT)
    subg = subg_ref[...]
    zeros = jnp.zeros((HEAD_DIM, TQ), BF16)
    ones = jnp.ones((SUM_ROWS, TQ), BF16)
    n_strips = 2 * TQ // LANES
    nq = SEQ // TQ
    blocks = [(qi, j) for qi in range(nq) for j in range(qi + 1)]
    qpads = {}

    def scores(qi, j):
        if qi not in qpads:
            qt = qt_ref[0, qi]
            qpads[qi] = jnp.concatenate(
                [jnp.concatenate([qt[0:HEAD_DIM], zeros], axis=0),
                 jnp.concatenate([zeros, qt[HEAD_DIM:2 * HEAD_DIM]], axis=0)], axis=1)
        return jnp.dot(k_ref[0, j * TQ:(j + 1) * TQ, :], qpads[qi],
                       preferred_element_type=F32)

    def probs_tracked(s, qi, j, m):
        p, alpha, m_out = [], [], []
        for c in range(n_strips):
            sc = s[:, c * LANES:(c + 1) * LANES]
            if j == qi:
                sc = sc + bias_ref[0, :, c * LANES:(c + 1) * LANES]
            elif j == qi - 1:
                sc = sc + bias_ref[1, :, c * LANES:(c + 1) * LANES]
            blk_max = jnp.max(sc, axis=0, keepdims=True)
            m_new = blk_max if j == 0 else jnp.maximum(m[c], blk_max)
            p.append(jnp.exp2(sc - m_new).astype(BF16))
            alpha.append(None if j == 0 else jnp.exp2(m[c] - m_new))
            m_out.append(m_new)
        return jnp.concatenate(p, axis=1), alpha, m_out

    def probs_bounded(s, qi, j):
        if j == qi:
            s = s + bias_ref[0]
        elif j == qi - 1:
            s = s + bias_ref[1]
        return jnp.exp2(s).astype(BF16)

    ahead = QK_AHEAD_TRACKED if track_max else QK_AHEAD_BOUNDED

    def issue(n):
        if track_max:
            return scores(*blocks[n])
        s_scr[n % S_SLOTS] = scores(*blocks[n])
        return None

    pending = [issue(n) for n in range(ahead)]
    m = acc = None
    for n, (qi, j) in enumerate(blocks):
        s = pending.pop(0)
        if n + ahead < len(blocks):
            pending.append(issue(n + ahead))
        if not track_max:
            s = s_scr[n % S_SLOTS]
        if track_max:
            p, alpha, m = probs_tracked(s, qi, j, m)
        else:
            p = probs_bounded(s, qi, j)
        vt_aug = jnp.concatenate([vt_ref[0, j], ones], axis=0)
        pv = jnp.dot(vt_aug, p, preferred_element_type=F32)[0:V_DIM + 8]
        if j == 0:
            acc = pv
        elif track_max:
            acc = jnp.concatenate(
                [alpha[c] * acc[:, c * LANES:(c + 1) * LANES] + pv[:, c * LANES:(c + 1) * LANES]
                 for c in range(n_strips)], axis=1)
        else:
            acc = acc + pv
        if j == qi:
            inv = 1.0 / acc[V_DIM:V_DIM + 1]
            ot = (acc[0:V_DIM, 0:TQ] * inv[:, 0:TQ]
                  - acc[0:V_DIM, TQ:2 * TQ] * (lam * inv[:, TQ:2 * TQ]))
            ms = jnp.mean(ot * ot, axis=0, keepdims=True)
            on = (ot * lax.rsqrt(ms + EPS)).T
            o_ref[0, qi * TQ:(qi + 1) * TQ, :] = (on * subg).astype(BF16)


def _attention(table, bound, lam, qt, k, vt, buckets, subg_row):
    return pl.pallas_call(
        _attn_kernel,
        grid=(N_HEADS, BATCH),
        in_specs=[
            pl.BlockSpec(memory_space=pltpu.SMEM),
            pl.BlockSpec(memory_space=pltpu.SMEM),
            _const_spec((4, HEAD_DIM)),
            pl.BlockSpec((1, SEQ // TQ, 2 * HEAD_DIM, TQ), lambda h, b: (b, 0, h, 0)),
            pl.BlockSpec((1, SEQ, 2 * HEAD_DIM), lambda h, b: (b, 0, h)),
            pl.BlockSpec((1, SEQ // TQ, V_DIM, TQ), lambda h, b: (b, 0, h, 0)),
            _const_spec((2, TQ, TQ)),
            _const_spec((1, V_DIM)),
        ],
        out_specs=pl.BlockSpec((1, SEQ, V_DIM), lambda h, b: (b, 0, h)),
        out_shape=jax.ShapeDtypeStruct((BATCH, SEQ, ATTN_WIDTH), BF16),
        scratch_shapes=[pltpu.VMEM((2, TQ, 2 * TQ), F32),
                        pltpu.VMEM((S_SLOTS, TQ, 2 * TQ), F32)],
        compiler_params=pltpu.CompilerParams(
            dimension_semantics=("arbitrary", "arbitrary"),
            vmem_limit_bytes=VMEM_LIMIT),
        name="diff_attn",
    )(table, bound, lam, qt, k, vt, buckets, subg_row)


def _mix_kernel(x_ref, mod_ref, o_ref, ub_ref, ga_ref, gb_ref, wa_ref, wb_ref, wo_ref,
                out_ref, wa_s, wb_s, wo_s):
    i = pl.program_id(0)

    @pl.when(i < W_STEPS)
    def _():
        _stash_rows(wa_s, wa_ref, i)
        _stash_rows(wb_s, wb_ref, i)
        _stash_rows(wo_s, wo_ref, i)

    @pl.when(i >= W_STEPS)
    def _():
        ya = jnp.dot(o_ref[...], wa_s[...], preferred_element_type=F32)
        yb = jnp.dot(ub_ref[...], wb_s[...], preferred_element_type=F32)
        merged = (ga_ref[...].astype(F32) * ya + gb_ref[...].astype(F32) * yb).astype(BF16)
        z = jnp.dot(merged, wo_s[...], preferred_element_type=F32)
        out_ref[...] = x_ref[...] + mod_ref[0, 5:6, :] * z


def _mix(x2d, mod, o2d, ub2d, ga2d, gb2d, wa, wb, wo):
    t = x2d.shape[0]
    tiles_per_batch = SEQ // TM_MIX
    tile = lambda i: jnp.maximum(i - W_STEPS, 0)
    tok = lambda: pl.BlockSpec((TM_MIX, D_MODEL), lambda i: (tile(i), 0))
    return pl.pallas_call(
        _mix_kernel,
        grid=(W_STEPS + t // TM_MIX,),
        in_specs=[
            tok(),
            pl.BlockSpec((1, N_MOD, D_MODEL), lambda i: (tile(i) // tiles_per_batch, 0, 0)),
            tok(), tok(), tok(), tok(),
            _weight_rows_spec(wa.shape),
            _weight_rows_spec(wb.shape),
            _weight_rows_spec(wo.shape),
        ],
        out_specs=tok(),
        out_shape=jax.ShapeDtypeStruct((t, D_MODEL), F32),
        scratch_shapes=[pltpu.VMEM(wa.shape, BF16), pltpu.VMEM(wb.shape, BF16),
                        pltpu.VMEM(wo.shape, BF16)],
        compiler_params=pltpu.CompilerParams(
            dimension_semantics=("arbitrary",), vmem_limit_bytes=VMEM_LIMIT),
        name="mix_out",
    )(x2d, mod, o2d, ub2d, ga2d, gb2d, wa, wb, wo)


def kernel(x, c, w_ada, b_ada, w_ffn1_gate, w_ffn1_up, w_ffn1_down, w_in, q_norm_g, k_norm_g, lam_q1, lam_k1, lam_q2, lam_k2, subln_g, rel_bias_table, gmlp_ln_g, gmlp_ln_b, w_spatial, b_spatial, w_a_proj, w_b_proj, w_o, w_ffn2_gate, w_ffn2_up, w_ffn2_down):
    b, s, d = x.shape
    t = b * s

    mod = _adaln_mod(c, w_ada[0], b_ada[0]).reshape(b, N_MOD, d)

    x1 = _ffn(x.reshape(t, d), mod, w_ffn1_gate[0], w_ffn1_up[0], w_ffn1_down[0], sub=0)

    scale = HEAD_DIM ** -0.5 * LOG2E
    qg_rows = jnp.broadcast_to(
        jnp.tile(q_norm_g[0] * scale, QK_WIDTH // HEAD_DIM)[:, None], (QK_WIDTH, LANES))
    kg_rows = jnp.broadcast_to(
        jnp.tile(k_norm_g[0], QK_WIDTH // HEAD_DIM)[:, None], (QK_WIDTH, LANES))
    bs_full = jnp.repeat(b_spatial[0].T, GROUP_DIM, axis=1)
    qt, k, vt, ub, ga, gb = _inproj(
        x1.reshape(b, s, d), mod, w_in[0], qg_rows, kg_rows,
        gmlp_ln_g[0].reshape(1, -1), gmlp_ln_b[0].reshape(1, -1), w_spatial[0], bs_full)

    lam = jnp.stack([lam_q1[0], lam_k1[0], lam_q2[0], lam_k2[0]])
    kk = np.arange(TQ)[:, None]
    qq = np.arange(TQ)[None, :]
    buckets = jnp.asarray(np.stack([_rel_buckets(qq - kk), _rel_buckets(TQ + qq - kk)]))
    subg_row = (subln_g[0] * (1.0 - LAM_INIT)).reshape(1, V_DIM)
    far_bias = rel_bias_table[N_BUCKETS - 1:N_BUCKETS]
    score_bound = (HEAD_DIM * jnp.max(jnp.abs(q_norm_g[0] * scale)) * jnp.max(jnp.abs(k_norm_g[0]))
                   * BF16_SLACK + LOG2E * jnp.max(jnp.abs(rel_bias_table - far_bias))).reshape(1)
    o = _attention(rel_bias_table, score_bound, lam, qt, k, vt, buckets, subg_row)

    x2 = _mix(x1, mod, o.reshape(t, ATTN_WIDTH), ub.reshape(t, GMLP_WIDTH),
              ga.reshape(t, d), gb.reshape(t, d), w_a_proj[0], w_b_proj[0], w_o[0])

    x3 = _ffn(x2, mod, w_ffn2_gate[0], w_ffn2_up[0], w_ffn2_down[0], sub=2)
    return x3.reshape(b, s, d)
```

```python
import functools
import math

import numpy as np
import jax
import jax.numpy as jnp
from jax import lax
from jax.experimental import pallas as pl
from jax.experimental.pallas import tpu as pltpu

F32 = jnp.float32
BF16 = jnp.bfloat16

D_MODEL = 1024
BATCH = 8
SEQ = 2048
N_HEADS = 8
HEAD_DIM = 64
V_DIM = 2 * HEAD_DIM
QK_WIDTH = N_HEADS * 2 * HEAD_DIM
ATTN_WIDTH = N_HEADS * V_DIM
N_GROUPS = 8
CHUNK = 128
GMLP_WIDTH = 1024
GROUP_DIM = GMLP_WIDTH // N_GROUPS
N_BUCKETS = 32
MAX_DISTANCE = 128
D_FF = 2816
N_MOD = 9
EPS = 1e-6
LAM_INIT = 0.8 - 0.6 * math.exp(-0.3 * 0)
LOG2E = math.log2(math.e)
QKV_COLS = 2 * QK_WIDTH + ATTN_WIDTH
REST_COLS = 2 * GMLP_WIDTH + 2 * D_MODEL

LANES = 128
VMEM_LIMIT = 56 * 1024 * 1024

TM_FFN = 512
TM_PROJ = 256
TM_MIX = 512
TQ = 256
MOD_TN = 1536
SUM_ROWS = 16
W_STEPS = 8
MAX_SAFE_LOG2 = 64.0
BF16_SLACK = 1.02
QK_AHEAD_TRACKED = 2
QK_AHEAD_BOUNDED = 3
S_SLOTS = QK_AHEAD_BOUNDED + 1


def _const_spec(shape):
    nd = len(shape)
    return pl.BlockSpec(shape, lambda *_: (0,) * nd, pipeline_mode=pl.Buffered(1))


def _weight_rows_spec(shape):
    rows, cols = shape
    return pl.BlockSpec((rows // W_STEPS, cols),
                        lambda i: (jnp.minimum(i, W_STEPS - 1), 0))


def _stash_rows(dst_ref, chunk_ref, i):
    rows = chunk_ref.shape[0]
    dst_ref[pl.ds(pl.multiple_of(i * rows, rows), rows), :] = chunk_ref[...].astype(BF16)


def _silu(x):
    return x * jax.nn.sigmoid(x)


def _gelu(x):
    return 0.5 * x * (1.0 + lax.erf(x * (1.0 / math.sqrt(2.0))))


def _modulated_rmsnorm(x, shift, scale):
    ms = jnp.mean(x * x, axis=-1, keepdims=True)
    return (x * lax.rsqrt(ms + EPS)) * (1.0 + scale) + shift


def _mod_kernel(c_ref, w_ref, b_ref, o_ref):
    a = _silu(c_ref[...]).astype(BF16)
    o_ref[...] = jnp.dot(a, w_ref[...].astype(BF16),
                         preferred_element_type=F32) + b_ref[...]


def _adaln_mod(c, w_ada, b_ada):
    n = w_ada.shape[1]
    return pl.pallas_call(
        _mod_kernel,
        grid=(n // MOD_TN,),
        in_specs=[
            pl.BlockSpec((BATCH, D_MODEL), lambda j: (0, 0)),
            pl.BlockSpec((D_MODEL, MOD_TN), lambda j: (0, j)),
            pl.BlockSpec((1, MOD_TN), lambda j: (0, j)),
        ],
        out_specs=pl.BlockSpec((BATCH, MOD_TN), lambda j: (0, j)),
        out_shape=jax.ShapeDtypeStruct((BATCH, n), F32),
        compiler_params=pltpu.CompilerParams(
            dimension_semantics=("arbitrary",), vmem_limit_bytes=VMEM_LIMIT),
        name="adaln_mod",
    )(c, w_ada, b_ada.reshape(1, n))


def _ffn_kernel(x_ref, mod_ref, wg_ref, wu_ref, wd_ref, o_ref, wg_s, wu_s, wd_s, *, sub):
    i = pl.program_id(0)

    @pl.when(i < W_STEPS)
    def _():
        _stash_rows(wg_s, wg_ref, i)
        _stash_rows(wu_s, wu_ref, i)
        _stash_rows(wd_s, wd_ref, i)

    @pl.when(i >= W_STEPS)
    def _():
        x = x_ref[...]
        shift = mod_ref[0, 3 * sub + 0:3 * sub + 1, :]
        scale = mod_ref[0, 3 * sub + 1:3 * sub + 2, :]
        gate = mod_ref[0, 3 * sub + 2:3 * sub + 3, :]
        h = _modulated_rmsnorm(x, shift, scale).astype(BF16)
        g = jnp.dot(h, wg_s[...], preferred_element_type=F32)
        u = jnp.dot(h, wu_s[...], preferred_element_type=F32)
        a = (_silu(g) * u).astype(BF16)
        y = jnp.dot(a, wd_s[...], preferred_element_type=F32)
        o_ref[...] = x + (0.5 * gate) * y


def _ffn(x2d, mod, wg, wu, wd, *, sub):
    t = x2d.shape[0]
    tiles_per_batch = SEQ // TM_FFN
    tile = lambda i: jnp.maximum(i - W_STEPS, 0)
    return pl.pallas_call(
        functools.partial(_ffn_kernel, sub=sub),
        grid=(W_STEPS + t // TM_FFN,),
        in_specs=[
            pl.BlockSpec((TM_FFN, D_MODEL), lambda i: (tile(i), 0)),
            pl.BlockSpec((1, N_MOD, D_MODEL), lambda i: (tile(i) // tiles_per_batch, 0, 0)),
            _weight_rows_spec(wg.shape),
            _weight_rows_spec(wu.shape),
            _weight_rows_spec(wd.shape),
        ],
        out_specs=pl.BlockSpec((TM_FFN, D_MODEL), lambda i: (tile(i), 0)),
        out_shape=jax.ShapeDtypeStruct((t, D_MODEL), F32),
        scratch_shapes=[pltpu.VMEM(wg.shape, BF16), pltpu.VMEM(wu.shape, BF16),
                        pltpu.VMEM(wd.shape, BF16)],
        compiler_params=pltpu.CompilerParams(
            dimension_semantics=("arbitrary",), vmem_limit_bytes=VMEM_LIMIT),
        name=f"ffn{sub}",
    )(x2d, mod, wg, wu, wd)


def _chunk_rmsnorm_rows(xt, gain_rows):
    width, tm = xt.shape
    x3 = xt.reshape(width // HEAD_DIM, HEAD_DIM, tm)
    ms = jnp.mean(x3 * x3, axis=1, keepdims=True)
    xn = (x3 * lax.rsqrt(ms + EPS)).reshape(width, tm)
    return jnp.concatenate(
        [xn[:, j * LANES:(j + 1) * LANES] * gain_rows for j in range(tm // LANES)], axis=1)


def _inproj_tile(x_ref, mod_ref, wqkvt_ref, wrest_ref, qg_ref, kg_ref,
                 lng_ref, lnb_ref, ws_ref, bs_ref,
                 qt_ref, k_ref, vt_ref, ub_ref, ga_ref, gb_ref):
    x = x_ref[...]
    shift = mod_ref[0, 3:4, :]
    scale = mod_ref[0, 4:5, :]
    h = _modulated_rmsnorm(x, shift, scale).astype(BF16)

    pt = lax.dot_general(wqkvt_ref[...], h, (((1,), (1,)), ((), ())),
                         preferred_element_type=F32)
    qt = _chunk_rmsnorm_rows(pt[0:QK_WIDTH], qg_ref[...])
    qt_ref[0, 0] = qt.astype(BF16)
    kt = _chunk_rmsnorm_rows(pt[QK_WIDTH:2 * QK_WIDTH], kg_ref[...])
    k_ref[0] = kt.T.astype(BF16)
    vt_ref[0, 0] = pt[2 * QK_WIDTH:QKV_COLS].astype(BF16)

    rest = jnp.dot(h, wrest_ref[...], preferred_element_type=F32)
    u = _gelu(rest[:, 0:GMLP_WIDTH])
    gv = _gelu(rest[:, GMLP_WIDTH:2 * GMLP_WIDTH])
    ga_ref[0] = jax.nn.sigmoid(rest[:, 2 * GMLP_WIDTH:2 * GMLP_WIDTH + D_MODEL]).astype(BF16)
    gb_ref[0] = jax.nn.sigmoid(rest[:, 2 * GMLP_WIDTH + D_MODEL:REST_COLS]).astype(BF16)

    mu = jnp.mean(gv, axis=-1, keepdims=True)
    cen = gv - mu
    var = jnp.mean(cen * cen, axis=-1, keepdims=True)
    vln = ((cen * lax.rsqrt(var + EPS)) * lng_ref[...] + lnb_ref[...]).astype(BF16)

    tm = x.shape[0]
    row = lax.broadcasted_iota(jnp.int32, (CHUNK, CHUNK), 0)
    col = lax.broadcasted_iota(jnp.int32, (CHUNK, CHUNK), 1)
    causal = row >= col
    for g in range(N_GROUPS):
        w = jnp.where(causal, ws_ref[g], 0.0).astype(BF16)
        cs = slice(g * GROUP_DIM, (g + 1) * GROUP_DIM)
        for n in range(tm // CHUNK):
            rs = slice(n * CHUNK, (n + 1) * CHUNK)
            f = jnp.dot(w, vln[rs, cs], preferred_element_type=F32) + bs_ref[:, cs]
            ub_ref[0, rs, cs] = (u[rs, cs] * f).astype(BF16)


def _inproj_kernel(x_ref, mod_ref, win_ref, qg_ref, kg_ref, lng_ref, lnb_ref, ws_ref, bs_ref,
                   qt_ref, k_ref, vt_ref, ub_ref, ga_ref, gb_ref, wqkvt_s, wrest_s):
    i = pl.program_id(0)

    @pl.when(i < W_STEPS)
    def _():
        rows = win_ref.shape[0]
        r = pl.multiple_of(i * rows, rows)
        chunk = win_ref[...]
        wqkvt_s[:, pl.ds(r, rows)] = chunk[:, 0:QKV_COLS].T.astype(BF16)
        wrest_s[pl.ds(r, rows), :] = chunk[:, QKV_COLS:QKV_COLS + REST_COLS].astype(BF16)

    @pl.when(i >= W_STEPS)
    def _():
        _inproj_tile(x_ref, mod_ref, wqkvt_s, wrest_s, qg_ref, kg_ref, lng_ref, lnb_ref,
                     ws_ref, bs_ref, qt_ref, k_ref, vt_ref, ub_ref, ga_ref, gb_ref)


def _inproj(x3d, mod, w_in, qg_rows, kg_rows, ln_g, ln_b, w_spatial, bs_full):
    nt = SEQ // TM_PROJ
    tile = lambda i: jnp.maximum(i - W_STEPS, 0)
    tok_spec = lambda width: pl.BlockSpec(
        (1, TM_PROJ, width), lambda i: (tile(i) // nt, tile(i) % nt, 0))
    feat_spec = lambda width: pl.BlockSpec(
        (1, 1, width, TM_PROJ), lambda i: (tile(i) // nt, tile(i) % nt, 0, 0))
    return pl.pallas_call(
        _inproj_kernel,
        grid=(W_STEPS + BATCH * nt,),
        in_specs=[
            pl.BlockSpec((None, TM_PROJ, D_MODEL), lambda i: (tile(i) // nt, tile(i) % nt, 0)),
            pl.BlockSpec((1, N_MOD, D_MODEL), lambda i: (tile(i) // nt, 0, 0)),
            _weight_rows_spec(w_in.shape),
            _const_spec((QK_WIDTH, LANES)),
            _const_spec((QK_WIDTH, LANES)),
            _const_spec((1, GMLP_WIDTH)),
            _const_spec((1, GMLP_WIDTH)),
            _const_spec((N_GROUPS, CHUNK, CHUNK)),
            _const_spec((CHUNK, GMLP_WIDTH)),
        ],
        out_specs=[
            feat_spec(QK_WIDTH), tok_spec(QK_WIDTH), feat_spec(ATTN_WIDTH),
            tok_spec(GMLP_WIDTH), tok_spec(D_MODEL), tok_spec(D_MODEL),
        ],
        out_shape=[
            jax.ShapeDtypeStruct((BATCH, nt, QK_WIDTH, TM_PROJ), BF16),
            jax.ShapeDtypeStruct((BATCH, SEQ, QK_WIDTH), BF16),
            jax.ShapeDtypeStruct((BATCH, nt, ATTN_WIDTH, TM_PROJ), BF16),
            jax.ShapeDtypeStruct((BATCH, SEQ, GMLP_WIDTH), BF16),
            jax.ShapeDtypeStruct((BATCH, SEQ, D_MODEL), BF16),
            jax.ShapeDtypeStruct((BATCH, SEQ, D_MODEL), BF16),
        ],
        scratch_shapes=[pltpu.VMEM((QKV_COLS, D_MODEL), BF16),
                        pltpu.VMEM((D_MODEL, REST_COLS), BF16)],
        compiler_params=pltpu.CompilerParams(
            dimension_semantics=("arbitrary",), vmem_limit_bytes=VMEM_LIMIT),
        name="inproj",
    )(x3d, mod, w_in, qg_rows, kg_rows, ln_g, ln_b, w_spatial, bs_full)


def _rel_buckets(dist):
    n = np.maximum(dist, 0)
    max_exact = N_BUCKETS // 2
    nf = np.maximum(n, 1).astype(np.float32)
    large = max_exact + (np.log(nf / np.float32(max_exact))
                         / np.float32(math.log(MAX_DISTANCE / max_exact))
                         * np.float32(N_BUCKETS - max_exact)).astype(np.int32)
    large = np.minimum(large, N_BUCKETS - 1)
    return np.where(n < max_exact, n, large).astype(np.int32)


def _attn_kernel(table_ref, bound_ref, lam_ref, qt_ref, k_ref, vt_ref, bkt_ref, subg_ref,
                 o_ref, bias_ref, s_scr):
    h = pl.program_id(0)
    b = pl.program_id(1)

    @pl.when(b == 0)
    def _():
        far = table_ref[N_BUCKETS - 1, h]
        key = lax.broadcasted_iota(jnp.int32, (TQ, TQ), 0)
        qry = lax.broadcasted_iota(jnp.int32, (TQ, TQ), 1)
        for t in range(2):
            bkt = bkt_ref[t]
            tile = jnp.zeros((TQ, TQ), F32)
            for bucket in range(N_BUCKETS - 1):
                tile = jnp.where(bkt == bucket, (table_ref[bucket, h] - far) * LOG2E, tile)
            if t == 0:
                tile = jnp.where(key <= qry, tile, -jnp.inf)
            bias_ref[t] = jnp.concatenate([tile, tile], axis=1)

    @pl.when(bound_ref[0] <= MAX_SAFE_LOG2)
    def _():
        _attn_program(lam_ref, qt_ref, k_ref, vt_ref, subg_ref, o_ref, bias_ref, s_scr,
                      track_max=False)

    @pl.when(jnp.logical_not(bound_ref[0] <= MAX_SAFE_LOG2))
    def _():
        _attn_program(lam_ref, qt_ref, k_ref, vt_ref, subg_ref, o_ref, bias_ref, s_scr,
                      track_max=True)


def _attn_program(lam_ref, qt_ref, k_ref, vt_ref, subg_ref, o_ref, bias_ref, s_scr, *, track_max):
    lp = lam_ref[...]
    lam = (jnp.exp(jnp.sum(lp[0:1] * lp[1:2], axis=1, keepdims=True))
           - jnp.exp(jnp.sum(lp[2:3] * lp[3:4], axis=1, keepdims=True)) + LAM_INIT)
    subg = subg_ref[...]
    zeros = jnp.zeros((HEAD_DIM, TQ), BF16)
    ones = jnp.ones((SUM_ROWS, TQ), BF16)
    n_strips = 2 * TQ // LANES
    nq = SEQ // TQ
    blocks = [(qi, j) for qi in range(nq) for j in range(qi + 1)]
    qpads = {}

    def qpad(qi):
        if qi not in qpads:
            qt = qt_ref[0, qi]
            qpads[qi] = jnp.concatenate(
                [jnp.concatenate([qt[0:HEAD_DIM], zeros], axis=0),
                 jnp.concatenate([zeros, qt[HEAD_DIM:2 * HEAD_DIM]], axis=0)], axis=1)
        return qpads[qi]

    def scores(qi, j):
        return jnp.dot(k_ref[0, j * TQ:(j + 1) * TQ, :], qpad(qi),
                       preferred_element_type=F32)

    def probs_tracked(s, qi, j, m):
        p, alpha, m_out = [], [], []
        for c in range(n_strips):
            sc = s[:, c * LANES:(c + 1) * LANES]
            if j == qi:
                sc = sc + bias_ref[0, :, c * LANES:(c + 1) * LANES]
            elif j == qi - 1:
                sc = sc + bias_ref[1, :, c * LANES:(c + 1) * LANES]
            blk_max = jnp.max(sc, axis=0, keepdims=True)
            m_new = blk_max if j == 0 else jnp.maximum(m[c], blk_max)
            p.append(jnp.exp2(sc - m_new).astype(BF16))
            alpha.append(None if j == 0 else jnp.exp2(m[c] - m_new))
            m_out.append(m_new)
        return jnp.concatenate(p, axis=1), alpha, m_out

    half = TQ // 2
    hi_cols = (slice(half, TQ), slice(TQ + half, 2 * TQ))

    def issue_bounded(n):
        qi, j = blocks[n]
        slot = s_scr.at[n % S_SLOTS]
        if j < qi:
            slot[...] = scores(qi, j)
            return
        kb = k_ref[0, j * TQ:(j + 1) * TQ, :]
        qp = qpad(qi)
        slot[0:half, :] = jnp.dot(kb[0:half], qp, preferred_element_type=F32)
        s_hi = jnp.dot(kb[half:TQ], jnp.concatenate([qp[:, c] for c in hi_cols], axis=1),
                       preferred_element_type=F32)
        for t, c in enumerate(hi_cols):
            slot[half:TQ, c] = s_hi[:, t * half:(t + 1) * half]

    def probs_bounded(n):
        qi, j = blocks[n]
        slot = s_scr.at[n % S_SLOTS]
        if j < qi - 1:
            return jnp.exp2(slot[...])
        if j == qi - 1:
            return jnp.exp2(slot[...] + bias_ref[1])
        lo = jnp.exp2(slot[0:half, :] + bias_ref[0, 0:half, :])
        z = jnp.zeros((half, half), F32)
        hi = [jnp.exp2(slot[half:TQ, c] + bias_ref[0, half:TQ, c]) for c in hi_cols]
        return jnp.concatenate([lo, jnp.concatenate([z, hi[0], z, hi[1]], axis=1)], axis=0)

    ahead = QK_AHEAD_TRACKED if track_max else QK_AHEAD_BOUNDED

    issue = (lambda n: scores(*blocks[n])) if track_max else issue_bounded
    pending = [issue(n) for n in range(ahead)]
    m = acc = psum = None
    for n, (qi, j) in enumerate(blocks):
        s = pending.pop(0)
        if n + ahead < len(blocks):
            pending.append(issue(n + ahead))
        if track_max:
            p, alpha, m = probs_tracked(s, qi, j, m)
            vt_aug = jnp.concatenate([vt_ref[0, j], ones], axis=0)
            pv = jnp.dot(vt_aug, p, preferred_element_type=F32)[0:V_DIM + 8]
            if j == 0:
                acc = pv
            else:
                acc = jnp.concatenate(
                    [alpha[c] * acc[:, c * LANES:(c + 1) * LANES] + pv[:, c * LANES:(c + 1) * LANES]
                     for c in range(n_strips)], axis=1)
        else:
            p = probs_bounded(n)
            blk_sum = jnp.sum(p, axis=0, keepdims=True)
            pv = jnp.dot(vt_ref[0, j], p.astype(BF16), preferred_element_type=F32)
            acc = pv if j == 0 else acc + pv
            psum = blk_sum if j == 0 else psum + blk_sum
        if j == qi:
            inv = 1.0 / (acc[V_DIM:V_DIM + 1] if track_max else psum)
            ot = (acc[0:V_DIM, 0:TQ] * inv[:, 0:TQ]
                  - acc[0:V_DIM, TQ:2 * TQ] * (lam * inv[:, TQ:2 * TQ]))
            ms = jnp.mean(ot * ot, axis=0, keepdims=True)
            on = (ot * lax.rsqrt(ms + EPS)).T
            o_ref[0, qi * TQ:(qi + 1) * TQ, :] = (on * subg).astype(BF16)


def _attention(table, bound, lam, qt, k, vt, buckets, subg_row):
    return pl.pallas_call(
        _attn_kernel,
        grid=(N_HEADS, BATCH),
        in_specs=[
            pl.BlockSpec(memory_space=pltpu.SMEM),
            pl.BlockSpec(memory_space=pltpu.SMEM),
            _const_spec((4, HEAD_DIM)),
            pl.BlockSpec((1, SEQ // TQ, 2 * HEAD_DIM, TQ), lambda h, b: (b, 0, h, 0)),
            pl.BlockSpec((1, SEQ, 2 * HEAD_DIM), lambda h, b: (b, 0, h)),
            pl.BlockSpec((1, SEQ // TQ, V_DIM, TQ), lambda h, b: (b, 0, h, 0)),
            _const_spec((2, TQ, TQ)),
            _const_spec((1, V_DIM)),
        ],
        out_specs=pl.BlockSpec((1, SEQ, V_DIM), lambda h, b: (b, 0, h)),
        out_shape=jax.ShapeDtypeStruct((BATCH, SEQ, ATTN_WIDTH), BF16),
        scratch_shapes=[pltpu.VMEM((2, TQ, 2 * TQ), F32),
                        pltpu.VMEM((S_SLOTS, TQ, 2 * TQ), F32)],
        compiler_params=pltpu.CompilerParams(
            dimension_semantics=("arbitrary", "arbitrary"),
            vmem_limit_bytes=VMEM_LIMIT),
        name="diff_attn",
    )(table, bound, lam, qt, k, vt, buckets, subg_row)


def _mix_kernel(x_ref, mod_ref, o_ref, ub_ref, ga_ref, gb_ref, wa_ref, wb_ref, wo_ref,
                out_ref, wa_s, wb_s, wo_s):
    i = pl.program_id(0)

    @pl.when(i < W_STEPS)
    def _():
        _stash_rows(wa_s, wa_ref, i)
        _stash_rows(wb_s, wb_ref, i)
        _stash_rows(wo_s, wo_ref, i)

    @pl.when(i >= W_STEPS)
    def _():
        ya = jnp.dot(o_ref[...], wa_s[...], preferred_element_type=F32)
        yb = jnp.dot(ub_ref[...], wb_s[...], preferred_element_type=F32)
        merged = (ga_ref[...].astype(F32) * ya + gb_ref[...].astype(F32) * yb).astype(BF16)
        z = jnp.dot(merged, wo_s[...], preferred_element_type=F32)
        out_ref[...] = x_ref[...] + mod_ref[0, 5:6, :] * z


def _mix(x2d, mod, o2d, ub2d, ga2d, gb2d, wa, wb, wo):
    t = x2d.shape[0]
    tiles_per_batch = SEQ // TM_MIX
    tile = lambda i: jnp.maximum(i - W_STEPS, 0)
    tok = lambda: pl.BlockSpec((TM_MIX, D_MODEL), lambda i: (tile(i), 0))
    return pl.pallas_call(
        _mix_kernel,
        grid=(W_STEPS + t // TM_MIX,),
        in_specs=[
            tok(),
            pl.BlockSpec((1, N_MOD, D_MODEL), lambda i: (tile(i) // tiles_per_batch, 0, 0)),
            tok(), tok(), tok(), tok(),
            _weight_rows_spec(wa.shape),
            _weight_rows_spec(wb.shape),
            _weight_rows_spec(wo.shape),
        ],
        out_specs=tok(),
        out_shape=jax.ShapeDtypeStruct((t, D_MODEL), F32),
        scratch_shapes=[pltpu.VMEM(wa.shape, BF16), pltpu.VMEM(wb.shape, BF16),
                        pltpu.VMEM(wo.shape, BF16)],
        compiler_params=pltpu.CompilerParams(
            dimension_semantics=("arbitrary",), vmem_limit_bytes=VMEM_LIMIT),
        name="mix_out",
    )(x2d, mod, o2d, ub2d, ga2d, gb2d, wa, wb, wo)


def kernel(x, c, w_ada, b_ada, w_ffn1_gate, w_ffn1_up, w_ffn1_down, w_in, q_norm_g, k_norm_g, lam_q1, lam_k1, lam_q2, lam_k2, subln_g, rel_bias_table, gmlp_ln_g, gmlp_ln_b, w_spatial, b_spatial, w_a_proj, w_b_proj, w_o, w_ffn2_gate, w_ffn2_up, w_ffn2_down):
    b, s, d = x.shape
    t = b * s

    mod = _adaln_mod(c, w_ada[0], b_ada[0]).reshape(b, N_MOD, d)

    x1 = _ffn(x.reshape(t, d), mod, w_ffn1_gate[0], w_ffn1_up[0], w_ffn1_down[0], sub=0)

    scale = HEAD_DIM ** -0.5 * LOG2E
    qg_rows = jnp.broadcast_to(
        jnp.tile(q_norm_g[0] * scale, QK_WIDTH // HEAD_DIM)[:, None], (QK_WIDTH, LANES))
    kg_rows = jnp.broadcast_to(
        jnp.tile(k_norm_g[0], QK_WIDTH // HEAD_DIM)[:, None], (QK_WIDTH, LANES))
    bs_full = jnp.repeat(b_spatial[0].T, GROUP_DIM, axis=1)
    qt, k, vt, ub, ga, gb = _inproj(
        x1.reshape(b, s, d), mod, w_in[0], qg_rows, kg_rows,
        gmlp_ln_g[0].reshape(1, -1), gmlp_ln_b[0].reshape(1, -1), w_spatial[0], bs_full)

    lam = jnp.stack([lam_q1[0], lam_k1[0], lam_q2[0], lam_k2[0]])
    kk = np.arange(TQ)[:, None]
    qq = np.arange(TQ)[None, :]
    buckets = jnp.asarray(np.stack([_rel_buckets(qq - kk), _rel_buckets(TQ + qq - kk)]))
    subg_row = (subln_g[0] * (1.0 - LAM_INIT)).reshape(1, V_DIM)
    far_bias = rel_bias_table[N_BUCKETS - 1:N_BUCKETS]
    score_bound = (HEAD_DIM * jnp.max(jnp.abs(q_norm_g[0] * scale)) * jnp.max(jnp.abs(k_norm_g[0]))
                   * BF16_SLACK + LOG2E * jnp.max(jnp.abs(rel_bias_table - far_bias))).reshape(1)
    o = _attention(rel_bias_table, score_bound, lam, qt, k, vt, buckets, subg_row)

    x2 = _mix(x1, mod, o.reshape(t, ATTN_WIDTH), ub.reshape(t, GMLP_WIDTH),
              ga.reshape(t, d), gb.reshape(t, d), w_a_proj[0], w_b_proj[0], w_o[0])

    x3 = _ffn(x2, mod, w_ffn2_gate[0], w_ffn2_up[0], w_ffn2_down[0], sub=2)
    return x3.reshape(b, s, d)
```

```python
import functools
import math

import numpy as np
import jax
import jax.numpy as jnp
from jax import lax
from jax.experimental import pallas as pl
from jax.experimental.pallas import tpu as pltpu

F32 = jnp.float32
BF16 = jnp.bfloat16

D_MODEL = 1024
BATCH = 8
SEQ = 2048
N_HEADS = 8
HEAD_DIM = 64
V_DIM = 2 * HEAD_DIM
QK_WIDTH = N_HEADS * 2 * HEAD_DIM
ATTN_WIDTH = N_HEADS * V_DIM
N_GROUPS = 8
CHUNK = 128
GMLP_WIDTH = 1024
GROUP_DIM = GMLP_WIDTH // N_GROUPS
N_BUCKETS = 32
MAX_DISTANCE = 128
D_FF = 2816
N_MOD = 9
EPS = 1e-6
LAM_INIT = 0.8 - 0.6 * math.exp(-0.3 * 0)
LOG2E = math.log2(math.e)
QKV_COLS = 2 * QK_WIDTH + ATTN_WIDTH
REST_COLS = 2 * GMLP_WIDTH + 2 * D_MODEL

LANES = 128
VMEM_LIMIT = 56 * 1024 * 1024

FFN_CHUNK = 256
TM_FFN = 1024
TM_PROJ = 512
MIX_CHUNK = 256
TM_MIX = 1024
TQ = 256
MOD_TN = 1536
SUM_ROWS = 16
W_STEPS = 8
MAX_SAFE_LOG2 = 64.0
BF16_SLACK = 1.02
QK_AHEAD_TRACKED = 2
QK_AHEAD_BOUNDED = 3
S_SLOTS = QK_AHEAD_BOUNDED + 1


def _const_spec(shape):
    nd = len(shape)
    return pl.BlockSpec(shape, lambda *_: (0,) * nd, pipeline_mode=pl.Buffered(1))


def _weight_rows_spec(shape):
    rows, cols = shape
    return pl.BlockSpec((rows // W_STEPS, cols),
                        lambda i: (jnp.minimum(i, W_STEPS - 1), 0))


def _stash_rows(dst_ref, chunk_ref, i):
    rows = chunk_ref.shape[0]
    dst_ref[pl.ds(pl.multiple_of(i * rows, rows), rows), :] = chunk_ref[...].astype(BF16)


def _silu(x):
    return x * jax.nn.sigmoid(x)


def _gelu(x):
    return 0.5 * x * (1.0 + lax.erf(x * (1.0 / math.sqrt(2.0))))


def _modulated_rmsnorm(x, shift, scale):
    ms = jnp.mean(x * x, axis=-1, keepdims=True)
    return (x * lax.rsqrt(ms + EPS)) * (1.0 + scale) + shift


def _mod_kernel(c_ref, w_ref, b_ref, o_ref):
    a = _silu(c_ref[...]).astype(BF16)
    o_ref[...] = jnp.dot(a, w_ref[...].astype(BF16),
                         preferred_element_type=F32) + b_ref[...]


def _adaln_mod(c, w_ada, b_ada):
    n = w_ada.shape[1]
    return pl.pallas_call(
        _mod_kernel,
        grid=(n // MOD_TN,),
        in_specs=[
            pl.BlockSpec((BATCH, D_MODEL), lambda j: (0, 0)),
            pl.BlockSpec((D_MODEL, MOD_TN), lambda j: (0, j)),
            pl.BlockSpec((1, MOD_TN), lambda j: (0, j)),
        ],
        out_specs=pl.BlockSpec((BATCH, MOD_TN), lambda j: (0, j)),
        out_shape=jax.ShapeDtypeStruct((BATCH, n), F32),
        compiler_params=pltpu.CompilerParams(
            dimension_semantics=("arbitrary",), vmem_limit_bytes=VMEM_LIMIT),
        name="adaln_mod",
    )(c, w_ada, b_ada.reshape(1, n))


def _ffn_kernel(x_ref, mod_ref, wg_ref, wu_ref, wd_ref, o_ref, wg_s, wu_s, wd_s, *, sub):
    i = pl.program_id(0)

    @pl.when(i < W_STEPS)
    def _():
        _stash_rows(wg_s, wg_ref, i)
        _stash_rows(wu_s, wu_ref, i)
        _stash_rows(wd_s, wd_ref, i)

    @pl.when(i >= W_STEPS)
    def _():
        shift = mod_ref[0, 3 * sub + 0:3 * sub + 1, :]
        scale = mod_ref[0, 3 * sub + 1:3 * sub + 2, :]
        gate = mod_ref[0, 3 * sub + 2:3 * sub + 3, :]
        for r in range(0, TM_FFN, FFN_CHUNK):
            x = x_ref[r:r + FFN_CHUNK, :]
            h = _modulated_rmsnorm(x, shift, scale).astype(BF16)
            g = jnp.dot(h, wg_s[...], preferred_element_type=F32)
            u = jnp.dot(h, wu_s[...], preferred_element_type=F32)
            a = (_silu(g) * u).astype(BF16)
            y = jnp.dot(a, wd_s[...], preferred_element_type=F32)
            o_ref[r:r + FFN_CHUNK, :] = x + (0.5 * gate) * y


def _ffn(x2d, mod, wg, wu, wd, *, sub):
    t = x2d.shape[0]
    tiles_per_batch = SEQ // TM_FFN
    tile = lambda i: jnp.maximum(i - W_STEPS, 0)
    return pl.pallas_call(
        functools.partial(_ffn_kernel, sub=sub),
        grid=(W_STEPS + t // TM_FFN,),
        in_specs=[
            pl.BlockSpec((TM_FFN, D_MODEL), lambda i: (tile(i), 0)),
            pl.BlockSpec((1, N_MOD, D_MODEL), lambda i: (tile(i) // tiles_per_batch, 0, 0)),
            _weight_rows_spec(wg.shape),
            _weight_rows_spec(wu.shape),
            _weight_rows_spec(wd.shape),
        ],
        out_specs=pl.BlockSpec((TM_FFN, D_MODEL), lambda i: (tile(i), 0)),
        out_shape=jax.ShapeDtypeStruct((t, D_MODEL), F32),
        scratch_shapes=[pltpu.VMEM(wg.shape, BF16), pltpu.VMEM(wu.shape, BF16),
                        pltpu.VMEM(wd.shape, BF16)],
        compiler_params=pltpu.CompilerParams(
            dimension_semantics=("arbitrary",), vmem_limit_bytes=VMEM_LIMIT),
        name=f"ffn{sub}",
    )(x2d, mod, wg, wu, wd)


def _chunk_rmsnorm_rows(xt, gain_rows):
    width, tm = xt.shape
    x3 = xt.reshape(width // HEAD_DIM, HEAD_DIM, tm)
    ms = jnp.mean(x3 * x3, axis=1, keepdims=True)
    xn = (x3 * lax.rsqrt(ms + EPS)).reshape(width, tm)
    return jnp.concatenate(
        [xn[:, j * LANES:(j + 1) * LANES] * gain_rows for j in range(tm // LANES)], axis=1)


def _inproj_tile(x_ref, mod_ref, wqkvt_ref, wrest_ref, qg_ref, kg_ref,
                 lng_ref, lnb_ref, ws_ref, bs_ref,
                 qt_ref, k_ref, vt_ref, ub_ref, ga_ref, gb_ref):
    shift = mod_ref[0, 3:4, :]
    scale = mod_ref[0, 4:5, :]
    row = lax.broadcasted_iota(jnp.int32, (CHUNK, CHUNK), 0)
    col = lax.broadcasted_iota(jnp.int32, (CHUNK, CHUNK), 1)
    w_spatial = [jnp.where(row >= col, ws_ref[g], 0.0).astype(BF16)
                 for g in range(N_GROUPS)]
    for c in range(TM_PROJ // TQ):
        rows = slice(c * TQ, (c + 1) * TQ)
        x = x_ref[rows, :]
        h = _modulated_rmsnorm(x, shift, scale).astype(BF16)

        pt = lax.dot_general(wqkvt_ref[...], h, (((1,), (1,)), ((), ())),
                             preferred_element_type=F32)
        qt = _chunk_rmsnorm_rows(pt[0:QK_WIDTH], qg_ref[...])
        qt_ref[0, c] = qt.astype(BF16)
        kt = _chunk_rmsnorm_rows(pt[QK_WIDTH:2 * QK_WIDTH], kg_ref[...])
        k_ref[0, rows, :] = kt.T.astype(BF16)
        vt_ref[0, c] = pt[2 * QK_WIDTH:QKV_COLS].astype(BF16)

        rest = jnp.dot(h, wrest_ref[...], preferred_element_type=F32)
        u = _gelu(rest[:, 0:GMLP_WIDTH])
        gv = _gelu(rest[:, GMLP_WIDTH:2 * GMLP_WIDTH])
        ga_ref[0, rows, :] = jax.nn.sigmoid(
            rest[:, 2 * GMLP_WIDTH:2 * GMLP_WIDTH + D_MODEL]).astype(BF16)
        gb_ref[0, rows, :] = jax.nn.sigmoid(
            rest[:, 2 * GMLP_WIDTH + D_MODEL:REST_COLS]).astype(BF16)

        mu = jnp.mean(gv, axis=-1, keepdims=True)
        cen = gv - mu
        var = jnp.mean(cen * cen, axis=-1, keepdims=True)
        vln = ((cen * lax.rsqrt(var + EPS)) * lng_ref[...] + lnb_ref[...]).astype(BF16)

        for g in range(N_GROUPS):
            cs = slice(g * GROUP_DIM, (g + 1) * GROUP_DIM)
            for n in range(TQ // CHUNK):
                rs = slice(n * CHUNK, (n + 1) * CHUNK)
                f = jnp.dot(w_spatial[g], vln[rs, cs], preferred_element_type=F32) + bs_ref[:, cs]
                ub_ref[0, c * TQ + n * CHUNK:c * TQ + (n + 1) * CHUNK, cs] = (
                    u[rs, cs] * f).astype(BF16)


def _inproj_kernel(x_ref, mod_ref, win_ref, qg_ref, kg_ref, lng_ref, lnb_ref, ws_ref, bs_ref,
                   qt_ref, k_ref, vt_ref, ub_ref, ga_ref, gb_ref, wqkvt_s, wrest_s):
    i = pl.program_id(0)

    @pl.when(i < W_STEPS)
    def _():
        rows = win_ref.shape[0]
        r = pl.multiple_of(i * rows, rows)
        chunk = win_ref[...]
        wqkvt_s[:, pl.ds(r, rows)] = chunk[:, 0:QKV_COLS].T.astype(BF16)
        wrest_s[pl.ds(r, rows), :] = chunk[:, QKV_COLS:QKV_COLS + REST_COLS].astype(BF16)

    @pl.when(i >= W_STEPS)
    def _():
        _inproj_tile(x_ref, mod_ref, wqkvt_s, wrest_s, qg_ref, kg_ref, lng_ref, lnb_ref,
                     ws_ref, bs_ref, qt_ref, k_ref, vt_ref, ub_ref, ga_ref, gb_ref)


def _inproj(x3d, mod, w_in, qg_rows, kg_rows, ln_g, ln_b, w_spatial, bs_full):
    nt = SEQ // TM_PROJ
    tile = lambda i: jnp.maximum(i - W_STEPS, 0)
    tok_spec = lambda width: pl.BlockSpec(
        (1, TM_PROJ, width), lambda i: (tile(i) // nt, tile(i) % nt, 0))
    feat_spec = lambda width: pl.BlockSpec(
        (1, TM_PROJ // TQ, width, TQ), lambda i: (tile(i) // nt, tile(i) % nt, 0, 0))
    return pl.pallas_call(
        _inproj_kernel,
        grid=(W_STEPS + BATCH * nt,),
        in_specs=[
            pl.BlockSpec((None, TM_PROJ, D_MODEL), lambda i: (tile(i) // nt, tile(i) % nt, 0)),
            pl.BlockSpec((1, N_MOD, D_MODEL), lambda i: (tile(i) // nt, 0, 0)),
            _weight_rows_spec(w_in.shape),
            _const_spec((QK_WIDTH, LANES)),
            _const_spec((QK_WIDTH, LANES)),
            _const_spec((1, GMLP_WIDTH)),
            _const_spec((1, GMLP_WIDTH)),
            _const_spec((N_GROUPS, CHUNK, CHUNK)),
            _const_spec((CHUNK, GMLP_WIDTH)),
        ],
        out_specs=[
            feat_spec(QK_WIDTH), tok_spec(QK_WIDTH), feat_spec(ATTN_WIDTH),
            tok_spec(GMLP_WIDTH), tok_spec(D_MODEL), tok_spec(D_MODEL),
        ],
        out_shape=[
            jax.ShapeDtypeStruct((BATCH, SEQ // TQ, QK_WIDTH, TQ), BF16),
            jax.ShapeDtypeStruct((BATCH, SEQ, QK_WIDTH), BF16),
            jax.ShapeDtypeStruct((BATCH, SEQ // TQ, ATTN_WIDTH, TQ), BF16),
            jax.ShapeDtypeStruct((BATCH, SEQ, GMLP_WIDTH), BF16),
            jax.ShapeDtypeStruct((BATCH, SEQ, D_MODEL), BF16),
            jax.ShapeDtypeStruct((BATCH, SEQ, D_MODEL), BF16),
        ],
        scratch_shapes=[pltpu.VMEM((QKV_COLS, D_MODEL), BF16),
                        pltpu.VMEM((D_MODEL, REST_COLS), BF16)],
        compiler_params=pltpu.CompilerParams(
            dimension_semantics=("arbitrary",), vmem_limit_bytes=VMEM_LIMIT),
        name="inproj",
    )(x3d, mod, w_in, qg_rows, kg_rows, ln_g, ln_b, w_spatial, bs_full)


def _rel_buckets(dist):
    n = np.maximum(dist, 0)
    max_exact = N_BUCKETS // 2
    nf = np.maximum(n, 1).astype(np.float32)
    large = max_exact + (np.log(nf / np.float32(max_exact))
                         / np.float32(math.log(MAX_DISTANCE / max_exact))
                         * np.float32(N_BUCKETS - max_exact)).astype(np.int32)
    large = np.minimum(large, N_BUCKETS - 1)
    return np.where(n < max_exact, n, large).astype(np.int32)


def _attn_kernel(table_ref, bound_ref, lam_ref, qt_ref, k_ref, vt_ref, bkt_ref, subg_ref,
                 o_ref, bias_ref, s_scr):
    h = pl.program_id(0)
    b = pl.program_id(1)

    @pl.when(b == 0)
    def _():
        far = table_ref[N_BUCKETS - 1, h]
        key = lax.broadcasted_iota(jnp.int32, (TQ, TQ), 0)
        qry = lax.broadcasted_iota(jnp.int32, (TQ, TQ), 1)
        for t in range(2):
            bkt = bkt_ref[t]
            tile = jnp.zeros((TQ, TQ), F32)
            for bucket in range(N_BUCKETS - 1):
                tile = jnp.where(bkt == bucket, (table_ref[bucket, h] - far) * LOG2E, tile)
            if t == 0:
                tile = jnp.where(key <= qry, tile, -jnp.inf)
            bias_ref[t] = jnp.concatenate([tile, tile], axis=1)

    @pl.when(bound_ref[0] <= MAX_SAFE_LOG2)
    def _():
        _attn_program(lam_ref, qt_ref, k_ref, vt_ref, subg_ref, o_ref, bias_ref, s_scr,
                      track_max=False)

    @pl.when(jnp.logical_not(bound_ref[0] <= MAX_SAFE_LOG2))
    def _():
        _attn_program(lam_ref, qt_ref, k_ref, vt_ref, subg_ref, o_ref, bias_ref, s_scr,
                      track_max=True)


def _attn_program(lam_ref, qt_ref, k_ref, vt_ref, subg_ref, o_ref, bias_ref, s_scr, *, track_max):
    lp = lam_ref[...]
    lam = (jnp.exp(jnp.sum(lp[0:1] * lp[1:2], axis=1, keepdims=True))
           - jnp.exp(jnp.sum(lp[2:3] * lp[3:4], axis=1, keepdims=True)) + LAM_INIT)
    subg = subg_ref[...]
    zeros = jnp.zeros((HEAD_DIM, TQ), BF16)
    ones = jnp.ones((SUM_ROWS, TQ), BF16)
    n_strips = 2 * TQ // LANES
    nq = SEQ // TQ
    blocks = [(qi, j) for qi in range(nq) for j in range(qi + 1)]
    qpads = {}

    def qpad(qi):
        if qi not in qpads:
            qt = qt_ref[0, qi]
            qpads[qi] = jnp.concatenate(
                [jnp.concatenate([qt[0:HEAD_DIM], zeros], axis=0),
                 jnp.concatenate([zeros, qt[HEAD_DIM:2 * HEAD_DIM]], axis=0)], axis=1)
        return qpads[qi]

    def scores(qi, j):
        return jnp.dot(k_ref[0, j * TQ:(j + 1) * TQ, :], qpad(qi),
                       preferred_element_type=F32)

    def probs_tracked(s, qi, j, m):
        p, alpha, m_out = [], [], []
        for c in range(n_strips):
            sc = s[:, c * LANES:(c + 1) * LANES]
            if j == qi:
                sc = sc + bias_ref[0, :, c * LANES:(c + 1) * LANES]
            elif j == qi - 1:
                sc = sc + bias_ref[1, :, c * LANES:(c + 1) * LANES]
            blk_max = jnp.max(sc, axis=0, keepdims=True)
            m_new = blk_max if j == 0 else jnp.maximum(m[c], blk_max)
            p.append(jnp.exp2(sc - m_new).astype(BF16))
            alpha.append(None if j == 0 else jnp.exp2(m[c] - m_new))
            m_out.append(m_new)
        return jnp.concatenate(p, axis=1), alpha, m_out

    half = TQ // 2
    hi_cols = (slice(half, TQ), slice(TQ + half, 2 * TQ))

    def issue_bounded(n):
        qi, j = blocks[n]
        slot = s_scr.at[n % S_SLOTS]
        if j < qi:
            slot[...] = scores(qi, j)
            return
        kb = k_ref[0, j * TQ:(j + 1) * TQ, :]
        qp = qpad(qi)
        slot[0:half, :] = jnp.dot(kb[0:half], qp, preferred_element_type=F32)
        s_hi = jnp.dot(kb[half:TQ], jnp.concatenate([qp[:, c] for c in hi_cols], axis=1),
                       preferred_element_type=F32)
        for t, c in enumerate(hi_cols):
            slot[half:TQ, c] = s_hi[:, t * half:(t + 1) * half]

    def probs_bounded(n):
        qi, j = blocks[n]
        slot = s_scr.at[n % S_SLOTS]
        if j < qi - 1:
            return jnp.exp2(slot[...])
        if j == qi - 1:
            return jnp.exp2(slot[...] + bias_ref[1])
        lo = jnp.exp2(slot[0:half, :] + bias_ref[0, 0:half, :])
        z = jnp.zeros((half, half), F32)
        hi = [jnp.exp2(slot[half:TQ, c] + bias_ref[0, half:TQ, c]) for c in hi_cols]
        return jnp.concatenate([lo, jnp.concatenate([z, hi[0], z, hi[1]], axis=1)], axis=0)

    ahead = QK_AHEAD_TRACKED if track_max else QK_AHEAD_BOUNDED

    issue = (lambda n: scores(*blocks[n])) if track_max else issue_bounded
    pending = [issue(n) for n in range(ahead)]
    m = acc = psum = None
    for n, (qi, j) in enumerate(blocks):
        s = pending.pop(0)
        if n + ahead < len(blocks):
            pending.append(issue(n + ahead))
        if track_max:
            p, alpha, m = probs_tracked(s, qi, j, m)
            vt_aug = jnp.concatenate([vt_ref[0, j], ones], axis=0)
            pv = jnp.dot(vt_aug, p, preferred_element_type=F32)[0:V_DIM + 8]
            if j == 0:
                acc = pv
            else:
                acc = jnp.concatenate(
                    [alpha[c] * acc[:, c * LANES:(c + 1) * LANES] + pv[:, c * LANES:(c + 1) * LANES]
                     for c in range(n_strips)], axis=1)
        else:
            p = probs_bounded(n)
            blk_sum = jnp.sum(p, axis=0, keepdims=True)
            pv = jnp.dot(vt_ref[0, j], p.astype(BF16), preferred_element_type=F32)
            acc = pv if j == 0 else acc + pv
            psum = blk_sum if j == 0 else psum + blk_sum
        if j == qi:
            inv = 1.0 / (acc[V_DIM:V_DIM + 1] if track_max else psum)
            ot = (acc[0:V_DIM, 0:TQ] * inv[:, 0:TQ]
                  - acc[0:V_DIM, TQ:2 * TQ] * (lam * inv[:, TQ:2 * TQ]))
            ms = jnp.mean(ot * ot, axis=0, keepdims=True)
            on = (ot * lax.rsqrt(ms + EPS)).T
            o_ref[0, qi * TQ:(qi + 1) * TQ, :] = (on * subg).astype(BF16)


def _attention(table, bound, lam, qt, k, vt, buckets, subg_row):
    return pl.pallas_call(
        _attn_kernel,
        grid=(N_HEADS, BATCH),
        in_specs=[
            pl.BlockSpec(memory_space=pltpu.SMEM),
            pl.BlockSpec(memory_space=pltpu.SMEM),
            _const_spec((4, HEAD_DIM)),
            pl.BlockSpec((1, SEQ // TQ, 2 * HEAD_DIM, TQ), lambda h, b: (b, 0, h, 0)),
            pl.BlockSpec((1, SEQ, 2 * HEAD_DIM), lambda h, b: (b, 0, h)),
            pl.BlockSpec((1, SEQ // TQ, V_DIM, TQ), lambda h, b: (b, 0, h, 0)),
            _const_spec((2, TQ, TQ)),
            _const_spec((1, V_DIM)),
        ],
        out_specs=pl.BlockSpec((1, SEQ, V_DIM), lambda h, b: (b, 0, h)),
        out_shape=jax.ShapeDtypeStruct((BATCH, SEQ, ATTN_WIDTH), BF16),
        scratch_shapes=[pltpu.VMEM((2, TQ, 2 * TQ), F32),
                        pltpu.VMEM((S_SLOTS, TQ, 2 * TQ), F32)],
        compiler_params=pltpu.CompilerParams(
            dimension_semantics=("arbitrary", "arbitrary"),
            vmem_limit_bytes=VMEM_LIMIT),
        name="diff_attn",
    )(table, bound, lam, qt, k, vt, buckets, subg_row)


def _mix_kernel(x_ref, mod_ref, o_ref, ub_ref, ga_ref, gb_ref, wa_ref, wb_ref, wo_ref,
                out_ref, wa_s, wb_s, wo_s):
    i = pl.program_id(0)

    @pl.when(i < W_STEPS)
    def _():
        _stash_rows(wa_s, wa_ref, i)
        _stash_rows(wb_s, wb_ref, i)
        _stash_rows(wo_s, wo_ref, i)

    @pl.when(i >= W_STEPS)
    def _():
        gate = mod_ref[0, 5:6, :]
        for r in range(0, TM_MIX, MIX_CHUNK):
            rows = slice(r, r + MIX_CHUNK)
            ya = jnp.dot(o_ref[rows, :], wa_s[...], preferred_element_type=F32)
            yb = jnp.dot(ub_ref[rows, :], wb_s[...], preferred_element_type=F32)
            merged = (ga_ref[rows, :].astype(F32) * ya
                      + gb_ref[rows, :].astype(F32) * yb).astype(BF16)
            z = jnp.dot(merged, wo_s[...], preferred_element_type=F32)
            out_ref[rows, :] = x_ref[rows, :] + gate * z


def _mix(x2d, mod, o2d, ub2d, ga2d, gb2d, wa, wb, wo):
    t = x2d.shape[0]
    tiles_per_batch = SEQ // TM_MIX
    tile = lambda i: jnp.maximum(i - W_STEPS, 0)
    tok = lambda: pl.BlockSpec((TM_MIX, D_MODEL), lambda i: (tile(i), 0))
    return pl.pallas_call(
        _mix_kernel,
        grid=(W_STEPS + t // TM_MIX,),
        in_specs=[
            tok(),
            pl.BlockSpec((1, N_MOD, D_MODEL), lambda i: (tile(i) // tiles_per_batch, 0, 0)),
            tok(), tok(), tok(), tok(),
            _weight_rows_spec(wa.shape),
            _weight_rows_spec(wb.shape),
            _weight_rows_spec(wo.shape),
        ],
        out_specs=tok(),
        out_shape=jax.ShapeDtypeStruct((t, D_MODEL), F32),
        scratch_shapes=[pltpu.VMEM(wa.shape, BF16), pltpu.VMEM(wb.shape, BF16),
                        pltpu.VMEM(wo.shape, BF16)],
        compiler_params=pltpu.CompilerParams(
            dimension_semantics=("arbitrary",), vmem_limit_bytes=VMEM_LIMIT),
        name="mix_out",
    )(x2d, mod, o2d, ub2d, ga2d, gb2d, wa, wb, wo)


def kernel(x, c, w_ada, b_ada, w_ffn1_gate, w_ffn1_up, w_ffn1_down, w_in, q_norm_g, k_norm_g, lam_q1, lam_k1, lam_q2, lam_k2, subln_g, rel_bias_table, gmlp_ln_g, gmlp_ln_b, w_spatial, b_spatial, w_a_proj, w_b_proj, w_o, w_ffn2_gate, w_ffn2_up, w_ffn2_down):
    b, s, d = x.shape
    t = b * s

    mod = _adaln_mod(c, w_ada[0], b_ada[0]).reshape(b, N_MOD, d)

    x1 = _ffn(x.reshape(t, d), mod, w_ffn1_gate[0], w_ffn1_up[0], w_ffn1_down[0], sub=0)

    scale = HEAD_DIM ** -0.5 * LOG2E
    qg_rows = jnp.broadcast_to(
        jnp.tile(q_norm_g[0] * scale, QK_WIDTH // HEAD_DIM)[:, None], (QK_WIDTH, LANES))
    kg_rows = jnp.broadcast_to(
        jnp.tile(k_norm_g[0], QK_WIDTH // HEAD_DIM)[:, None], (QK_WIDTH, LANES))
    bs_full = jnp.repeat(b_spatial[0].T, GROUP_DIM, axis=1)
    qt, k, vt, ub, ga, gb = _inproj(
        x1.reshape(b, s, d), mod, w_in[0], qg_rows, kg_rows,
        gmlp_ln_g[0].reshape(1, -1), gmlp_ln_b[0].reshape(1, -1), w_spatial[0], bs_full)

    lam = jnp.stack([lam_q1[0], lam_k1[0], lam_q2[0], lam_k2[0]])
    kk = np.arange(TQ)[:, None]
    qq = np.arange(TQ)[None, :]
    buckets = jnp.asarray(np.stack([_rel_buckets(qq - kk), _rel_buckets(TQ + qq - kk)]))
    subg_row = (subln_g[0] * (1.0 - LAM_INIT)).reshape(1, V_DIM)
    far_bias = rel_bias_table[N_BUCKETS - 1:N_BUCKETS]
    score_bound = (HEAD_DIM * jnp.max(jnp.abs(q_norm_g[0] * scale)) * jnp.max(jnp.abs(k_norm_g[0]))
                   * BF16_SLACK + LOG2E * jnp.max(jnp.abs(rel_bias_table - far_bias))).reshape(1)
    o = _attention(rel_bias_table, score_bound, lam, qt, k, vt, buckets, subg_row)

    x2 = _mix(x1, mod, o.reshape(t, ATTN_WIDTH), ub.reshape(t, GMLP_WIDTH),
              ga.reshape(t, d), gb.reshape(t, d), w_a_proj[0], w_b_proj[0], w_o[0])

    x3 = _ffn(x2, mod, w_ffn2_gate[0], w_ffn2_up[0], w_ffn2_down[0], sub=2)
    return x3.reshape(b, s, d)
```

```python
import functools
import math

import numpy as np
import jax
import jax.numpy as jnp
from jax import lax
from jax.experimental import pallas as pl
from jax.experimental.pallas import tpu as pltpu

F32 = jnp.float32
BF16 = jnp.bfloat16

D_MODEL = 1024
BATCH = 8
SEQ = 2048
N_HEADS = 8
HEAD_DIM = 64
V_DIM = 2 * HEAD_DIM
QK_WIDTH = N_HEADS * 2 * HEAD_DIM
ATTN_WIDTH = N_HEADS * V_DIM
N_GROUPS = 8
CHUNK = 128
GMLP_WIDTH = 1024
GROUP_DIM = GMLP_WIDTH // N_GROUPS
N_BUCKETS = 32
MAX_DISTANCE = 128
D_FF = 2816
N_MOD = 9
EPS = 1e-6
LAM_INIT = 0.8 - 0.6 * math.exp(-0.3 * 0)
LOG2E = math.log2(math.e)
QKV_COLS = 2 * QK_WIDTH + ATTN_WIDTH
REST_COLS = 2 * GMLP_WIDTH + 2 * D_MODEL

LANES = 128
VMEM_LIMIT = 56 * 1024 * 1024

FFN_CHUNK = 256
TM_FFN = 1024
TM_PROJ = 512
MIX_CHUNK = 256
TM_MIX = 1024
TQ = 256
SUM_ROWS = 16
W_STEPS = 8
MAX_SAFE_LOG2 = 64.0
BF16_SLACK = 1.02
QK_AHEAD_TRACKED = 2
QK_AHEAD_BOUNDED = 3
S_SLOTS = QK_AHEAD_BOUNDED + 1


def _const_spec(shape):
    nd = len(shape)
    return pl.BlockSpec(shape, lambda *_: (0,) * nd, pipeline_mode=pl.Buffered(1))


def _weight_rows_spec(shape):
    rows, cols = shape
    return pl.BlockSpec((rows // W_STEPS, cols),
                        lambda i: (jnp.minimum(i, W_STEPS - 1), 0))


def _stash_rows(dst_ref, chunk_ref, i):
    rows = chunk_ref.shape[0]
    dst_ref[pl.ds(pl.multiple_of(i * rows, rows), rows), :] = chunk_ref[...].astype(BF16)


def _silu(x):
    return x * jax.nn.sigmoid(x)


def _gelu(x):
    return 0.5 * x * (1.0 + lax.erf(x * (1.0 / math.sqrt(2.0))))


def _modulated_rmsnorm(x, shift, scale):
    ms = jnp.mean(x * x, axis=-1, keepdims=True)
    return (x * lax.rsqrt(ms + EPS)) * (1.0 + scale) + shift


def _as_column(row):
    n = row.shape[1]
    r = lax.broadcasted_iota(jnp.int32, (n, n), 0)
    c = lax.broadcasted_iota(jnp.int32, (n, n), 1)
    return jnp.sum(jnp.where(r == c, row, 0.0), axis=1, keepdims=True)


def _mod_kernel(c_ref, w_ref, b_ref, qg_ref, kg_ref, bsp_ref, table_ref,
                o_ref, qgr_ref, kgr_ref, bsf_ref, bound_ref):
    a = _silu(c_ref[...]).astype(BF16)
    o_ref[0] = jnp.dot(a, w_ref[...].astype(BF16), preferred_element_type=F32) + b_ref[...]

    @pl.when(pl.program_id(0) == 0)
    def _():
        qg = qg_ref[...] * (HEAD_DIM ** -0.5 * LOG2E)
        kg = kg_ref[...]
        reps = QK_WIDTH // HEAD_DIM
        qgr_ref[...] = jnp.concatenate(
            [jnp.broadcast_to(_as_column(qg), (HEAD_DIM, LANES))] * reps, axis=0)
        kgr_ref[...] = jnp.concatenate(
            [jnp.broadcast_to(_as_column(kg), (HEAD_DIM, LANES))] * reps, axis=0)
        bsf_ref[...] = jnp.concatenate(
            [jnp.broadcast_to(_as_column(bsp_ref[g:g + 1, :]), (CHUNK, GROUP_DIM))
             for g in range(N_GROUPS)], axis=1)
        table = table_ref[...]
        rel = jnp.abs(table - table[N_BUCKETS - 1:N_BUCKETS, :])
        bound_ref[...] = (
            HEAD_DIM * BF16_SLACK * jnp.max(jnp.abs(qg), axis=(0, 1), keepdims=True)
            * jnp.max(jnp.abs(kg), axis=(0, 1), keepdims=True)
            + LOG2E * jnp.max(rel, axis=(0, 1), keepdims=True))


def _adaln_mod(c, w_ada, b_ada, q_norm_g, k_norm_g, b_spatial, rel_bias_table):
    n = w_ada.shape[1]
    whole = lambda shape: pl.BlockSpec(shape, lambda j: (0,) * len(shape))
    return pl.pallas_call(
        _mod_kernel,
        grid=(n // D_MODEL,),
        in_specs=[
            whole((BATCH, D_MODEL)),
            pl.BlockSpec((D_MODEL, D_MODEL), lambda j: (0, j)),
            pl.BlockSpec((1, D_MODEL), lambda j: (0, j)),
            whole((1, HEAD_DIM)), whole((1, HEAD_DIM)),
            whole((N_GROUPS, CHUNK)), whole((N_BUCKETS, N_HEADS)),
        ],
        out_specs=[
            pl.BlockSpec((1, BATCH, D_MODEL), lambda j: (j, 0, 0)),
            whole((QK_WIDTH, LANES)), whole((QK_WIDTH, LANES)),
            whole((CHUNK, GMLP_WIDTH)), whole((1, 1)),
        ],
        out_shape=[
            jax.ShapeDtypeStruct((n // D_MODEL, BATCH, D_MODEL), F32),
            jax.ShapeDtypeStruct((QK_WIDTH, LANES), F32),
            jax.ShapeDtypeStruct((QK_WIDTH, LANES), F32),
            jax.ShapeDtypeStruct((CHUNK, GMLP_WIDTH), F32),
            jax.ShapeDtypeStruct((1, 1), F32),
        ],
        compiler_params=pltpu.CompilerParams(
            dimension_semantics=("arbitrary",), vmem_limit_bytes=VMEM_LIMIT),
        name="adaln_mod",
    )(c, w_ada, b_ada.reshape(1, n), q_norm_g, k_norm_g, b_spatial, rel_bias_table)


def _mod_row(mod_ref, j, b):
    return mod_ref[j, pl.ds(b, 1), :]


def _ffn_kernel(x_ref, mod_ref, wg_ref, wu_ref, wd_ref, o_ref, wg_s, wu_s, wd_s, *, sub):
    i = pl.program_id(0)

    @pl.when(i < W_STEPS)
    def _():
        _stash_rows(wg_s, wg_ref, i)
        _stash_rows(wu_s, wu_ref, i)
        _stash_rows(wd_s, wd_ref, i)

    @pl.when(i >= W_STEPS)
    def _():
        b = lax.div(i - W_STEPS, SEQ // TM_FFN)
        shift = _mod_row(mod_ref, 3 * sub + 0, b)
        scale = _mod_row(mod_ref, 3 * sub + 1, b)
        gate = _mod_row(mod_ref, 3 * sub + 2, b)
        for r in range(0, TM_FFN, FFN_CHUNK):
            x = x_ref[r:r + FFN_CHUNK, :]
            h = _modulated_rmsnorm(x, shift, scale).astype(BF16)
            g = jnp.dot(h, wg_s[...], preferred_element_type=F32)
            u = jnp.dot(h, wu_s[...], preferred_element_type=F32)
            a = (_silu(g) * u).astype(BF16)
            y = jnp.dot(a, wd_s[...], preferred_element_type=F32)
            o_ref[r:r + FFN_CHUNK, :] = x + (0.5 * gate) * y


def _ffn(x2d, mod, wg, wu, wd, *, sub):
    t = x2d.shape[0]
    tile = lambda i: jnp.maximum(i - W_STEPS, 0)
    return pl.pallas_call(
        functools.partial(_ffn_kernel, sub=sub),
        grid=(W_STEPS + t // TM_FFN,),
        in_specs=[
            pl.BlockSpec((TM_FFN, D_MODEL), lambda i: (tile(i), 0)),
            _const_spec((N_MOD, BATCH, D_MODEL)),
            _weight_rows_spec(wg.shape),
            _weight_rows_spec(wu.shape),
            _weight_rows_spec(wd.shape),
        ],
        out_specs=pl.BlockSpec((TM_FFN, D_MODEL), lambda i: (tile(i), 0)),
        out_shape=jax.ShapeDtypeStruct((t, D_MODEL), F32),
        scratch_shapes=[pltpu.VMEM(wg.shape, BF16), pltpu.VMEM(wu.shape, BF16),
                        pltpu.VMEM(wd.shape, BF16)],
        compiler_params=pltpu.CompilerParams(
            dimension_semantics=("arbitrary",), vmem_limit_bytes=VMEM_LIMIT),
        name=f"ffn{sub}",
    )(x2d, mod, wg, wu, wd)


def _chunk_rmsnorm_rows(xt, gain_rows):
    width, tm = xt.shape
    x3 = xt.reshape(width // HEAD_DIM, HEAD_DIM, tm)
    ms = jnp.mean(x3 * x3, axis=1, keepdims=True)
    xn = (x3 * lax.rsqrt(ms + EPS)).reshape(width, tm)
    return jnp.concatenate(
        [xn[:, j * LANES:(j + 1) * LANES] * gain_rows for j in range(tm // LANES)], axis=1)


def _inproj_tile(x_ref, shift, scale, wqkvt_ref, wrest_ref, qg_ref, kg_ref,
                 lng_ref, lnb_ref, ws_ref, bs_ref,
                 qt_ref, k_ref, vt_ref, ub_ref, ga_ref, gb_ref):
    row = lax.broadcasted_iota(jnp.int32, (CHUNK, CHUNK), 0)
    col = lax.broadcasted_iota(jnp.int32, (CHUNK, CHUNK), 1)
    w_spatial = [jnp.where(row >= col, ws_ref[g], 0.0).astype(BF16)
                 for g in range(N_GROUPS)]
    for c in range(TM_PROJ // TQ):
        rows = slice(c * TQ, (c + 1) * TQ)
        x = x_ref[rows, :]
        h = _modulated_rmsnorm(x, shift, scale).astype(BF16)

        pt = lax.dot_general(wqkvt_ref[...], h, (((1,), (1,)), ((), ())),
                             preferred_element_type=F32)
        qt = _chunk_rmsnorm_rows(pt[0:QK_WIDTH], qg_ref[...])
        qt_ref[0, c] = qt.astype(BF16)
        kt = _chunk_rmsnorm_rows(pt[QK_WIDTH:2 * QK_WIDTH], kg_ref[...])
        k_ref[0, rows, :] = kt.T.astype(BF16)
        vt_ref[0, c] = pt[2 * QK_WIDTH:QKV_COLS].astype(BF16)

        rest = jnp.dot(h, wrest_ref[...], preferred_element_type=F32)
        u = _gelu(rest[:, 0:GMLP_WIDTH])
        gv = _gelu(rest[:, GMLP_WIDTH:2 * GMLP_WIDTH])
        ga_ref[0, rows, :] = jax.nn.sigmoid(
            rest[:, 2 * GMLP_WIDTH:2 * GMLP_WIDTH + D_MODEL]).astype(BF16)
        gb_ref[0, rows, :] = jax.nn.sigmoid(
            rest[:, 2 * GMLP_WIDTH + D_MODEL:REST_COLS]).astype(BF16)

        mu = jnp.mean(gv, axis=-1, keepdims=True)
        cen = gv - mu
        var = jnp.mean(cen * cen, axis=-1, keepdims=True)
        vln = ((cen * lax.rsqrt(var + EPS)) * lng_ref[...] + lnb_ref[...]).astype(BF16)

        for g in range(N_GROUPS):
            cs = slice(g * GROUP_DIM, (g + 1) * GROUP_DIM)
            for n in range(TQ // CHUNK):
                rs = slice(n * CHUNK, (n + 1) * CHUNK)
                f = jnp.dot(w_spatial[g], vln[rs, cs], preferred_element_type=F32) + bs_ref[:, cs]
                ub_ref[0, c * TQ + n * CHUNK:c * TQ + (n + 1) * CHUNK, cs] = (
                    u[rs, cs] * f).astype(BF16)


def _inproj_kernel(x_ref, mod_ref, win_ref, qg_ref, kg_ref, lng_ref, lnb_ref, ws_ref, bs_ref,
                   qt_ref, k_ref, vt_ref, ub_ref, ga_ref, gb_ref, wqkvt_s, wrest_s):
    i = pl.program_id(0)

    @pl.when(i < W_STEPS)
    def _():
        rows = win_ref.shape[0]
        r = pl.multiple_of(i * rows, rows)
        chunk = win_ref[...]
        wqkvt_s[:, pl.ds(r, rows)] = chunk[:, 0:QKV_COLS].T.astype(BF16)
        wrest_s[pl.ds(r, rows), :] = chunk[:, QKV_COLS:QKV_COLS + REST_COLS].astype(BF16)

    @pl.when(i >= W_STEPS)
    def _():
        b = lax.div(i - W_STEPS, SEQ // TM_PROJ)
        _inproj_tile(x_ref, _mod_row(mod_ref, 3, b), _mod_row(mod_ref, 4, b), wqkvt_s, wrest_s,
                     qg_ref, kg_ref, lng_ref, lnb_ref, ws_ref, bs_ref,
                     qt_ref, k_ref, vt_ref, ub_ref, ga_ref, gb_ref)


def _inproj(x3d, mod, w_in, qg_rows, kg_rows, ln_g, ln_b, w_spatial, bs_full):
    nt = SEQ // TM_PROJ
    tile = lambda i: jnp.maximum(i - W_STEPS, 0)
    tok_spec = lambda width: pl.BlockSpec(
        (1, TM_PROJ, width), lambda i: (tile(i) // nt, tile(i) % nt, 0))
    feat_spec = lambda width: pl.BlockSpec(
        (1, TM_PROJ // TQ, width, TQ), lambda i: (tile(i) // nt, tile(i) % nt, 0, 0))
    return pl.pallas_call(
        _inproj_kernel,
        grid=(W_STEPS + BATCH * nt,),
        in_specs=[
            pl.BlockSpec((None, TM_PROJ, D_MODEL), lambda i: (tile(i) // nt, tile(i) % nt, 0)),
            _const_spec((N_MOD, BATCH, D_MODEL)),
            _weight_rows_spec(w_in.shape),
            _const_spec((QK_WIDTH, LANES)),
            _const_spec((QK_WIDTH, LANES)),
            _const_spec((1, GMLP_WIDTH)),
            _const_spec((1, GMLP_WIDTH)),
            _const_spec((N_GROUPS, CHUNK, CHUNK)),
            _const_spec((CHUNK, GMLP_WIDTH)),
        ],
        out_specs=[
            feat_spec(QK_WIDTH), tok_spec(QK_WIDTH), feat_spec(ATTN_WIDTH),
            tok_spec(GMLP_WIDTH), tok_spec(D_MODEL), tok_spec(D_MODEL),
        ],
        out_shape=[
            jax.ShapeDtypeStruct((BATCH, SEQ // TQ, QK_WIDTH, TQ), BF16),
            jax.ShapeDtypeStruct((BATCH, SEQ, QK_WIDTH), BF16),
            jax.ShapeDtypeStruct((BATCH, SEQ // TQ, ATTN_WIDTH, TQ), BF16),
            jax.ShapeDtypeStruct((BATCH, SEQ, GMLP_WIDTH), BF16),
            jax.ShapeDtypeStruct((BATCH, SEQ, D_MODEL), BF16),
            jax.ShapeDtypeStruct((BATCH, SEQ, D_MODEL), BF16),
        ],
        scratch_shapes=[pltpu.VMEM((QKV_COLS, D_MODEL), BF16),
                        pltpu.VMEM((D_MODEL, REST_COLS), BF16)],
        compiler_params=pltpu.CompilerParams(
            dimension_semantics=("arbitrary",), vmem_limit_bytes=VMEM_LIMIT),
        name="inproj",
    )(x3d, mod, w_in, qg_rows, kg_rows, ln_g, ln_b, w_spatial, bs_full)


def _rel_buckets(dist):
    n = np.maximum(dist, 0)
    max_exact = N_BUCKETS // 2
    nf = np.maximum(n, 1).astype(np.float32)
    large = max_exact + (np.log(nf / np.float32(max_exact))
                         / np.float32(math.log(MAX_DISTANCE / max_exact))
                         * np.float32(N_BUCKETS - max_exact)).astype(np.int32)
    large = np.minimum(large, N_BUCKETS - 1)
    return np.where(n < max_exact, n, large).astype(np.int32)


def _attn_kernel(table_ref, bound_ref, lq1_ref, lk1_ref, lq2_ref, lk2_ref,
                 qt_ref, k_ref, vt_ref, bkt_ref, subg_ref, o_ref, bias_ref, s_scr):
    h = pl.program_id(0)
    b = pl.program_id(1)

    @pl.when(b == 0)
    def _():
        far = table_ref[N_BUCKETS - 1, h]
        key = lax.broadcasted_iota(jnp.int32, (TQ, TQ), 0)
        qry = lax.broadcasted_iota(jnp.int32, (TQ, TQ), 1)
        for t in range(2):
            bkt = bkt_ref[t]
            tile = jnp.zeros((TQ, TQ), F32)
            for bucket in range(N_BUCKETS - 1):
                tile = jnp.where(bkt == bucket, (table_ref[bucket, h] - far) * LOG2E, tile)
            if t == 0:
                tile = jnp.where(key <= qry, tile, -jnp.inf)
            bias_ref[t] = jnp.concatenate([tile, tile], axis=1)

    lam = (jnp.exp(jnp.sum(lq1_ref[...] * lk1_ref[...], axis=1, keepdims=True))
           - jnp.exp(jnp.sum(lq2_ref[...] * lk2_ref[...], axis=1, keepdims=True)) + LAM_INIT)
    subg = subg_ref[...] * (1.0 - LAM_INIT)

    @pl.when(bound_ref[0, 0] <= MAX_SAFE_LOG2)
    def _():
        _attn_program(lam, subg, qt_ref, k_ref, vt_ref, o_ref, bias_ref, s_scr, track_max=False)

    @pl.when(jnp.logical_not(bound_ref[0, 0] <= MAX_SAFE_LOG2))
    def _():
        _attn_program(lam, subg, qt_ref, k_ref, vt_ref, o_ref, bias_ref, s_scr, track_max=True)


def _attn_program(lam, subg, qt_ref, k_ref, vt_ref, o_ref, bias_ref, s_scr, *, track_max):
    zeros = jnp.zeros((HEAD_DIM, TQ), BF16)
    ones = jnp.ones((SUM_ROWS, TQ), BF16)
    n_strips = 2 * TQ // LANES
    nq = SEQ // TQ
    blocks = [(qi, j) for qi in range(nq) for j in range(qi + 1)]
    qpads = {}

    def qpad(qi):
        if qi not in qpads:
            qt = qt_ref[0, qi]
            qpads[qi] = jnp.concatenate(
                [jnp.concatenate([qt[0:HEAD_DIM], zeros], axis=0),
                 jnp.concatenate([zeros, qt[HEAD_DIM:2 * HEAD_DIM]], axis=0)], axis=1)
        return qpads[qi]

    def scores(qi, j):
        return jnp.dot(k_ref[0, j * TQ:(j + 1) * TQ, :], qpad(qi),
                       preferred_element_type=F32)

    def probs_tracked(s, qi, j, m):
        p, alpha, m_out = [], [], []
        for c in range(n_strips):
            sc = s[:, c * LANES:(c + 1) * LANES]
            if j == qi:
                sc = sc + bias_ref[0, :, c * LANES:(c + 1) * LANES]
            elif j == qi - 1:
                sc = sc + bias_ref[1, :, c * LANES:(c + 1) * LANES]
            blk_max = jnp.max(sc, axis=0, keepdims=True)
            m_new = blk_max if j == 0 else jnp.maximum(m[c], blk_max)
            p.append(jnp.exp2(sc - m_new).astype(BF16))
            alpha.append(None if j == 0 else jnp.exp2(m[c] - m_new))
            m_out.append(m_new)
        return jnp.concatenate(p, axis=1), alpha, m_out

    half = TQ // 2
    hi_cols = (slice(half, TQ), slice(TQ + half, 2 * TQ))

    def issue_bounded(n):
        qi, j = blocks[n]
        slot = s_scr.at[n % S_SLOTS]
        if j < qi:
            slot[...] = scores(qi, j)
            return
        kb = k_ref[0, j * TQ:(j + 1) * TQ, :]
        qp = qpad(qi)
        slot[0:half, :] = jnp.dot(kb[0:half], qp, preferred_element_type=F32)
        s_hi = jnp.dot(kb[half:TQ], jnp.concatenate([qp[:, c] for c in hi_cols], axis=1),
                       preferred_element_type=F32)
        for t, c in enumerate(hi_cols):
            slot[half:TQ, c] = s_hi[:, t * half:(t + 1) * half]

    def probs_bounded(n):
        qi, j = blocks[n]
        slot = s_scr.at[n % S_SLOTS]
        if j < qi - 1:
            return jnp.exp2(slot[...])
        if j == qi - 1:
            return jnp.exp2(slot[...] + bias_ref[1])
        lo = jnp.exp2(slot[0:half, :] + bias_ref[0, 0:half, :])
        z = jnp.zeros((half, half), F32)
        hi = [jnp.exp2(slot[half:TQ, c] + bias_ref[0, half:TQ, c]) for c in hi_cols]
        return jnp.concatenate([lo, jnp.concatenate([z, hi[0], z, hi[1]], axis=1)], axis=0)

    ahead = QK_AHEAD_TRACKED if track_max else QK_AHEAD_BOUNDED

    issue = (lambda n: scores(*blocks[n])) if track_max else issue_bounded
    pending = [issue(n) for n in range(ahead)]
    m = acc = psum = None
    for n, (qi, j) in enumerate(blocks):
        s = pending.pop(0)
        if n + ahead < len(blocks):
            pending.append(issue(n + ahead))
        if track_max:
            p, alpha, m = probs_tracked(s, qi, j, m)
            vt_aug = jnp.concatenate([vt_ref[0, j], ones], axis=0)
            pv = jnp.dot(vt_aug, p, preferred_element_type=F32)[0:V_DIM + 8]
            if j == 0:
                acc = pv
            else:
                acc = jnp.concatenate(
                    [alpha[c] * acc[:, c * LANES:(c + 1) * LANES] + pv[:, c * LANES:(c + 1) * LANES]
                     for c in range(n_strips)], axis=1)
        else:
            p = probs_bounded(n)
            blk_sum = jnp.sum(p, axis=0, keepdims=True)
            pv = jnp.dot(vt_ref[0, j], p.astype(BF16), preferred_element_type=F32)
            acc = pv if j == 0 else acc + pv
            psum = blk_sum if j == 0 else psum + blk_sum
        if j == qi:
            inv = 1.0 / (acc[V_DIM:V_DIM + 1] if track_max else psum)
            ot = (acc[0:V_DIM, 0:TQ] * inv[:, 0:TQ]
                  - acc[0:V_DIM, TQ:2 * TQ] * (lam * inv[:, TQ:2 * TQ]))
            ms = jnp.mean(ot * ot, axis=0, keepdims=True)
            on = (ot * lax.rsqrt(ms + EPS)).T
            o_ref[0, qi * TQ:(qi + 1) * TQ, :] = (on * subg).astype(BF16)


def _attention(table, bound, lam_q1, lam_k1, lam_q2, lam_k2, qt, k, vt, buckets, subln_g):
    return pl.pallas_call(
        _attn_kernel,
        grid=(N_HEADS, BATCH),
        in_specs=[
            pl.BlockSpec(memory_space=pltpu.SMEM),
            pl.BlockSpec(memory_space=pltpu.SMEM),
            _const_spec((1, HEAD_DIM)), _const_spec((1, HEAD_DIM)),
            _const_spec((1, HEAD_DIM)), _const_spec((1, HEAD_DIM)),
            pl.BlockSpec((1, SEQ // TQ, 2 * HEAD_DIM, TQ), lambda h, b: (b, 0, h, 0)),
            pl.BlockSpec((1, SEQ, 2 * HEAD_DIM), lambda h, b: (b, 0, h)),
            pl.BlockSpec((1, SEQ // TQ, V_DIM, TQ), lambda h, b: (b, 0, h, 0)),
            _const_spec((2, TQ, TQ)),
            _const_spec((1, V_DIM)),
        ],
        out_specs=pl.BlockSpec((1, SEQ, V_DIM), lambda h, b: (b, 0, h)),
        out_shape=jax.ShapeDtypeStruct((BATCH, SEQ, ATTN_WIDTH), BF16),
        scratch_shapes=[pltpu.VMEM((2, TQ, 2 * TQ), F32),
                        pltpu.VMEM((S_SLOTS, TQ, 2 * TQ), F32)],
        compiler_params=pltpu.CompilerParams(
            dimension_semantics=("arbitrary", "arbitrary"),
            vmem_limit_bytes=VMEM_LIMIT),
        name="diff_attn",
    )(table, bound, lam_q1, lam_k1, lam_q2, lam_k2, qt, k, vt, buckets, subln_g)


def _mix_kernel(x_ref, mod_ref, o_ref, ub_ref, ga_ref, gb_ref, wa_ref, wb_ref, wo_ref,
                out_ref, wa_s, wb_s, wo_s):
    i = pl.program_id(0)

    @pl.when(i < W_STEPS)
    def _():
        _stash_rows(wa_s, wa_ref, i)
        _stash_rows(wb_s, wb_ref, i)
        _stash_rows(wo_s, wo_ref, i)

    @pl.when(i >= W_STEPS)
    def _():
        gate = _mod_row(mod_ref, 5, lax.div(i - W_STEPS, SEQ // TM_MIX))
        for r in range(0, TM_MIX, MIX_CHUNK):
            rows = slice(r, r + MIX_CHUNK)
            ya = jnp.dot(o_ref[rows, :], wa_s[...], preferred_element_type=F32)
            yb = jnp.dot(ub_ref[rows, :], wb_s[...], preferred_element_type=F32)
            merged = (ga_ref[rows, :].astype(F32) * ya
                      + gb_ref[rows, :].astype(F32) * yb).astype(BF16)
            z = jnp.dot(merged, wo_s[...], preferred_element_type=F32)
            out_ref[rows, :] = x_ref[rows, :] + gate * z


def _mix(x2d, mod, o2d, ub2d, ga2d, gb2d, wa, wb, wo):
    t = x2d.shape[0]
    tile = lambda i: jnp.maximum(i - W_STEPS, 0)
    tok = lambda: pl.BlockSpec((TM_MIX, D_MODEL), lambda i: (tile(i), 0))
    return pl.pallas_call(
        _mix_kernel,
        grid=(W_STEPS + t // TM_MIX,),
        in_specs=[
            tok(),
            _const_spec((N_MOD, BATCH, D_MODEL)),
            tok(), tok(), tok(), tok(),
            _weight_rows_spec(wa.shape),
            _weight_rows_spec(wb.shape),
            _weight_rows_spec(wo.shape),
        ],
        out_specs=tok(),
        out_shape=jax.ShapeDtypeStruct((t, D_MODEL), F32),
        scratch_shapes=[pltpu.VMEM(wa.shape, BF16), pltpu.VMEM(wb.shape, BF16),
                        pltpu.VMEM(wo.shape, BF16)],
        compiler_params=pltpu.CompilerParams(
            dimension_semantics=("arbitrary",), vmem_limit_bytes=VMEM_LIMIT),
        name="mix_out",
    )(x2d, mod, o2d, ub2d, ga2d, gb2d, wa, wb, wo)


def kernel(x, c, w_ada, b_ada, w_ffn1_gate, w_ffn1_up, w_ffn1_down, w_in, q_norm_g, k_norm_g, lam_q1, lam_k1, lam_q2, lam_k2, subln_g, rel_bias_table, gmlp_ln_g, gmlp_ln_b, w_spatial, b_spatial, w_a_proj, w_b_proj, w_o, w_ffn2_gate, w_ffn2_up, w_ffn2_down):
    b, s, d = x.shape
    t = b * s

    mod, qg_rows, kg_rows, bs_full, score_bound = _adaln_mod(
        c, w_ada[0], b_ada[0], q_norm_g, k_norm_g, b_spatial[0], rel_bias_table)

    x1 = _ffn(x.reshape(t, d), mod, w_ffn1_gate[0], w_ffn1_up[0], w_ffn1_down[0], sub=0)

    qt, k, vt, ub, ga, gb = _inproj(
        x1.reshape(b, s, d), mod, w_in[0], qg_rows, kg_rows, gmlp_ln_g, gmlp_ln_b,
        w_spatial[0], bs_full)

    kk = np.arange(TQ)[:, None]
    qq = np.arange(TQ)[None, :]
    buckets = jnp.asarray(np.stack([_rel_buckets(qq - kk), _rel_buckets(TQ + qq - kk)]))
    o = _attention(rel_bias_table, score_bound, lam_q1, lam_k1, lam_q2, lam_k2,
                   qt, k, vt, buckets, subln_g)

    x2 = _mix(x1, mod, o.reshape(t, ATTN_WIDTH), ub.reshape(t, GMLP_WIDTH),
              ga.reshape(t, d), gb.reshape(t, d), w_a_proj[0], w_b_proj[0], w_o[0])

    x3 = _ffn(x2, mod, w_ffn2_gate[0], w_ffn2_up[0], w_ffn2_down[0], sub=2)
    return x3.reshape(b, s, d)
```

```python
import functools
import math

import numpy as np
import jax
import jax.numpy as jnp
from jax import lax
from jax.experimental import pallas as pl
from jax.experimental.pallas import tpu as pltpu

F32 = jnp.float32
BF16 = jnp.bfloat16

D_MODEL = 1024
BATCH = 8
SEQ = 2048
N_HEADS = 8
HEAD_DIM = 64
V_DIM = 2 * HEAD_DIM
QK_WIDTH = N_HEADS * 2 * HEAD_DIM
ATTN_WIDTH = N_HEADS * V_DIM
N_GROUPS = 8
CHUNK = 128
GMLP_WIDTH = 1024
GROUP_DIM = GMLP_WIDTH // N_GROUPS
N_BUCKETS = 32
MAX_DISTANCE = 128
D_FF = 2816
N_MOD = 9
EPS = 1e-6
LAM_INIT = 0.8 - 0.6 * math.exp(-0.3 * 0)
LOG2E = math.log2(math.e)
QKV_COLS = 2 * QK_WIDTH + ATTN_WIDTH
REST_COLS = 2 * GMLP_WIDTH + 2 * D_MODEL

LANES = 128
VMEM_LIMIT = 56 * 1024 * 1024

FFN_CHUNK = 256
TM_FFN = 1024
TM_PROJ = 512
MIX_CHUNK = 256
TM_MIX = 512
TQ = 256
SUM_ROWS = 16
W_STEPS = 8
MAX_SAFE_LOG2 = 64.0
BF16_SLACK = 1.02
QK_AHEAD_TRACKED = 2
QK_AHEAD_BOUNDED = 3
S_SLOTS = QK_AHEAD_BOUNDED + 1


def _const_spec(shape):
    nd = len(shape)
    return pl.BlockSpec(shape, lambda *_: (0,) * nd, pipeline_mode=pl.Buffered(1))


def _weight_rows_spec(shape):
    rows, cols = shape
    return pl.BlockSpec((rows // W_STEPS, cols),
                        lambda i: (jnp.minimum(i, W_STEPS - 1), 0))


def _stash_rows(dst_ref, chunk_ref, i):
    rows = chunk_ref.shape[0]
    dst_ref[pl.ds(pl.multiple_of(i * rows, rows), rows), :] = chunk_ref[...].astype(BF16)


def _silu(x):
    return x * jax.nn.sigmoid(x)


def _gelu(x):
    return 0.5 * x * (1.0 + lax.erf(x * (1.0 / math.sqrt(2.0))))


def _modulated_rmsnorm(x, shift, scale):
    ms = jnp.mean(x * x, axis=-1, keepdims=True)
    return (x * lax.rsqrt(ms + EPS)) * (1.0 + scale) + shift


def _as_column(row):
    n = row.shape[1]
    r = lax.broadcasted_iota(jnp.int32, (n, n), 0)
    c = lax.broadcasted_iota(jnp.int32, (n, n), 1)
    return jnp.sum(jnp.where(r == c, row, 0.0), axis=1, keepdims=True)


def _mod_kernel(c_ref, w_ref, b_ref, qg_ref, kg_ref, bsp_ref, table_ref,
                o_ref, qgr_ref, kgr_ref, bsf_ref, bound_ref):
    a = _silu(c_ref[...]).astype(BF16)
    o_ref[0] = jnp.dot(a, w_ref[...].astype(BF16), preferred_element_type=F32) + b_ref[...]

    @pl.when(pl.program_id(0) == 0)
    def _():
        qg = qg_ref[...] * (HEAD_DIM ** -0.5 * LOG2E)
        kg = kg_ref[...]
        reps = QK_WIDTH // HEAD_DIM
        qgr_ref[...] = jnp.concatenate(
            [jnp.broadcast_to(_as_column(qg), (HEAD_DIM, LANES))] * reps, axis=0)
        kgr_ref[...] = jnp.concatenate(
            [jnp.broadcast_to(_as_column(kg), (HEAD_DIM, LANES))] * reps, axis=0)
        bsf_ref[...] = jnp.concatenate(
            [jnp.broadcast_to(_as_column(bsp_ref[g:g + 1, :]), (CHUNK, GROUP_DIM))
             for g in range(N_GROUPS)], axis=1)
        table = table_ref[...]
        rel = jnp.abs(table - table[N_BUCKETS - 1:N_BUCKETS, :])
        bound_ref[...] = (
            HEAD_DIM * BF16_SLACK * jnp.max(jnp.abs(qg), axis=(0, 1), keepdims=True)
            * jnp.max(jnp.abs(kg), axis=(0, 1), keepdims=True)
            + LOG2E * jnp.max(rel, axis=(0, 1), keepdims=True))


def _adaln_mod(c, w_ada, b_ada, q_norm_g, k_norm_g, b_spatial, rel_bias_table):
    n = w_ada.shape[1]
    whole = lambda shape: pl.BlockSpec(shape, lambda j: (0,) * len(shape))
    return pl.pallas_call(
        _mod_kernel,
        grid=(n // D_MODEL,),
        in_specs=[
            whole((BATCH, D_MODEL)),
            pl.BlockSpec((D_MODEL, D_MODEL), lambda j: (0, j)),
            pl.BlockSpec((1, D_MODEL), lambda j: (0, j)),
            whole((1, HEAD_DIM)), whole((1, HEAD_DIM)),
            whole((N_GROUPS, CHUNK)), whole((N_BUCKETS, N_HEADS)),
        ],
        out_specs=[
            pl.BlockSpec((1, BATCH, D_MODEL), lambda j: (j, 0, 0)),
            whole((QK_WIDTH, LANES)), whole((QK_WIDTH, LANES)),
            whole((CHUNK, GMLP_WIDTH)), whole((1, 1)),
        ],
        out_shape=[
            jax.ShapeDtypeStruct((n // D_MODEL, BATCH, D_MODEL), F32),
            jax.ShapeDtypeStruct((QK_WIDTH, LANES), F32),
            jax.ShapeDtypeStruct((QK_WIDTH, LANES), F32),
            jax.ShapeDtypeStruct((CHUNK, GMLP_WIDTH), F32),
            jax.ShapeDtypeStruct((1, 1), F32),
        ],
        compiler_params=pltpu.CompilerParams(
            dimension_semantics=("arbitrary",), vmem_limit_bytes=VMEM_LIMIT),
        name="adaln_mod",
    )(c, w_ada, b_ada.reshape(1, n), q_norm_g, k_norm_g, b_spatial, rel_bias_table)


def _mod_row(mod_ref, j, b):
    return mod_ref[j, pl.ds(b, 1), :]


def _ffn_kernel(x_ref, mod_ref, wg_ref, wu_ref, wd_ref, o_ref, wg_s, wu_s, wd_s, *, sub):
    i = pl.program_id(0)

    @pl.when(i < W_STEPS)
    def _():
        _stash_rows(wg_s, wg_ref, i)
        _stash_rows(wu_s, wu_ref, i)
        _stash_rows(wd_s, wd_ref, i)

    @pl.when(i >= W_STEPS)
    def _():
        b = lax.div(i - W_STEPS, SEQ // TM_FFN)
        shift = _mod_row(mod_ref, 3 * sub + 0, b)
        scale = _mod_row(mod_ref, 3 * sub + 1, b)
        gate = _mod_row(mod_ref, 3 * sub + 2, b)
        for r in range(0, TM_FFN, FFN_CHUNK):
            x = x_ref[r:r + FFN_CHUNK, :]
            h = _modulated_rmsnorm(x, shift, scale).astype(BF16)
            g = jnp.dot(h, wg_s[...], preferred_element_type=F32)
            u = jnp.dot(h, wu_s[...], preferred_element_type=F32)
            a = (_silu(g) * u).astype(BF16)
            y = jnp.dot(a, wd_s[...], preferred_element_type=F32)
            o_ref[r:r + FFN_CHUNK, :] = x + (0.5 * gate) * y


def _ffn(x2d, mod, wg, wu, wd, *, sub):
    t = x2d.shape[0]
    tile = lambda i: jnp.maximum(i - W_STEPS, 0)
    return pl.pallas_call(
        functools.partial(_ffn_kernel, sub=sub),
        grid=(W_STEPS + t // TM_FFN,),
        in_specs=[
            pl.BlockSpec((TM_FFN, D_MODEL), lambda i: (tile(i), 0)),
            _const_spec((N_MOD, BATCH, D_MODEL)),
            _weight_rows_spec(wg.shape),
            _weight_rows_spec(wu.shape),
            _weight_rows_spec(wd.shape),
        ],
        out_specs=pl.BlockSpec((TM_FFN, D_MODEL), lambda i: (tile(i), 0)),
        out_shape=jax.ShapeDtypeStruct((t, D_MODEL), F32),
        scratch_shapes=[pltpu.VMEM(wg.shape, BF16), pltpu.VMEM(wu.shape, BF16),
                        pltpu.VMEM(wd.shape, BF16)],
        compiler_params=pltpu.CompilerParams(
            dimension_semantics=("arbitrary",), vmem_limit_bytes=VMEM_LIMIT),
        name=f"ffn{sub}",
    )(x2d, mod, wg, wu, wd)


def _chunk_rmsnorm_rows(xt, gain_rows):
    width, tm = xt.shape
    x3 = xt.reshape(width // HEAD_DIM, HEAD_DIM, tm)
    ms = jnp.mean(x3 * x3, axis=1, keepdims=True)
    xn = (x3 * lax.rsqrt(ms + EPS)).reshape(width, tm)
    return jnp.concatenate(
        [xn[:, j * LANES:(j + 1) * LANES] * gain_rows for j in range(tm // LANES)], axis=1)


def _inproj_tile(x_ref, shift, scale, wqkvt_ref, wrest_ref, qg_ref, kg_ref,
                 lng_ref, lnb_ref, ws_ref, bs_ref,
                 qt_ref, k_ref, vt_ref, ub_ref, ga_ref, gb_ref):
    row = lax.broadcasted_iota(jnp.int32, (CHUNK, CHUNK), 0)
    col = lax.broadcasted_iota(jnp.int32, (CHUNK, CHUNK), 1)
    w_spatial = [jnp.where(row >= col, ws_ref[g], 0.0).astype(BF16)
                 for g in range(N_GROUPS)]
    for c in range(TM_PROJ // TQ):
        rows = slice(c * TQ, (c + 1) * TQ)
        x = x_ref[rows, :]
        h = _modulated_rmsnorm(x, shift, scale).astype(BF16)

        pt = lax.dot_general(wqkvt_ref[...], h, (((1,), (1,)), ((), ())),
                             preferred_element_type=F32)
        qt = _chunk_rmsnorm_rows(pt[0:QK_WIDTH], qg_ref[...])
        qt_ref[0, c] = qt.astype(BF16)
        kt = _chunk_rmsnorm_rows(pt[QK_WIDTH:2 * QK_WIDTH], kg_ref[...])
        k_ref[0, rows, :] = kt.T.astype(BF16)
        vt_ref[0, c] = pt[2 * QK_WIDTH:QKV_COLS].astype(BF16)

        rest = jnp.dot(h, wrest_ref[...], preferred_element_type=F32)
        u = _gelu(rest[:, 0:GMLP_WIDTH])
        gv = _gelu(rest[:, GMLP_WIDTH:2 * GMLP_WIDTH])
        ga_ref[0, rows, :] = jax.nn.sigmoid(
            rest[:, 2 * GMLP_WIDTH:2 * GMLP_WIDTH + D_MODEL]).astype(BF16)
        gb_ref[0, rows, :] = jax.nn.sigmoid(
            rest[:, 2 * GMLP_WIDTH + D_MODEL:REST_COLS]).astype(BF16)

        mu = jnp.mean(gv, axis=-1, keepdims=True)
        cen = gv - mu
        var = jnp.mean(cen * cen, axis=-1, keepdims=True)
        vln = ((cen * lax.rsqrt(var + EPS)) * lng_ref[...] + lnb_ref[...]).astype(BF16)

        for g in range(N_GROUPS):
            cs = slice(g * GROUP_DIM, (g + 1) * GROUP_DIM)
            for n in range(TQ // CHUNK):
                rs = slice(n * CHUNK, (n + 1) * CHUNK)
                f = jnp.dot(w_spatial[g], vln[rs, cs], preferred_element_type=F32) + bs_ref[:, cs]
                ub_ref[0, c * TQ + n * CHUNK:c * TQ + (n + 1) * CHUNK, cs] = (
                    u[rs, cs] * f).astype(BF16)


def _inproj_kernel(x_ref, mod_ref, win_ref, qg_ref, kg_ref, lng_ref, lnb_ref, ws_ref, bs_ref,
                   qt_ref, k_ref, vt_ref, ub_ref, ga_ref, gb_ref, wqkvt_s, wrest_s):
    i = pl.program_id(0)

    @pl.when(i < W_STEPS)
    def _():
        rows = win_ref.shape[0]
        r = pl.multiple_of(i * rows, rows)
        chunk = win_ref[...]
        wqkvt_s[:, pl.ds(r, rows)] = chunk[:, 0:QKV_COLS].T.astype(BF16)
        wrest_s[pl.ds(r, rows), :] = chunk[:, QKV_COLS:QKV_COLS + REST_COLS].astype(BF16)

    @pl.when(i >= W_STEPS)
    def _():
        b = lax.div(i - W_STEPS, SEQ // TM_PROJ)
        _inproj_tile(x_ref, _mod_row(mod_ref, 3, b), _mod_row(mod_ref, 4, b), wqkvt_s, wrest_s,
                     qg_ref, kg_ref, lng_ref, lnb_ref, ws_ref, bs_ref,
                     qt_ref, k_ref, vt_ref, ub_ref, ga_ref, gb_ref)


def _inproj(x3d, mod, w_in, qg_rows, kg_rows, ln_g, ln_b, w_spatial, bs_full):
    nt = SEQ // TM_PROJ
    tile = lambda i: jnp.maximum(i - W_STEPS, 0)
    tok_spec = lambda width: pl.BlockSpec(
        (1, TM_PROJ, width), lambda i: (tile(i) // nt, tile(i) % nt, 0))
    feat_spec = lambda width: pl.BlockSpec(
        (1, TM_PROJ // TQ, width, TQ), lambda i: (tile(i) // nt, tile(i) % nt, 0, 0))
    return pl.pallas_call(
        _inproj_kernel,
        grid=(W_STEPS + BATCH * nt,),
        in_specs=[
            pl.BlockSpec((None, TM_PROJ, D_MODEL), lambda i: (tile(i) // nt, tile(i) % nt, 0)),
            _const_spec((N_MOD, BATCH, D_MODEL)),
            _weight_rows_spec(w_in.shape),
            _const_spec((QK_WIDTH, LANES)),
            _const_spec((QK_WIDTH, LANES)),
            _const_spec((1, GMLP_WIDTH)),
            _const_spec((1, GMLP_WIDTH)),
            _const_spec((N_GROUPS, CHUNK, CHUNK)),
            _const_spec((CHUNK, GMLP_WIDTH)),
        ],
        out_specs=[
            feat_spec(QK_WIDTH), tok_spec(QK_WIDTH), feat_spec(ATTN_WIDTH),
            tok_spec(GMLP_WIDTH), tok_spec(D_MODEL), tok_spec(D_MODEL),
        ],
        out_shape=[
            jax.ShapeDtypeStruct((BATCH, SEQ // TQ, QK_WIDTH, TQ), BF16),
            jax.ShapeDtypeStruct((BATCH, SEQ, QK_WIDTH), BF16),
            jax.ShapeDtypeStruct((BATCH, SEQ // TQ, ATTN_WIDTH, TQ), BF16),
            jax.ShapeDtypeStruct((BATCH, SEQ, GMLP_WIDTH), BF16),
            jax.ShapeDtypeStruct((BATCH, SEQ, D_MODEL), BF16),
            jax.ShapeDtypeStruct((BATCH, SEQ, D_MODEL), BF16),
        ],
        scratch_shapes=[pltpu.VMEM((QKV_COLS, D_MODEL), BF16),
                        pltpu.VMEM((D_MODEL, REST_COLS), BF16)],
        compiler_params=pltpu.CompilerParams(
            dimension_semantics=("arbitrary",), vmem_limit_bytes=VMEM_LIMIT),
        name="inproj",
    )(x3d, mod, w_in, qg_rows, kg_rows, ln_g, ln_b, w_spatial, bs_full)


def _rel_buckets(dist):
    n = np.maximum(dist, 0)
    max_exact = N_BUCKETS // 2
    nf = np.maximum(n, 1).astype(np.float32)
    large = max_exact + (np.log(nf / np.float32(max_exact))
                         / np.float32(math.log(MAX_DISTANCE / max_exact))
                         * np.float32(N_BUCKETS - max_exact)).astype(np.int32)
    large = np.minimum(large, N_BUCKETS - 1)
    return np.where(n < max_exact, n, large).astype(np.int32)


def _attn_kernel(table_ref, bound_ref, lq1_ref, lk1_ref, lq2_ref, lk2_ref,
                 qt_ref, k_ref, vt_ref, bkt_ref, subg_ref, o_ref, bias_ref, s_scr):
    h = pl.program_id(0)
    b = pl.program_id(1)

    @pl.when(b == 0)
    def _():
        far = table_ref[N_BUCKETS - 1, h]
        key = lax.broadcasted_iota(jnp.int32, (TQ, TQ), 0)
        qry = lax.broadcasted_iota(jnp.int32, (TQ, TQ), 1)
        for t in range(2):
            bkt = bkt_ref[t]
            tile = jnp.zeros((TQ, TQ), F32)
            for bucket in range(N_BUCKETS - 1):
                tile = jnp.where(bkt == bucket, (table_ref[bucket, h] - far) * LOG2E, tile)
            if t == 0:
                tile = jnp.where(key <= qry, tile, -jnp.inf)
            bias_ref[t] = jnp.concatenate([tile, tile], axis=1)

    lam_refs = (lq1_ref, lk1_ref, lq2_ref, lk2_ref)

    @pl.when(bound_ref[0, 0] <= MAX_SAFE_LOG2)
    def _():
        _attn_program(lam_refs, subg_ref, qt_ref, k_ref, vt_ref, o_ref, bias_ref, s_scr,
                      track_max=False)

    @pl.when(jnp.logical_not(bound_ref[0, 0] <= MAX_SAFE_LOG2))
    def _():
        _attn_program(lam_refs, subg_ref, qt_ref, k_ref, vt_ref, o_ref, bias_ref, s_scr,
                      track_max=True)


def _attn_program(lam_refs, subg_ref, qt_ref, k_ref, vt_ref, o_ref, bias_ref, s_scr, *, track_max):
    lq1, lk1, lq2, lk2 = [r[...] for r in lam_refs]
    lam = (jnp.exp(jnp.sum(lq1 * lk1, axis=1, keepdims=True))
           - jnp.exp(jnp.sum(lq2 * lk2, axis=1, keepdims=True)) + LAM_INIT)
    subg = subg_ref[...] * (1.0 - LAM_INIT)
    zeros = jnp.zeros((HEAD_DIM, TQ), BF16)
    ones = jnp.ones((SUM_ROWS, TQ), BF16)
    n_strips = 2 * TQ // LANES
    nq = SEQ // TQ
    blocks = [(qi, j) for qi in range(nq) for j in range(qi + 1)]
    qpads = {}

    def qpad(qi):
        if qi not in qpads:
            qt = qt_ref[0, qi]
            qpads[qi] = jnp.concatenate(
                [jnp.concatenate([qt[0:HEAD_DIM], zeros], axis=0),
                 jnp.concatenate([zeros, qt[HEAD_DIM:2 * HEAD_DIM]], axis=0)], axis=1)
        return qpads[qi]

    def scores(qi, j):
        return jnp.dot(k_ref[0, j * TQ:(j + 1) * TQ, :], qpad(qi),
                       preferred_element_type=F32)

    def probs_tracked(s, qi, j, m):
        p, alpha, m_out = [], [], []
        for c in range(n_strips):
            sc = s[:, c * LANES:(c + 1) * LANES]
            if j == qi:
                sc = sc + bias_ref[0, :, c * LANES:(c + 1) * LANES]
            elif j == qi - 1:
                sc = sc + bias_ref[1, :, c * LANES:(c + 1) * LANES]
            blk_max = jnp.max(sc, axis=0, keepdims=True)
            m_new = blk_max if j == 0 else jnp.maximum(m[c], blk_max)
            p.append(jnp.exp2(sc - m_new).astype(BF16))
            alpha.append(None if j == 0 else jnp.exp2(m[c] - m_new))
            m_out.append(m_new)
        return jnp.concatenate(p, axis=1), alpha, m_out

    half = TQ // 2
    hi_cols = (slice(half, TQ), slice(TQ + half, 2 * TQ))

    def issue_bounded(n):
        qi, j = blocks[n]
        slot = s_scr.at[n % S_SLOTS]
        if j < qi:
            slot[...] = scores(qi, j)
            return
        kb = k_ref[0, j * TQ:(j + 1) * TQ, :]
        qp = qpad(qi)
        slot[0:half, :] = jnp.dot(kb[0:half], qp, preferred_element_type=F32)
        s_hi = jnp.dot(kb[half:TQ], jnp.concatenate([qp[:, c] for c in hi_cols], axis=1),
                       preferred_element_type=F32)
        for t, c in enumerate(hi_cols):
            slot[half:TQ, c] = s_hi[:, t * half:(t + 1) * half]

    def probs_bounded(n):
        qi, j = blocks[n]
        slot = s_scr.at[n % S_SLOTS]
        if j < qi - 1:
            return jnp.exp2(slot[...])
        if j == qi - 1:
            return jnp.exp2(slot[...] + bias_ref[1])
        lo = jnp.exp2(slot[0:half, :] + bias_ref[0, 0:half, :])
        z = jnp.zeros((half, half), F32)
        hi = [jnp.exp2(slot[half:TQ, c] + bias_ref[0, half:TQ, c]) for c in hi_cols]
        return jnp.concatenate([lo, jnp.concatenate([z, hi[0], z, hi[1]], axis=1)], axis=0)

    ahead = QK_AHEAD_TRACKED if track_max else QK_AHEAD_BOUNDED

    issue = (lambda n: scores(*blocks[n])) if track_max else issue_bounded
    pending = [issue(n) for n in range(ahead)]
    m = acc = psum = None
    for n, (qi, j) in enumerate(blocks):
        s = pending.pop(0)
        if n + ahead < len(blocks):
            pending.append(issue(n + ahead))
        if track_max:
            p, alpha, m = probs_tracked(s, qi, j, m)
            vt_aug = jnp.concatenate([vt_ref[0, j], ones], axis=0)
            pv = jnp.dot(vt_aug, p, preferred_element_type=F32)[0:V_DIM + 8]
            if j == 0:
                acc = pv
            else:
                acc = jnp.concatenate(
                    [alpha[c] * acc[:, c * LANES:(c + 1) * LANES] + pv[:, c * LANES:(c + 1) * LANES]
                     for c in range(n_strips)], axis=1)
        else:
            p = probs_bounded(n)
            blk_sum = jnp.sum(p, axis=0, keepdims=True)
            pv = jnp.dot(vt_ref[0, j], p.astype(BF16), preferred_element_type=F32)
            acc = pv if j == 0 else acc + pv
            psum = blk_sum if j == 0 else psum + blk_sum
        if j == qi:
            inv = 1.0 / (acc[V_DIM:V_DIM + 1] if track_max else psum)
            ot = (acc[0:V_DIM, 0:TQ] * inv[:, 0:TQ]
                  - acc[0:V_DIM, TQ:2 * TQ] * (lam * inv[:, TQ:2 * TQ]))
            ms = jnp.mean(ot * ot, axis=0, keepdims=True)
            on = (ot * lax.rsqrt(ms + EPS)).T
            o_ref[0, qi * TQ:(qi + 1) * TQ, :] = (on * subg).astype(BF16)


def _attention(table, bound, lam_q1, lam_k1, lam_q2, lam_k2, qt, k, vt, buckets, subln_g):
    return pl.pallas_call(
        _attn_kernel,
        grid=(N_HEADS, BATCH),
        in_specs=[
            pl.BlockSpec(memory_space=pltpu.SMEM),
            pl.BlockSpec(memory_space=pltpu.SMEM),
            _const_spec((1, HEAD_DIM)), _const_spec((1, HEAD_DIM)),
            _const_spec((1, HEAD_DIM)), _const_spec((1, HEAD_DIM)),
            pl.BlockSpec((1, SEQ // TQ, 2 * HEAD_DIM, TQ), lambda h, b: (b, 0, h, 0)),
            pl.BlockSpec((1, SEQ, 2 * HEAD_DIM), lambda h, b: (b, 0, h)),
            pl.BlockSpec((1, SEQ // TQ, V_DIM, TQ), lambda h, b: (b, 0, h, 0)),
            _const_spec((2, TQ, TQ)),
            _const_spec((1, V_DIM)),
        ],
        out_specs=pl.BlockSpec((1, SEQ, V_DIM), lambda h, b: (b, 0, h)),
        out_shape=jax.ShapeDtypeStruct((BATCH, SEQ, ATTN_WIDTH), BF16),
        scratch_shapes=[pltpu.VMEM((2, TQ, 2 * TQ), F32),
                        pltpu.VMEM((S_SLOTS, TQ, 2 * TQ), F32)],
        compiler_params=pltpu.CompilerParams(
            dimension_semantics=("arbitrary", "arbitrary"),
            vmem_limit_bytes=VMEM_LIMIT),
        name="diff_attn",
    )(table, bound, lam_q1, lam_k1, lam_q2, lam_k2, qt, k, vt, buckets, subln_g)


def _mix_kernel(x_ref, mod_ref, o_ref, ub_ref, ga_ref, gb_ref, wa_ref, wb_ref, wo_ref,
                out_ref, wa_s, wb_s, wo_s):
    i = pl.program_id(0)

    @pl.when(i < W_STEPS)
    def _():
        _stash_rows(wa_s, wa_ref, i)
        _stash_rows(wb_s, wb_ref, i)
        _stash_rows(wo_s, wo_ref, i)

    @pl.when(i >= W_STEPS)
    def _():
        gate = _mod_row(mod_ref, 5, lax.div(i - W_STEPS, SEQ // TM_MIX))
        for r in range(0, TM_MIX, MIX_CHUNK):
            rows = slice(r, r + MIX_CHUNK)
            ya = jnp.dot(o_ref[rows, :], wa_s[...], preferred_element_type=F32)
            yb = jnp.dot(ub_ref[rows, :], wb_s[...], preferred_element_type=F32)
            merged = (ga_ref[rows, :].astype(F32) * ya
                      + gb_ref[rows, :].astype(F32) * yb).astype(BF16)
            z = jnp.dot(merged, wo_s[...], preferred_element_type=F32)
            out_ref[rows, :] = x_ref[rows, :] + gate * z


def _mix(x2d, mod, o2d, ub2d, ga2d, gb2d, wa, wb, wo):
    t = x2d.shape[0]
    tile = lambda i: jnp.maximum(i - W_STEPS, 0)
    tok = lambda: pl.BlockSpec((TM_MIX, D_MODEL), lambda i: (tile(i), 0))
    return pl.pallas_call(
        _mix_kernel,
        grid=(W_STEPS + t // TM_MIX,),
        in_specs=[
            tok(),
            _const_spec((N_MOD, BATCH, D_MODEL)),
            tok(), tok(), tok(), tok(),
            _weight_rows_spec(wa.shape),
            _weight_rows_spec(wb.shape),
            _weight_rows_spec(wo.shape),
        ],
        out_specs=tok(),
        out_shape=jax.ShapeDtypeStruct((t, D_MODEL), F32),
        scratch_shapes=[pltpu.VMEM(wa.shape, BF16), pltpu.VMEM(wb.shape, BF16),
                        pltpu.VMEM(wo.shape, BF16)],
        compiler_params=pltpu.CompilerParams(
            dimension_semantics=("arbitrary",), vmem_limit_bytes=VMEM_LIMIT),
        name="mix_out",
    )(x2d, mod, o2d, ub2d, ga2d, gb2d, wa, wb, wo)


def kernel(x, c, w_ada, b_ada, w_ffn1_gate, w_ffn1_up, w_ffn1_down, w_in, q_norm_g, k_norm_g, lam_q1, lam_k1, lam_q2, lam_k2, subln_g, rel_bias_table, gmlp_ln_g, gmlp_ln_b, w_spatial, b_spatial, w_a_proj, w_b_proj, w_o, w_ffn2_gate, w_ffn2_up, w_ffn2_down):
    b, s, d = x.shape
    t = b * s

    mod, qg_rows, kg_rows, bs_full, score_bound = _adaln_mod(
        c, w_ada[0], b_ada[0], q_norm_g, k_norm_g, b_spatial[0], rel_bias_table)

    x1 = _ffn(x.reshape(t, d), mod, w_ffn1_gate[0], w_ffn1_up[0], w_ffn1_down[0], sub=0)

    qt, k, vt, ub, ga, gb = _inproj(
        x1.reshape(b, s, d), mod, w_in[0], qg_rows, kg_rows, gmlp_ln_g, gmlp_ln_b,
        w_spatial[0], bs_full)

    kk = np.arange(TQ)[:, None]
    qq = np.arange(TQ)[None, :]
    buckets = jnp.asarray(np.stack([_rel_buckets(qq - kk), _rel_buckets(TQ + qq - kk)]))
    o = _attention(rel_bias_table, score_bound, lam_q1, lam_k1, lam_q2, lam_k2,
                   qt, k, vt, buckets, subln_g)

    x2 = _mix(x1, mod, o.reshape(t, ATTN_WIDTH), ub.reshape(t, GMLP_WIDTH),
              ga.reshape(t, d), gb.reshape(t, d), w_a_proj[0], w_b_proj[0], w_o[0])

    x3 = _ffn(x2, mod, w_ffn2_gate[0], w_ffn2_up[0], w_ffn2_down[0], sub=2)
    return x3.reshape(b, s, d)
```

```python
import functools
import math

import numpy as np
import jax
import jax.numpy as jnp
from jax import lax
from jax.experimental import pallas as pl
from jax.experimental.pallas import tpu as pltpu

F32 = jnp.float32
BF16 = jnp.bfloat16

D_MODEL = 1024
BATCH = 8
SEQ = 2048
N_HEADS = 8
HEAD_DIM = 64
V_DIM = 2 * HEAD_DIM
QK_WIDTH = N_HEADS * 2 * HEAD_DIM
ATTN_WIDTH = N_HEADS * V_DIM
N_GROUPS = 8
CHUNK = 128
GMLP_WIDTH = 1024
GROUP_DIM = GMLP_WIDTH // N_GROUPS
N_BUCKETS = 32
MAX_DISTANCE = 128
D_FF = 2816
N_MOD = 9
EPS = 1e-6
LAM_INIT = 0.8 - 0.6 * math.exp(-0.3 * 0)
LOG2E = math.log2(math.e)
QKV_COLS = 2 * QK_WIDTH + ATTN_WIDTH
REST_COLS = 2 * GMLP_WIDTH + 2 * D_MODEL

LANES = 128
VMEM_LIMIT = 56 * 1024 * 1024

FFN_CHUNK = 256
TM_FFN = 1024
TM_PROJ = 512
MIX_CHUNK = 256
TM_MIX = 1024
TQ = 256
SUM_ROWS = 16
W_STEPS = 8
MAX_SAFE_LOG2 = 64.0
BF16_SLACK = 1.02
QK_AHEAD_TRACKED = 2
QK_AHEAD_BOUNDED = 3
S_SLOTS = QK_AHEAD_BOUNDED + 1


def _const_spec(shape):
    nd = len(shape)
    return pl.BlockSpec(shape, lambda *_: (0,) * nd, pipeline_mode=pl.Buffered(1))


def _weight_rows_spec(shape):
    rows, cols = shape
    return pl.BlockSpec((rows // W_STEPS, cols),
                        lambda i: (jnp.minimum(i, W_STEPS - 1), 0))


def _stash_rows(dst_ref, chunk_ref, i):
    rows = chunk_ref.shape[0]
    dst_ref[pl.ds(pl.multiple_of(i * rows, rows), rows), :] = chunk_ref[...].astype(BF16)


def _silu(x):
    return x * jax.nn.sigmoid(x)


def _gelu(x):
    return 0.5 * x * (1.0 + lax.erf(x * (1.0 / math.sqrt(2.0))))


def _modulated_rmsnorm(x, shift, scale):
    ms = jnp.mean(x * x, axis=-1, keepdims=True)
    return (x * lax.rsqrt(ms + EPS)) * (1.0 + scale) + shift


def _as_column(row):
    n = row.shape[1]
    r = lax.broadcasted_iota(jnp.int32, (n, n), 0)
    c = lax.broadcasted_iota(jnp.int32, (n, n), 1)
    return jnp.sum(jnp.where(r == c, row, 0.0), axis=1, keepdims=True)


def _mod_kernel(c_ref, w_ref, b_ref, qg_ref, kg_ref, bsp_ref, table_ref,
                lq1_ref, lk1_ref, lq2_ref, lk2_ref, subg_ref,
                o_ref, qgr_ref, kgr_ref, bsf_ref, bound_ref, lam_ref, subs_ref):
    a = _silu(c_ref[...]).astype(BF16)
    o_ref[0] = jnp.dot(a, w_ref[...].astype(BF16), preferred_element_type=F32) + b_ref[...]

    @pl.when(pl.program_id(0) == 0)
    def _():
        qg = qg_ref[...] * (HEAD_DIM ** -0.5 * LOG2E)
        kg = kg_ref[...]
        reps = QK_WIDTH // HEAD_DIM
        qgr_ref[...] = jnp.concatenate(
            [jnp.broadcast_to(_as_column(qg), (HEAD_DIM, LANES))] * reps, axis=0)
        kgr_ref[...] = jnp.concatenate(
            [jnp.broadcast_to(_as_column(kg), (HEAD_DIM, LANES))] * reps, axis=0)
        bsf_ref[...] = jnp.concatenate(
            [jnp.broadcast_to(_as_column(bsp_ref[g:g + 1, :]), (CHUNK, GROUP_DIM))
             for g in range(N_GROUPS)], axis=1)
        table = table_ref[...]
        rel = jnp.abs(table - table[N_BUCKETS - 1:N_BUCKETS, :])
        bound_ref[...] = (
            HEAD_DIM * BF16_SLACK * jnp.max(jnp.abs(qg), axis=(0, 1), keepdims=True)
            * jnp.max(jnp.abs(kg), axis=(0, 1), keepdims=True)
            + LOG2E * jnp.max(rel, axis=(0, 1), keepdims=True))
        lam_ref[...] = jnp.concatenate(
            [lq1_ref[...], lk1_ref[...], lq2_ref[...], lk2_ref[...]], axis=0)
        subs_ref[...] = subg_ref[...] * (1.0 - LAM_INIT)


def _adaln_mod(c, w_ada, b_ada, q_norm_g, k_norm_g, b_spatial, rel_bias_table,
               lam_q1, lam_k1, lam_q2, lam_k2, subln_g):
    n = w_ada.shape[1]
    whole = lambda shape: pl.BlockSpec(shape, lambda j: (0,) * len(shape))
    return pl.pallas_call(
        _mod_kernel,
        grid=(n // D_MODEL,),
        in_specs=[
            whole((BATCH, D_MODEL)),
            pl.BlockSpec((D_MODEL, D_MODEL), lambda j: (0, j)),
            pl.BlockSpec((1, D_MODEL), lambda j: (0, j)),
            whole((1, HEAD_DIM)), whole((1, HEAD_DIM)),
            whole((N_GROUPS, CHUNK)), whole((N_BUCKETS, N_HEADS)),
            whole((1, HEAD_DIM)), whole((1, HEAD_DIM)), whole((1, HEAD_DIM)), whole((1, HEAD_DIM)),
            whole((1, V_DIM)),
        ],
        out_specs=[
            pl.BlockSpec((1, BATCH, D_MODEL), lambda j: (j, 0, 0)),
            whole((QK_WIDTH, LANES)), whole((QK_WIDTH, LANES)),
            whole((CHUNK, GMLP_WIDTH)), whole((1, 1)), whole((4, HEAD_DIM)), whole((1, V_DIM)),
        ],
        out_shape=[
            jax.ShapeDtypeStruct((n // D_MODEL, BATCH, D_MODEL), F32),
            jax.ShapeDtypeStruct((QK_WIDTH, LANES), F32),
            jax.ShapeDtypeStruct((QK_WIDTH, LANES), F32),
            jax.ShapeDtypeStruct((CHUNK, GMLP_WIDTH), F32),
            jax.ShapeDtypeStruct((1, 1), F32),
            jax.ShapeDtypeStruct((4, HEAD_DIM), F32),
            jax.ShapeDtypeStruct((1, V_DIM), F32),
        ],
        compiler_params=pltpu.CompilerParams(
            dimension_semantics=("arbitrary",), vmem_limit_bytes=VMEM_LIMIT),
        name="adaln_mod",
    )(c, w_ada, b_ada.reshape(1, n), q_norm_g, k_norm_g, b_spatial, rel_bias_table,
      lam_q1, lam_k1, lam_q2, lam_k2, subln_g)


def _mod_row(mod_ref, j, b):
    return mod_ref[j, pl.ds(b, 1), :]


def _ffn_kernel(x_ref, mod_ref, wg_ref, wu_ref, wd_ref, o_ref, wg_s, wu_s, wd_s, *, sub):
    i = pl.program_id(0)

    @pl.when(i < W_STEPS)
    def _():
        _stash_rows(wg_s, wg_ref, i)
        _stash_rows(wu_s, wu_ref, i)
        _stash_rows(wd_s, wd_ref, i)

    @pl.when(i >= W_STEPS)
    def _():
        b = lax.div(i - W_STEPS, SEQ // TM_FFN)
        shift = _mod_row(mod_ref, 3 * sub + 0, b)
        scale = _mod_row(mod_ref, 3 * sub + 1, b)
        gate = _mod_row(mod_ref, 3 * sub + 2, b)
        for r in range(0, TM_FFN, FFN_CHUNK):
            x = x_ref[r:r + FFN_CHUNK, :]
            h = _modulated_rmsnorm(x, shift, scale).astype(BF16)
            g = jnp.dot(h, wg_s[...], preferred_element_type=F32)
            u = jnp.dot(h, wu_s[...], preferred_element_type=F32)
            a = (_silu(g) * u).astype(BF16)
            y = jnp.dot(a, wd_s[...], preferred_element_type=F32)
            o_ref[r:r + FFN_CHUNK, :] = x + (0.5 * gate) * y


def _ffn(x2d, mod, wg, wu, wd, *, sub):
    t = x2d.shape[0]
    tile = lambda i: jnp.maximum(i - W_STEPS, 0)
    return pl.pallas_call(
        functools.partial(_ffn_kernel, sub=sub),
        grid=(W_STEPS + t // TM_FFN,),
        in_specs=[
            pl.BlockSpec((TM_FFN, D_MODEL), lambda i: (tile(i), 0)),
            _const_spec((N_MOD, BATCH, D_MODEL)),
            _weight_rows_spec(wg.shape),
            _weight_rows_spec(wu.shape),
            _weight_rows_spec(wd.shape),
        ],
        out_specs=pl.BlockSpec((TM_FFN, D_MODEL), lambda i: (tile(i), 0)),
        out_shape=jax.ShapeDtypeStruct((t, D_MODEL), F32),
        scratch_shapes=[pltpu.VMEM(wg.shape, BF16), pltpu.VMEM(wu.shape, BF16),
                        pltpu.VMEM(wd.shape, BF16)],
        compiler_params=pltpu.CompilerParams(
            dimension_semantics=("arbitrary",), vmem_limit_bytes=VMEM_LIMIT),
        name=f"ffn{sub}",
    )(x2d, mod, wg, wu, wd)


def _chunk_rmsnorm_rows(xt, gain_rows):
    width, tm = xt.shape
    x3 = xt.reshape(width // HEAD_DIM, HEAD_DIM, tm)
    ms = jnp.mean(x3 * x3, axis=1, keepdims=True)
    xn = (x3 * lax.rsqrt(ms + EPS)).reshape(width, tm)
    return jnp.concatenate(
        [xn[:, j * LANES:(j + 1) * LANES] * gain_rows for j in range(tm // LANES)], axis=1)


def _inproj_tile(x_ref, shift, scale, wqkvt_ref, wrest_ref, qg_ref, kg_ref,
                 lng_ref, lnb_ref, ws_ref, bs_ref,
                 qt_ref, k_ref, vt_ref, ub_ref, ga_ref, gb_ref):
    row = lax.broadcasted_iota(jnp.int32, (CHUNK, CHUNK), 0)
    col = lax.broadcasted_iota(jnp.int32, (CHUNK, CHUNK), 1)
    w_spatial = [jnp.where(row >= col, ws_ref[g], 0.0).astype(BF16)
                 for g in range(N_GROUPS)]
    for c in range(TM_PROJ // TQ):
        rows = slice(c * TQ, (c + 1) * TQ)
        x = x_ref[rows, :]
        h = _modulated_rmsnorm(x, shift, scale).astype(BF16)

        pt = lax.dot_general(wqkvt_ref[...], h, (((1,), (1,)), ((), ())),
                             preferred_element_type=F32)
        qt = _chunk_rmsnorm_rows(pt[0:QK_WIDTH], qg_ref[...])
        qt_ref[0, c] = qt.astype(BF16)
        kt = _chunk_rmsnorm_rows(pt[QK_WIDTH:2 * QK_WIDTH], kg_ref[...])
        k_ref[0, rows, :] = kt.T.astype(BF16)
        vt_ref[0, c] = pt[2 * QK_WIDTH:QKV_COLS].astype(BF16)

        rest = jnp.dot(h, wrest_ref[...], preferred_element_type=F32)
        u = _gelu(rest[:, 0:GMLP_WIDTH])
        gv = _gelu(rest[:, GMLP_WIDTH:2 * GMLP_WIDTH])
        ga_ref[0, rows, :] = jax.nn.sigmoid(
            rest[:, 2 * GMLP_WIDTH:2 * GMLP_WIDTH + D_MODEL]).astype(BF16)
        gb_ref[0, rows, :] = jax.nn.sigmoid(
            rest[:, 2 * GMLP_WIDTH + D_MODEL:REST_COLS]).astype(BF16)

        mu = jnp.mean(gv, axis=-1, keepdims=True)
        cen = gv - mu
        var = jnp.mean(cen * cen, axis=-1, keepdims=True)
        vln = ((cen * lax.rsqrt(var + EPS)) * lng_ref[...] + lnb_ref[...]).astype(BF16)

        for g in range(N_GROUPS):
            cs = slice(g * GROUP_DIM, (g + 1) * GROUP_DIM)
            for n in range(TQ // CHUNK):
                rs = slice(n * CHUNK, (n + 1) * CHUNK)
                f = jnp.dot(w_spatial[g], vln[rs, cs], preferred_element_type=F32) + bs_ref[:, cs]
                ub_ref[0, c * TQ + n * CHUNK:c * TQ + (n + 1) * CHUNK, cs] = (
                    u[rs, cs] * f).astype(BF16)


def _inproj_kernel(x_ref, mod_ref, win_ref, qg_ref, kg_ref, lng_ref, lnb_ref, ws_ref, bs_ref,
                   qt_ref, k_ref, vt_ref, ub_ref, ga_ref, gb_ref, wqkvt_s, wrest_s):
    i = pl.program_id(0)

    @pl.when(i < W_STEPS)
    def _():
        rows = win_ref.shape[0]
        r = pl.multiple_of(i * rows, rows)
        chunk = win_ref[...]
        wqkvt_s[:, pl.ds(r, rows)] = chunk[:, 0:QKV_COLS].T.astype(BF16)
        wrest_s[pl.ds(r, rows), :] = chunk[:, QKV_COLS:QKV_COLS + REST_COLS].astype(BF16)

    @pl.when(i >= W_STEPS)
    def _():
        b = lax.div(i - W_STEPS, SEQ // TM_PROJ)
        _inproj_tile(x_ref, _mod_row(mod_ref, 3, b), _mod_row(mod_ref, 4, b), wqkvt_s, wrest_s,
                     qg_ref, kg_ref, lng_ref, lnb_ref, ws_ref, bs_ref,
                     qt_ref, k_ref, vt_ref, ub_ref, ga_ref, gb_ref)


def _inproj(x3d, mod, w_in, qg_rows, kg_rows, ln_g, ln_b, w_spatial, bs_full):
    nt = SEQ // TM_PROJ
    tile = lambda i: jnp.maximum(i - W_STEPS, 0)
    tok_spec = lambda width: pl.BlockSpec(
        (1, TM_PROJ, width), lambda i: (tile(i) // nt, tile(i) % nt, 0))
    feat_spec = lambda width: pl.BlockSpec(
        (1, TM_PROJ // TQ, width, TQ), lambda i: (tile(i) // nt, tile(i) % nt, 0, 0))
    return pl.pallas_call(
        _inproj_kernel,
        grid=(W_STEPS + BATCH * nt,),
        in_specs=[
            pl.BlockSpec((None, TM_PROJ, D_MODEL), lambda i: (tile(i) // nt, tile(i) % nt, 0)),
            _const_spec((N_MOD, BATCH, D_MODEL)),
            _weight_rows_spec(w_in.shape),
            _const_spec((QK_WIDTH, LANES)),
            _const_spec((QK_WIDTH, LANES)),
            _const_spec((1, GMLP_WIDTH)),
            _const_spec((1, GMLP_WIDTH)),
            _const_spec((N_GROUPS, CHUNK, CHUNK)),
            _const_spec((CHUNK, GMLP_WIDTH)),
        ],
        out_specs=[
            feat_spec(QK_WIDTH), tok_spec(QK_WIDTH), feat_spec(ATTN_WIDTH),
            tok_spec(GMLP_WIDTH), tok_spec(D_MODEL), tok_spec(D_MODEL),
        ],
        out_shape=[
            jax.ShapeDtypeStruct((BATCH, SEQ // TQ, QK_WIDTH, TQ), BF16),
            jax.ShapeDtypeStruct((BATCH, SEQ, QK_WIDTH), BF16),
            jax.ShapeDtypeStruct((BATCH, SEQ // TQ, ATTN_WIDTH, TQ), BF16),
            jax.ShapeDtypeStruct((BATCH, SEQ, GMLP_WIDTH), BF16),
            jax.ShapeDtypeStruct((BATCH, SEQ, D_MODEL), BF16),
            jax.ShapeDtypeStruct((BATCH, SEQ, D_MODEL), BF16),
        ],
        scratch_shapes=[pltpu.VMEM((QKV_COLS, D_MODEL), BF16),
                        pltpu.VMEM((D_MODEL, REST_COLS), BF16)],
        compiler_params=pltpu.CompilerParams(
            dimension_semantics=("arbitrary",), vmem_limit_bytes=VMEM_LIMIT),
        name="inproj",
    )(x3d, mod, w_in, qg_rows, kg_rows, ln_g, ln_b, w_spatial, bs_full)


def _rel_buckets(dist):
    n = np.maximum(dist, 0)
    max_exact = N_BUCKETS // 2
    nf = np.maximum(n, 1).astype(np.float32)
    large = max_exact + (np.log(nf / np.float32(max_exact))
                         / np.float32(math.log(MAX_DISTANCE / max_exact))
                         * np.float32(N_BUCKETS - max_exact)).astype(np.int32)
    large = np.minimum(large, N_BUCKETS - 1)
    return np.where(n < max_exact, n, large).astype(np.int32)


def _attn_kernel(table_ref, bound_ref, lam_ref, qt_ref, k_ref, vt_ref, bkt_ref, subg_ref,
                 o_ref, bias_ref, s_scr):
    h = pl.program_id(0)
    b = pl.program_id(1)

    @pl.when(b == 0)
    def _():
        far = table_ref[N_BUCKETS - 1, h]
        key = lax.broadcasted_iota(jnp.int32, (TQ, TQ), 0)
        qry = lax.broadcasted_iota(jnp.int32, (TQ, TQ), 1)
        for t in range(2):
            bkt = bkt_ref[t]
            tile = jnp.zeros((TQ, TQ), F32)
            for bucket in range(N_BUCKETS - 1):
                tile = jnp.where(bkt == bucket, (table_ref[bucket, h] - far) * LOG2E, tile)
            if t == 0:
                tile = jnp.where(key <= qry, tile, -jnp.inf)
            bias_ref[t] = jnp.concatenate([tile, tile], axis=1)

    @pl.when(bound_ref[0, 0] <= MAX_SAFE_LOG2)
    def _():
        _attn_program(lam_ref, subg_ref, qt_ref, k_ref, vt_ref, o_ref, bias_ref, s_scr,
                      track_max=False)

    @pl.when(jnp.logical_not(bound_ref[0, 0] <= MAX_SAFE_LOG2))
    def _():
        _attn_program(lam_ref, subg_ref, qt_ref, k_ref, vt_ref, o_ref, bias_ref, s_scr,
                      track_max=True)


def _attn_program(lam_ref, subg_ref, qt_ref, k_ref, vt_ref, o_ref, bias_ref, s_scr, *, track_max):
    lp = lam_ref[...]
    lam = (jnp.exp(jnp.sum(lp[0:1] * lp[1:2], axis=1, keepdims=True))
           - jnp.exp(jnp.sum(lp[2:3] * lp[3:4], axis=1, keepdims=True)) + LAM_INIT)
    subg = subg_ref[...]
    zeros = jnp.zeros((HEAD_DIM, TQ), BF16)
    ones = jnp.ones((SUM_ROWS, TQ), BF16)
    n_strips = 2 * TQ // LANES
    nq = SEQ // TQ
    blocks = [(qi, j) for qi in range(nq) for j in range(qi + 1)]
    qpads = {}

    def qpad(qi):
        if qi not in qpads:
            qt = qt_ref[0, qi]
            qpads[qi] = jnp.concatenate(
                [jnp.concatenate([qt[0:HEAD_DIM], zeros], axis=0),
                 jnp.concatenate([zeros, qt[HEAD_DIM:2 * HEAD_DIM]], axis=0)], axis=1)
        return qpads[qi]

    def scores(qi, j):
        return jnp.dot(k_ref[0, j * TQ:(j + 1) * TQ, :], qpad(qi),
                       preferred_element_type=F32)

    def probs_tracked(s, qi, j, m):
        p, alpha, m_out = [], [], []
        for c in range(n_strips):
            sc = s[:, c * LANES:(c + 1) * LANES]
            if j == qi:
                sc = sc + bias_ref[0, :, c * LANES:(c + 1) * LANES]
            elif j == qi - 1:
                sc = sc + bias_ref[1, :, c * LANES:(c + 1) * LANES]
            blk_max = jnp.max(sc, axis=0, keepdims=True)
            m_new = blk_max if j == 0 else jnp.maximum(m[c], blk_max)
            p.append(jnp.exp2(sc - m_new).astype(BF16))
            alpha.append(None if j == 0 else jnp.exp2(m[c] - m_new))
            m_out.append(m_new)
        return jnp.concatenate(p, axis=1), alpha, m_out

    half = TQ // 2
    hi_cols = (slice(half, TQ), slice(TQ + half, 2 * TQ))

    def issue_bounded(n):
        qi, j = blocks[n]
        slot = s_scr.at[n % S_SLOTS]
        if j < qi:
            slot[...] = scores(qi, j)
            return
        kb = k_ref[0, j * TQ:(j + 1) * TQ, :]
        qp = qpad(qi)
        slot[0:half, :] = jnp.dot(kb[0:half], qp, preferred_element_type=F32)
        s_hi = jnp.dot(kb[half:TQ], jnp.concatenate([qp[:, c] for c in hi_cols], axis=1),
                       preferred_element_type=F32)
        for t, c in enumerate(hi_cols):
            slot[half:TQ, c] = s_hi[:, t * half:(t + 1) * half]

    def probs_bounded(n):
        qi, j = blocks[n]
        slot = s_scr.at[n % S_SLOTS]
        if j < qi - 1:
            return jnp.exp2(slot[...])
        if j == qi - 1:
            return jnp.exp2(slot[...] + bias_ref[1])
        lo = jnp.exp2(slot[0:half, :] + bias_ref[0, 0:half, :])
        z = jnp.zeros((half, half), F32)
        hi = [jnp.exp2(slot[half:TQ, c] + bias_ref[0, half:TQ, c]) for c in hi_cols]
        return jnp.concatenate([lo, jnp.concatenate([z, hi[0], z, hi[1]], axis=1)], axis=0)

    ahead = QK_AHEAD_TRACKED if track_max else QK_AHEAD_BOUNDED

    issue = (lambda n: scores(*blocks[n])) if track_max else issue_bounded
    pending = [issue(n) for n in range(ahead)]
    m = acc = psum = None
    for n, (qi, j) in enumerate(blocks):
        s = pending.pop(0)
        if n + ahead < len(blocks):
            pending.append(issue(n + ahead))
        if track_max:
            p, alpha, m = probs_tracked(s, qi, j, m)
            vt_aug = jnp.concatenate([vt_ref[0, j], ones], axis=0)
            pv = jnp.dot(vt_aug, p, preferred_element_type=F32)[0:V_DIM + 8]
            if j == 0:
                acc = pv
            else:
                acc = jnp.concatenate(
                    [alpha[c] * acc[:, c * LANES:(c + 1) * LANES] + pv[:, c * LANES:(c + 1) * LANES]
                     for c in range(n_strips)], axis=1)
        else:
            p = probs_bounded(n)
            blk_sum = jnp.sum(p, axis=0, keepdims=True)
            pv = jnp.dot(vt_ref[0, j], p.astype(BF16), preferred_element_type=F32)
            acc = pv if j == 0 else acc + pv
            psum = blk_sum if j == 0 else psum + blk_sum
        if j == qi:
            inv = 1.0 / (acc[V_DIM:V_DIM + 1] if track_max else psum)
            ot = (acc[0:V_DIM, 0:TQ] * inv[:, 0:TQ]
                  - acc[0:V_DIM, TQ:2 * TQ] * (lam * inv[:, TQ:2 * TQ]))
            ms = jnp.mean(ot * ot, axis=0, keepdims=True)
            on = (ot * lax.rsqrt(ms + EPS)).T
            o_ref[0, qi * TQ:(qi + 1) * TQ, :] = (on * subg).astype(BF16)


def _attention(table, bound, lam_rows, qt, k, vt, buckets, sub_gain):
    return pl.pallas_call(
        _attn_kernel,
        grid=(N_HEADS, BATCH),
        in_specs=[
            pl.BlockSpec(memory_space=pltpu.SMEM),
            pl.BlockSpec(memory_space=pltpu.SMEM),
            _const_spec((4, HEAD_DIM)),
            pl.BlockSpec((1, SEQ // TQ, 2 * HEAD_DIM, TQ), lambda h, b: (b, 0, h, 0)),
            pl.BlockSpec((1, SEQ, 2 * HEAD_DIM), lambda h, b: (b, 0, h)),
            pl.BlockSpec((1, SEQ // TQ, V_DIM, TQ), lambda h, b: (b, 0, h, 0)),
            _const_spec((2, TQ, TQ)),
            _const_spec((1, V_DIM)),
        ],
        out_specs=pl.BlockSpec((1, SEQ, V_DIM), lambda h, b: (b, 0, h)),
        out_shape=jax.ShapeDtypeStruct((BATCH, SEQ, ATTN_WIDTH), BF16),
        scratch_shapes=[pltpu.VMEM((2, TQ, 2 * TQ), F32),
                        pltpu.VMEM((S_SLOTS, TQ, 2 * TQ), F32)],
        compiler_params=pltpu.CompilerParams(
            dimension_semantics=("arbitrary", "arbitrary"),
            vmem_limit_bytes=VMEM_LIMIT),
        name="diff_attn",
    )(table, bound, lam_rows, qt, k, vt, buckets, sub_gain)


def _mix_kernel(x_ref, mod_ref, o_ref, ub_ref, ga_ref, gb_ref, wa_ref, wb_ref, wo_ref,
                out_ref, wa_s, wb_s, wo_s):
    i = pl.program_id(0)

    @pl.when(i < W_STEPS)
    def _():
        _stash_rows(wa_s, wa_ref, i)
        _stash_rows(wb_s, wb_ref, i)
        _stash_rows(wo_s, wo_ref, i)

    @pl.when(i >= W_STEPS)
    def _():
        gate = _mod_row(mod_ref, 5, lax.div(i - W_STEPS, SEQ // TM_MIX))
        for r in range(0, TM_MIX, MIX_CHUNK):
            rows = slice(r, r + MIX_CHUNK)
            ya = jnp.dot(o_ref[rows, :], wa_s[...], preferred_element_type=F32)
            yb = jnp.dot(ub_ref[rows, :], wb_s[...], preferred_element_type=F32)
            merged = (ga_ref[rows, :].astype(F32) * ya
                      + gb_ref[rows, :].astype(F32) * yb).astype(BF16)
            z = jnp.dot(merged, wo_s[...], preferred_element_type=F32)
            out_ref[rows, :] = x_ref[rows, :] + gate * z


def _mix(x2d, mod, o2d, ub2d, ga2d, gb2d, wa, wb, wo):
    t = x2d.shape[0]
    tile = lambda i: jnp.maximum(i - W_STEPS, 0)
    tok = lambda: pl.BlockSpec((TM_MIX, D_MODEL), lambda i: (tile(i), 0))
    return pl.pallas_call(
        _mix_kernel,
        grid=(W_STEPS + t // TM_MIX,),
        in_specs=[
            tok(),
            _const_spec((N_MOD, BATCH, D_MODEL)),
            tok(), tok(), tok(), tok(),
            _weight_rows_spec(wa.shape),
            _weight_rows_spec(wb.shape),
            _weight_rows_spec(wo.shape),
        ],
        out_specs=tok(),
        out_shape=jax.ShapeDtypeStruct((t, D_MODEL), F32),
        scratch_shapes=[pltpu.VMEM(wa.shape, BF16), pltpu.VMEM(wb.shape, BF16),
                        pltpu.VMEM(wo.shape, BF16)],
        compiler_params=pltpu.CompilerParams(
            dimension_semantics=("arbitrary",), vmem_limit_bytes=VMEM_LIMIT),
        name="mix_out",
    )(x2d, mod, o2d, ub2d, ga2d, gb2d, wa, wb, wo)


def kernel(x, c, w_ada, b_ada, w_ffn1_gate, w_ffn1_up, w_ffn1_down, w_in, q_norm_g, k_norm_g, lam_q1, lam_k1, lam_q2, lam_k2, subln_g, rel_bias_table, gmlp_ln_g, gmlp_ln_b, w_spatial, b_spatial, w_a_proj, w_b_proj, w_o, w_ffn2_gate, w_ffn2_up, w_ffn2_down):
    b, s, d = x.shape
    t = b * s

    mod, qg_rows, kg_rows, bs_full, score_bound, lam_rows, sub_gain = _adaln_mod(
        c, w_ada[0], b_ada[0], q_norm_g, k_norm_g, b_spatial[0], rel_bias_table,
        lam_q1, lam_k1, lam_q2, lam_k2, subln_g)

    x1 = _ffn(x.reshape(t, d), mod, w_ffn1_gate[0], w_ffn1_up[0], w_ffn1_down[0], sub=0)

    qt, k, vt, ub, ga, gb = _inproj(
        x1.reshape(b, s, d), mod, w_in[0], qg_rows, kg_rows, gmlp_ln_g, gmlp_ln_b,
        w_spatial[0], bs_full)

    kk = np.arange(TQ)[:, None]
    qq = np.arange(TQ)[None, :]
    buckets = jnp.asarray(np.stack([_rel_buckets(qq - kk), _rel_buckets(TQ + qq - kk)]))
    o = _attention(rel_bias_table, score_bound, lam_rows, qt, k, vt, buckets, sub_gain)

    x2 = _mix(x1, mod, o.reshape(t, ATTN_WIDTH), ub.reshape(t, GMLP_WIDTH),
              ga.reshape(t, d), gb.reshape(t, d), w_a_proj[0], w_b_proj[0], w_o[0])

    x3 = _ffn(x2, mod, w_ffn2_gate[0], w_ffn2_up[0], w_ffn2_down[0], sub=2)
    return x3.reshape(b, s, d)
```

```python
import functools
import math

import numpy as np
import jax
import jax.numpy as jnp
from jax import lax
from jax.experimental import pallas as pl
from jax.experimental.pallas import tpu as pltpu

F32 = jnp.float32
BF16 = jnp.bfloat16

D_MODEL = 1024
BATCH = 8
SEQ = 2048
N_HEADS = 8
HEAD_DIM = 64
V_DIM = 2 * HEAD_DIM
QK_WIDTH = N_HEADS * 2 * HEAD_DIM
ATTN_WIDTH = N_HEADS * V_DIM
N_GROUPS = 8
CHUNK = 128
GMLP_WIDTH = 1024
GROUP_DIM = GMLP_WIDTH // N_GROUPS
N_BUCKETS = 32
MAX_DISTANCE = 128
D_FF = 2816
N_MOD = 9
EPS = 1e-6
LAM_INIT = 0.8 - 0.6 * math.exp(-0.3 * 0)
LOG2E = math.log2(math.e)
QKV_COLS = 2 * QK_WIDTH + ATTN_WIDTH
REST_COLS = 2 * GMLP_WIDTH + 2 * D_MODEL

LANES = 128
VMEM_LIMIT = 56 * 1024 * 1024

FFN_CHUNK = 256
TM_FFN = 1024
TM_PROJ = 512
MIX_CHUNK = 256
TM_MIX = 1024
TQ = 256
SUM_ROWS = 16
W_STEPS = 8
MAX_SAFE_LOG2 = 64.0
BF16_SLACK = 1.02
QK_AHEAD_TRACKED = 2
QK_AHEAD_BOUNDED = 3
S_SLOTS = QK_AHEAD_BOUNDED + 1
HEADS_PER_STEP = 2


def _const_spec(shape):
    nd = len(shape)
    return pl.BlockSpec(shape, lambda *_: (0,) * nd, pipeline_mode=pl.Buffered(1))


def _weight_rows_spec(shape):
    rows, cols = shape
    return pl.BlockSpec((rows // W_STEPS, cols),
                        lambda i: (jnp.minimum(i, W_STEPS - 1), 0))


def _stash_rows(dst_ref, chunk_ref, i):
    rows = chunk_ref.shape[0]
    dst_ref[pl.ds(pl.multiple_of(i * rows, rows), rows), :] = chunk_ref[...].astype(BF16)


def _silu(x):
    return x * jax.nn.sigmoid(x)


def _gelu(x):
    return 0.5 * x * (1.0 + lax.erf(x * (1.0 / math.sqrt(2.0))))


def _modulated_rmsnorm(x, shift, scale):
    ms = jnp.mean(x * x, axis=-1, keepdims=True)
    return (x * lax.rsqrt(ms + EPS)) * (1.0 + scale) + shift


def _as_column(row):
    n = row.shape[1]
    r = lax.broadcasted_iota(jnp.int32, (n, n), 0)
    c = lax.broadcasted_iota(jnp.int32, (n, n), 1)
    return jnp.sum(jnp.where(r == c, row, 0.0), axis=1, keepdims=True)


def _mod_kernel(c_ref, w_ref, b_ref, qg_ref, kg_ref, bsp_ref, table_ref,
                lq1_ref, lk1_ref, lq2_ref, lk2_ref, subg_ref,
                o_ref, qgr_ref, kgr_ref, bsf_ref, bound_ref, lam_ref, subs_ref):
    a = _silu(c_ref[...]).astype(BF16)
    o_ref[0] = jnp.dot(a, w_ref[...].astype(BF16), preferred_element_type=F32) + b_ref[...]

    @pl.when(pl.program_id(0) == 0)
    def _():
        qg = qg_ref[...] * (HEAD_DIM ** -0.5 * LOG2E)
        kg = kg_ref[...]
        reps = QK_WIDTH // HEAD_DIM
        qgr_ref[...] = jnp.concatenate(
            [jnp.broadcast_to(_as_column(qg), (HEAD_DIM, LANES))] * reps, axis=0)
        kgr_ref[...] = jnp.concatenate(
            [jnp.broadcast_to(_as_column(kg), (HEAD_DIM, LANES))] * reps, axis=0)
        bsf_ref[...] = jnp.concatenate(
            [jnp.broadcast_to(_as_column(bsp_ref[g:g + 1, :]), (CHUNK, GROUP_DIM))
             for g in range(N_GROUPS)], axis=1)
        table = table_ref[...]
        rel = jnp.abs(table - table[N_BUCKETS - 1:N_BUCKETS, :])
        bound_ref[...] = (
            HEAD_DIM * BF16_SLACK * jnp.max(jnp.abs(qg), axis=(0, 1), keepdims=True)
            * jnp.max(jnp.abs(kg), axis=(0, 1), keepdims=True)
            + LOG2E * jnp.max(rel, axis=(0, 1), keepdims=True))
        lam_ref[...] = jnp.concatenate(
            [lq1_ref[...], lk1_ref[...], lq2_ref[...], lk2_ref[...]], axis=0)
        subs_ref[...] = subg_ref[...] * (1.0 - LAM_INIT)


def _adaln_mod(c, w_ada, b_ada, q_norm_g, k_norm_g, b_spatial, rel_bias_table,
               lam_q1, lam_k1, lam_q2, lam_k2, subln_g):
    n = w_ada.shape[1]
    whole = lambda shape: pl.BlockSpec(shape, lambda j: (0,) * len(shape))
    return pl.pallas_call(
        _mod_kernel,
        grid=(n // D_MODEL,),
        in_specs=[
            whole((BATCH, D_MODEL)),
            pl.BlockSpec((D_MODEL, D_MODEL), lambda j: (0, j)),
            pl.BlockSpec((1, D_MODEL), lambda j: (0, j)),
            whole((1, HEAD_DIM)), whole((1, HEAD_DIM)),
            whole((N_GROUPS, CHUNK)), whole((N_BUCKETS, N_HEADS)),
            whole((1, HEAD_DIM)), whole((1, HEAD_DIM)), whole((1, HEAD_DIM)), whole((1, HEAD_DIM)),
            whole((1, V_DIM)),
        ],
        out_specs=[
            pl.BlockSpec((1, BATCH, D_MODEL), lambda j: (j, 0, 0)),
            whole((QK_WIDTH, LANES)), whole((QK_WIDTH, LANES)),
            whole((CHUNK, GMLP_WIDTH)), whole((1, 1)), whole((4, HEAD_DIM)), whole((1, V_DIM)),
        ],
        out_shape=[
            jax.ShapeDtypeStruct((n // D_MODEL, BATCH, D_MODEL), F32),
            jax.ShapeDtypeStruct((QK_WIDTH, LANES), F32),
            jax.ShapeDtypeStruct((QK_WIDTH, LANES), F32),
            jax.ShapeDtypeStruct((CHUNK, GMLP_WIDTH), F32),
            jax.ShapeDtypeStruct((1, 1), F32),
            jax.ShapeDtypeStruct((4, HEAD_DIM), F32),
            jax.ShapeDtypeStruct((1, V_DIM), F32),
        ],
        compiler_params=pltpu.CompilerParams(
            dimension_semantics=("arbitrary",), vmem_limit_bytes=VMEM_LIMIT),
        name="adaln_mod",
    )(c, w_ada, b_ada.reshape(1, n), q_norm_g, k_norm_g, b_spatial, rel_bias_table,
      lam_q1, lam_k1, lam_q2, lam_k2, subln_g)


def _mod_row(mod_ref, j, b):
    return mod_ref[j, pl.ds(b, 1), :]


def _ffn_kernel(x_ref, mod_ref, wg_ref, wu_ref, wd_ref, o_ref, wg_s, wu_s, wd_s, *, sub):
    i = pl.program_id(0)

    @pl.when(i < W_STEPS)
    def _():
        _stash_rows(wg_s, wg_ref, i)
        _stash_rows(wu_s, wu_ref, i)
        _stash_rows(wd_s, wd_ref, i)

    @pl.when(i >= W_STEPS)
    def _():
        b = lax.div(i - W_STEPS, SEQ // TM_FFN)
        shift = _mod_row(mod_ref, 3 * sub + 0, b)
        scale = _mod_row(mod_ref, 3 * sub + 1, b)
        gate = _mod_row(mod_ref, 3 * sub + 2, b)
        for r in range(0, TM_FFN, FFN_CHUNK):
            x = x_ref[r:r + FFN_CHUNK, :]
            h = _modulated_rmsnorm(x, shift, scale).astype(BF16)
            g = jnp.dot(h, wg_s[...], preferred_element_type=F32)
            u = jnp.dot(h, wu_s[...], preferred_element_type=F32)
            a = (_silu(g) * u).astype(BF16)
            y = jnp.dot(a, wd_s[...], preferred_element_type=F32)
            o_ref[r:r + FFN_CHUNK, :] = x + (0.5 * gate) * y


def _ffn(x2d, mod, wg, wu, wd, *, sub):
    t = x2d.shape[0]
    tile = lambda i: jnp.maximum(i - W_STEPS, 0)
    return pl.pallas_call(
        functools.partial(_ffn_kernel, sub=sub),
        grid=(W_STEPS + t // TM_FFN,),
        in_specs=[
            pl.BlockSpec((TM_FFN, D_MODEL), lambda i: (tile(i), 0)),
            _const_spec((N_MOD, BATCH, D_MODEL)),
            _weight_rows_spec(wg.shape),
            _weight_rows_spec(wu.shape),
            _weight_rows_spec(wd.shape),
        ],
        out_specs=pl.BlockSpec((TM_FFN, D_MODEL), lambda i: (tile(i), 0)),
        out_shape=jax.ShapeDtypeStruct((t, D_MODEL), F32),
        scratch_shapes=[pltpu.VMEM(wg.shape, BF16), pltpu.VMEM(wu.shape, BF16),
                        pltpu.VMEM(wd.shape, BF16)],
        compiler_params=pltpu.CompilerParams(
            dimension_semantics=("arbitrary",), vmem_limit_bytes=VMEM_LIMIT),
        name=f"ffn{sub}",
    )(x2d, mod, wg, wu, wd)


def _chunk_rmsnorm_rows(xt, gain_rows):
    width, tm = xt.shape
    x3 = xt.reshape(width // HEAD_DIM, HEAD_DIM, tm)
    ms = jnp.mean(x3 * x3, axis=1, keepdims=True)
    xn = (x3 * lax.rsqrt(ms + EPS)).reshape(width, tm)
    return jnp.concatenate(
        [xn[:, j * LANES:(j + 1) * LANES] * gain_rows for j in range(tm // LANES)], axis=1)


def _inproj_tile(x_ref, shift, scale, wqkvt_ref, wrest_ref, qg_ref, kg_ref,
                 lng_ref, lnb_ref, ws_ref, bs_ref,
                 qt_ref, k_ref, vt_ref, ub_ref, ga_ref, gb_ref):
    row = lax.broadcasted_iota(jnp.int32, (CHUNK, CHUNK), 0)
    col = lax.broadcasted_iota(jnp.int32, (CHUNK, CHUNK), 1)
    w_spatial = [jnp.where(row >= col, ws_ref[g], 0.0).astype(BF16)
                 for g in range(N_GROUPS)]
    for c in range(TM_PROJ // TQ):
        rows = slice(c * TQ, (c + 1) * TQ)
        x = x_ref[rows, :]
        h = _modulated_rmsnorm(x, shift, scale).astype(BF16)

        pt = lax.dot_general(wqkvt_ref[...], h, (((1,), (1,)), ((), ())),
                             preferred_element_type=F32)
        qt = _chunk_rmsnorm_rows(pt[0:QK_WIDTH], qg_ref[...])
        qt_ref[0, c] = qt.astype(BF16)
        kt = _chunk_rmsnorm_rows(pt[QK_WIDTH:2 * QK_WIDTH], kg_ref[...])
        k_ref[0, rows, :] = kt.T.astype(BF16)
        vt_ref[0, c] = pt[2 * QK_WIDTH:QKV_COLS].astype(BF16)

        rest = jnp.dot(h, wrest_ref[...], preferred_element_type=F32)
        u = _gelu(rest[:, 0:GMLP_WIDTH])
        gv = _gelu(rest[:, GMLP_WIDTH:2 * GMLP_WIDTH])
        ga_ref[0, rows, :] = jax.nn.sigmoid(
            rest[:, 2 * GMLP_WIDTH:2 * GMLP_WIDTH + D_MODEL]).astype(BF16)
        gb_ref[0, rows, :] = jax.nn.sigmoid(
            rest[:, 2 * GMLP_WIDTH + D_MODEL:REST_COLS]).astype(BF16)

        mu = jnp.mean(gv, axis=-1, keepdims=True)
        cen = gv - mu
        var = jnp.mean(cen * cen, axis=-1, keepdims=True)
        vln = ((cen * lax.rsqrt(var + EPS)) * lng_ref[...] + lnb_ref[...]).astype(BF16)

        for g in range(N_GROUPS):
            cs = slice(g * GROUP_DIM, (g + 1) * GROUP_DIM)
            for n in range(TQ // CHUNK):
                rs = slice(n * CHUNK, (n + 1) * CHUNK)
                f = jnp.dot(w_spatial[g], vln[rs, cs], preferred_element_type=F32) + bs_ref[:, cs]
                ub_ref[0, c * TQ + n * CHUNK:c * TQ + (n + 1) * CHUNK, cs] = (
                    u[rs, cs] * f).astype(BF16)


def _inproj_kernel(x_ref, mod_ref, win_ref, qg_ref, kg_ref, lng_ref, lnb_ref, ws_ref, bs_ref,
                   qt_ref, k_ref, vt_ref, ub_ref, ga_ref, gb_ref, wqkvt_s, wrest_s):
    i = pl.program_id(0)

    @pl.when(i < W_STEPS)
    def _():
        rows = win_ref.shape[0]
        r = pl.multiple_of(i * rows, rows)
        chunk = win_ref[...]
        wqkvt_s[:, pl.ds(r, rows)] = chunk[:, 0:QKV_COLS].T.astype(BF16)
        wrest_s[pl.ds(r, rows), :] = chunk[:, QKV_COLS:QKV_COLS + REST_COLS].astype(BF16)

    @pl.when(i >= W_STEPS)
    def _():
        b = lax.div(i - W_STEPS, SEQ // TM_PROJ)
        _inproj_tile(x_ref, _mod_row(mod_ref, 3, b), _mod_row(mod_ref, 4, b), wqkvt_s, wrest_s,
                     qg_ref, kg_ref, lng_ref, lnb_ref, ws_ref, bs_ref,
                     qt_ref, k_ref, vt_ref, ub_ref, ga_ref, gb_ref)


def _inproj(x3d, mod, w_in, qg_rows, kg_rows, ln_g, ln_b, w_spatial, bs_full):
    nt = SEQ // TM_PROJ
    tile = lambda i: jnp.maximum(i - W_STEPS, 0)
    tok_spec = lambda width: pl.BlockSpec(
        (1, TM_PROJ, width), lambda i: (tile(i) // nt, tile(i) % nt, 0))
    feat_spec = lambda width: pl.BlockSpec(
        (1, TM_PROJ // TQ, width, TQ), lambda i: (tile(i) // nt, tile(i) % nt, 0, 0))
    return pl.pallas_call(
        _inproj_kernel,
        grid=(W_STEPS + BATCH * nt,),
        in_specs=[
            pl.BlockSpec((None, TM_PROJ, D_MODEL), lambda i: (tile(i) // nt, tile(i) % nt, 0)),
            _const_spec((N_MOD, BATCH, D_MODEL)),
            _weight_rows_spec(w_in.shape),
            _const_spec((QK_WIDTH, LANES)),
            _const_spec((QK_WIDTH, LANES)),
            _const_spec((1, GMLP_WIDTH)),
            _const_spec((1, GMLP_WIDTH)),
            _const_spec((N_GROUPS, CHUNK, CHUNK)),
            _const_spec((CHUNK, GMLP_WIDTH)),
        ],
        out_specs=[
            feat_spec(QK_WIDTH), tok_spec(QK_WIDTH), feat_spec(ATTN_WIDTH),
            tok_spec(GMLP_WIDTH), tok_spec(D_MODEL), tok_spec(D_MODEL),
        ],
        out_shape=[
            jax.ShapeDtypeStruct((BATCH, SEQ // TQ, QK_WIDTH, TQ), BF16),
            jax.ShapeDtypeStruct((BATCH, SEQ, QK_WIDTH), BF16),
            jax.ShapeDtypeStruct((BATCH, SEQ // TQ, ATTN_WIDTH, TQ), BF16),
            jax.ShapeDtypeStruct((BATCH, SEQ, GMLP_WIDTH), BF16),
            jax.ShapeDtypeStruct((BATCH, SEQ, D_MODEL), BF16),
            jax.ShapeDtypeStruct((BATCH, SEQ, D_MODEL), BF16),
        ],
        scratch_shapes=[pltpu.VMEM((QKV_COLS, D_MODEL), BF16),
                        pltpu.VMEM((D_MODEL, REST_COLS), BF16)],
        compiler_params=pltpu.CompilerParams(
            dimension_semantics=("arbitrary",), vmem_limit_bytes=VMEM_LIMIT),
        name="inproj",
    )(x3d, mod, w_in, qg_rows, kg_rows, ln_g, ln_b, w_spatial, bs_full)


def _rel_buckets(dist):
    n = np.maximum(dist, 0)
    max_exact = N_BUCKETS // 2
    nf = np.maximum(n, 1).astype(np.float32)
    large = max_exact + (np.log(nf / np.float32(max_exact))
                         / np.float32(math.log(MAX_DISTANCE / max_exact))
                         * np.float32(N_BUCKETS - max_exact)).astype(np.int32)
    large = np.minimum(large, N_BUCKETS - 1)
    return np.where(n < max_exact, n, large).astype(np.int32)


def _attn_kernel(table_ref, bound_ref, lam_ref, qt_ref, k_ref, vt_ref, bkt_ref, subg_ref,
                 o_ref, bias_ref, s_scr):
    b = pl.program_id(1)

    @pl.when(b == 0)
    def _():
        key = lax.broadcasted_iota(jnp.int32, (TQ, TQ), 0)
        qry = lax.broadcasted_iota(jnp.int32, (TQ, TQ), 1)
        for hh in range(HEADS_PER_STEP):
            h = pl.program_id(0) * HEADS_PER_STEP + hh
            far = table_ref[N_BUCKETS - 1, h]
            for t in range(2):
                bkt = bkt_ref[t]
                tile = jnp.zeros((TQ, TQ), F32)
                for bucket in range(N_BUCKETS - 1):
                    tile = jnp.where(bkt == bucket, (table_ref[bucket, h] - far) * LOG2E, tile)
                if t == 0:
                    tile = jnp.where(key <= qry, tile, -jnp.inf)
                bias_ref[hh, t] = jnp.concatenate([tile, tile], axis=1)

    @pl.when(bound_ref[0, 0] <= MAX_SAFE_LOG2)
    def _():
        _attn_program(lam_ref, subg_ref, qt_ref, k_ref, vt_ref, o_ref, bias_ref, s_scr,
                      track_max=False)

    @pl.when(jnp.logical_not(bound_ref[0, 0] <= MAX_SAFE_LOG2))
    def _():
        _attn_program(lam_ref, subg_ref, qt_ref, k_ref, vt_ref, o_ref, bias_ref, s_scr,
                      track_max=True)


def _attn_program(lam_ref, subg_ref, qt_ref, k_ref, vt_ref, o_ref, bias_ref, s_scr, *, track_max):
    lp = lam_ref[...]
    lam = (jnp.exp(jnp.sum(lp[0:1] * lp[1:2], axis=1, keepdims=True))
           - jnp.exp(jnp.sum(lp[2:3] * lp[3:4], axis=1, keepdims=True)) + LAM_INIT)
    subg = subg_ref[...]
    zeros = jnp.zeros((HEAD_DIM, TQ), BF16)
    ones = jnp.ones((SUM_ROWS, TQ), BF16)
    n_strips = 2 * TQ // LANES
    nq = SEQ // TQ
    blocks = [(hh, qi, j) for hh in range(HEADS_PER_STEP) for qi in range(nq) for j in range(qi + 1)]
    qpads = {}

    def head_cols(hh, width):
        return slice(hh * width, (hh + 1) * width)

    def keys(hh, j):
        return k_ref[0, j * TQ:(j + 1) * TQ, head_cols(hh, 2 * HEAD_DIM)]

    def qpad(hh, qi):
        if (hh, qi) not in qpads:
            qt = qt_ref[0, qi, head_cols(hh, 2 * HEAD_DIM), :]
            qpads[hh, qi] = jnp.concatenate(
                [jnp.concatenate([qt[0:HEAD_DIM], zeros], axis=0),
                 jnp.concatenate([zeros, qt[HEAD_DIM:2 * HEAD_DIM]], axis=0)], axis=1)
        return qpads[hh, qi]

    def scores(hh, qi, j):
        return jnp.dot(keys(hh, j), qpad(hh, qi), preferred_element_type=F32)

    def probs_tracked(s, hh, qi, j, m):
        p, alpha, m_out = [], [], []
        for c in range(n_strips):
            sc = s[:, c * LANES:(c + 1) * LANES]
            if j == qi:
                sc = sc + bias_ref[hh, 0, :, c * LANES:(c + 1) * LANES]
            elif j == qi - 1:
                sc = sc + bias_ref[hh, 1, :, c * LANES:(c + 1) * LANES]
            blk_max = jnp.max(sc, axis=0, keepdims=True)
            m_new = blk_max if j == 0 else jnp.maximum(m[c], blk_max)
            p.append(jnp.exp2(sc - m_new).astype(BF16))
            alpha.append(None if j == 0 else jnp.exp2(m[c] - m_new))
            m_out.append(m_new)
        return jnp.concatenate(p, axis=1), alpha, m_out

    half = TQ // 2
    hi_cols = (slice(half, TQ), slice(TQ + half, 2 * TQ))

    def issue_bounded(n):
        hh, qi, j = blocks[n]
        slot = s_scr.at[n % S_SLOTS]
        if j < qi:
            slot[...] = scores(hh, qi, j)
            return
        kb = keys(hh, j)
        qp = qpad(hh, qi)
        slot[0:half, :] = jnp.dot(kb[0:half], qp, preferred_element_type=F32)
        s_hi = jnp.dot(kb[half:TQ], jnp.concatenate([qp[:, c] for c in hi_cols], axis=1),
                       preferred_element_type=F32)
        for t, c in enumerate(hi_cols):
            slot[half:TQ, c] = s_hi[:, t * half:(t + 1) * half]

    def probs_bounded(n):
        hh, qi, j = blocks[n]
        slot = s_scr.at[n % S_SLOTS]
        if j < qi - 1:
            return jnp.exp2(slot[...])
        if j == qi - 1:
            return jnp.exp2(slot[...] + bias_ref[hh, 1])
        lo = jnp.exp2(slot[0:half, :] + bias_ref[hh, 0, 0:half, :])
        z = jnp.zeros((half, half), F32)
        hi = [jnp.exp2(slot[half:TQ, c] + bias_ref[hh, 0, half:TQ, c]) for c in hi_cols]
        return jnp.concatenate([lo, jnp.concatenate([z, hi[0], z, hi[1]], axis=1)], axis=0)

    ahead = QK_AHEAD_TRACKED if track_max else QK_AHEAD_BOUNDED

    issue = (lambda n: scores(*blocks[n])) if track_max else issue_bounded
    pending = [issue(n) for n in range(ahead)]
    m = acc = psum = None
    for n, (hh, qi, j) in enumerate(blocks):
        s = pending.pop(0)
        if n + ahead < len(blocks):
            pending.append(issue(n + ahead))
        vt = vt_ref[0, j, head_cols(hh, V_DIM), :]
        if track_max:
            p, alpha, m = probs_tracked(s, hh, qi, j, m)
            vt_aug = jnp.concatenate([vt, ones], axis=0)
            pv = jnp.dot(vt_aug, p, preferred_element_type=F32)[0:V_DIM + 8]
            if j == 0:
                acc = pv
            else:
                acc = jnp.concatenate(
                    [alpha[c] * acc[:, c * LANES:(c + 1) * LANES] + pv[:, c * LANES:(c + 1) * LANES]
                     for c in range(n_strips)], axis=1)
        else:
            p = probs_bounded(n)
            blk_sum = jnp.sum(p, axis=0, keepdims=True)
            pv = jnp.dot(vt, p.astype(BF16), preferred_element_type=F32)
            acc = pv if j == 0 else acc + pv
            psum = blk_sum if j == 0 else psum + blk_sum
        if j == qi:
            inv = 1.0 / (acc[V_DIM:V_DIM + 1] if track_max else psum)
            ot = (acc[0:V_DIM, 0:TQ] * inv[:, 0:TQ]
                  - acc[0:V_DIM, TQ:2 * TQ] * (lam * inv[:, TQ:2 * TQ]))
            ms = jnp.mean(ot * ot, axis=0, keepdims=True)
            on = (ot * lax.rsqrt(ms + EPS)).T
            o_ref[0, qi * TQ:(qi + 1) * TQ, head_cols(hh, V_DIM)] = (on * subg).astype(BF16)


def _attention(table, bound, lam_rows, qt, k, vt, buckets, sub_gain):
    return pl.pallas_call(
        _attn_kernel,
        grid=(N_HEADS // HEADS_PER_STEP, BATCH),
        in_specs=[
            pl.BlockSpec(memory_space=pltpu.SMEM),
            pl.BlockSpec(memory_space=pltpu.SMEM),
            _const_spec((4, HEAD_DIM)),
            pl.BlockSpec((1, SEQ // TQ, HEADS_PER_STEP * 2 * HEAD_DIM, TQ),
                         lambda g, b: (b, 0, g, 0)),
            pl.BlockSpec((1, SEQ, HEADS_PER_STEP * 2 * HEAD_DIM), lambda g, b: (b, 0, g)),
            pl.BlockSpec((1, SEQ // TQ, HEADS_PER_STEP * V_DIM, TQ), lambda g, b: (b, 0, g, 0)),
            _const_spec((2, TQ, TQ)),
            _const_spec((1, V_DIM)),
        ],
        out_specs=pl.BlockSpec((1, SEQ, HEADS_PER_STEP * V_DIM), lambda g, b: (b, 0, g)),
        out_shape=jax.ShapeDtypeStruct((BATCH, SEQ, ATTN_WIDTH), BF16),
        scratch_shapes=[pltpu.VMEM((HEADS_PER_STEP, 2, TQ, 2 * TQ), F32),
                        pltpu.VMEM((S_SLOTS, TQ, 2 * TQ), F32)],
        compiler_params=pltpu.CompilerParams(
            dimension_semantics=("arbitrary", "arbitrary"),
            vmem_limit_bytes=VMEM_LIMIT),
        name="diff_attn",
    )(table, bound, lam_rows, qt, k, vt, buckets, sub_gain)


def _mix_kernel(x_ref, mod_ref, o_ref, ub_ref, ga_ref, gb_ref, wa_ref, wb_ref, wo_ref,
                out_ref, wa_s, wb_s, wo_s):
    i = pl.program_id(0)

    @pl.when(i < W_STEPS)
    def _():
        _stash_rows(wa_s, wa_ref, i)
        _stash_rows(wb_s, wb_ref, i)
        _stash_rows(wo_s, wo_ref, i)

    @pl.when(i >= W_STEPS)
    def _():
        gate = _mod_row(mod_ref, 5, lax.div(i - W_STEPS, SEQ // TM_MIX))
        for r in range(0, TM_MIX, MIX_CHUNK):
            rows = slice(r, r + MIX_CHUNK)
            ya = jnp.dot(o_ref[rows, :], wa_s[...], preferred_element_type=F32)
            yb = jnp.dot(ub_ref[rows, :], wb_s[...], preferred_element_type=F32)
            merged = (ga_ref[rows, :].astype(F32) * ya
                      + gb_ref[rows, :].astype(F32) * yb).astype(BF16)
            z = jnp.dot(merged, wo_s[...], preferred_element_type=F32)
            out_ref[rows, :] = x_ref[rows, :] + gate * z


def _mix(x2d, mod, o2d, ub2d, ga2d, gb2d, wa, wb, wo):
    t = x2d.shape[0]
    tile = lambda i: jnp.maximum(i - W_STEPS, 0)
    tok = lambda: pl.BlockSpec((TM_MIX, D_MODEL), lambda i: (tile(i), 0))
    return pl.pallas_call(
        _mix_kernel,
        grid=(W_STEPS + t // TM_MIX,),
        in_specs=[
            tok(),
            _const_spec((N_MOD, BATCH, D_MODEL)),
            tok(), tok(), tok(), tok(),
            _weight_rows_spec(wa.shape),
            _weight_rows_spec(wb.shape),
            _weight_rows_spec(wo.shape),
        ],
        out_specs=tok(),
        out_shape=jax.ShapeDtypeStruct((t, D_MODEL), F32),
        scratch_shapes=[pltpu.VMEM(wa.shape, BF16), pltpu.VMEM(wb.shape, BF16),
                        pltpu.VMEM(wo.shape, BF16)],
        compiler_params=pltpu.CompilerParams(
            dimension_semantics=("arbitrary",), vmem_limit_bytes=VMEM_LIMIT),
        name="mix_out",
    )(x2d, mod, o2d, ub2d, ga2d, gb2d, wa, wb, wo)


def kernel(x, c, w_ada, b_ada, w_ffn1_gate, w_ffn1_up, w_ffn1_down, w_in, q_norm_g, k_norm_g, lam_q1, lam_k1, lam_q2, lam_k2, subln_g, rel_bias_table, gmlp_ln_g, gmlp_ln_b, w_spatial, b_spatial, w_a_proj, w_b_proj, w_o, w_ffn2_gate, w_ffn2_up, w_ffn2_down):
    b, s, d = x.shape
    t = b * s

    mod, qg_rows, kg_rows, bs_full, score_bound, lam_rows, sub_gain = _adaln_mod(
        c, w_ada[0], b_ada[0], q_norm_g, k_norm_g, b_spatial[0], rel_bias_table,
        lam_q1, lam_k1, lam_q2, lam_k2, subln_g)

    x1 = _ffn(x.reshape(t, d), mod, w_ffn1_gate[0], w_ffn1_up[0], w_ffn1_down[0], sub=0)

    qt, k, vt, ub, ga, gb = _inproj(
        x1.reshape(b, s, d), mod, w_in[0], qg_rows, kg_rows, gmlp_ln_g, gmlp_ln_b,
        w_spatial[0], bs_full)

    kk = np.arange(TQ)[:, None]
    qq = np.arange(TQ)[None, :]
    buckets = jnp.asarray(np.stack([_rel_buckets(qq - kk), _rel_buckets(TQ + qq - kk)]))
    o = _attention(rel_bias_table, score_bound, lam_rows, qt, k, vt, buckets, sub_gain)

    x2 = _mix(x1, mod, o.reshape(t, ATTN_WIDTH), ub.reshape(t, GMLP_WIDTH),
              ga.reshape(t, d), gb.reshape(t, d), w_a_proj[0], w_b_proj[0], w_o[0])

    x3 = _ffn(x2, mod, w_ffn2_gate[0], w_ffn2_up[0], w_ffn2_down[0], sub=2)
    return x3.reshape(b, s, d)
```

```python
import functools
import math

import numpy as np
import jax
import jax.numpy as jnp
from jax import lax
from jax.experimental import pallas as pl
from jax.experimental.pallas import tpu as pltpu

F32 = jnp.float32
BF16 = jnp.bfloat16

D_MODEL = 1024
BATCH = 8
SEQ = 2048
N_HEADS = 8
HEAD_DIM = 64
V_DIM = 2 * HEAD_DIM
QK_WIDTH = N_HEADS * 2 * HEAD_DIM
ATTN_WIDTH = N_HEADS * V_DIM
N_GROUPS = 8
CHUNK = 128
GMLP_WIDTH = 1024
GROUP_DIM = GMLP_WIDTH // N_GROUPS
N_BUCKETS = 32
MAX_DISTANCE = 128
D_FF = 2816
N_MOD = 9
EPS = 1e-6
LAM_INIT = 0.8 - 0.6 * math.exp(-0.3 * 0)
LOG2E = math.log2(math.e)
QKV_COLS = 2 * QK_WIDTH + ATTN_WIDTH
REST_COLS = 2 * GMLP_WIDTH + 2 * D_MODEL

LANES = 128
VMEM_LIMIT = 56 * 1024 * 1024

FFN_CHUNK = 256
TM_FFN = 1024
TM_PROJ = 512
MIX_CHUNK = 256
TM_MIX = 1024
TQ = 256
SUM_ROWS = 16
W_STEPS = 8
W_STEPS_PROJ = 4
W_STEPS_MIX = 2
MAX_SAFE_LOG2 = 64.0
BF16_SLACK = 1.02
QK_AHEAD_TRACKED = 2
QK_AHEAD_BOUNDED = 3
S_SLOTS = QK_AHEAD_BOUNDED + 1
HEADS_PER_STEP = 2
MOD_ROWS_PER_STEP = 3


def _const_spec(shape):
    nd = len(shape)
    return pl.BlockSpec(shape, lambda *_: (0,) * nd, pipeline_mode=pl.Buffered(1))


def _weight_rows_spec(shape, steps):
    rows, cols = shape
    return pl.BlockSpec((rows // steps, cols), lambda i: (jnp.minimum(i, steps - 1), 0))


def _stash_rows(dst_ref, chunk_ref, i):
    rows = chunk_ref.shape[0]
    dst_ref[pl.ds(pl.multiple_of(i * rows, rows), rows), :] = chunk_ref[...].astype(BF16)


def _silu(x):
    return x * jax.nn.sigmoid(x)


def _gelu(x):
    return 0.5 * x * (1.0 + lax.erf(x * (1.0 / math.sqrt(2.0))))


def _modulated_rmsnorm(x, shift, scale):
    ms = jnp.mean(x * x, axis=-1, keepdims=True)
    return (x * lax.rsqrt(ms + EPS)) * (1.0 + scale) + shift


def _as_column(row):
    n = row.shape[1]
    r = lax.broadcasted_iota(jnp.int32, (n, n), 0)
    c = lax.broadcasted_iota(jnp.int32, (n, n), 1)
    return jnp.sum(jnp.where(r == c, row, 0.0), axis=1, keepdims=True)


def _mod_kernel(c_ref, w_ref, b_ref, qg_ref, kg_ref, bsp_ref, table_ref,
                lq1_ref, lk1_ref, lq2_ref, lk2_ref, subg_ref,
                o_ref, qgr_ref, kgr_ref, bsf_ref, bound_ref, lam_ref, subs_ref):
    a = _silu(c_ref[...]).astype(BF16)
    m = jnp.dot(a, w_ref[...].astype(BF16), preferred_element_type=F32) + b_ref[...]
    for r in range(MOD_ROWS_PER_STEP):
        o_ref[r] = m[:, r * D_MODEL:(r + 1) * D_MODEL]

    @pl.when(pl.program_id(0) == 0)
    def _():
        qg = qg_ref[...] * (HEAD_DIM ** -0.5 * LOG2E)
        kg = kg_ref[...]
        reps = QK_WIDTH // HEAD_DIM
        qgr_ref[...] = jnp.concatenate(
            [jnp.broadcast_to(_as_column(qg), (HEAD_DIM, LANES))] * reps, axis=0)
        kgr_ref[...] = jnp.concatenate(
            [jnp.broadcast_to(_as_column(kg), (HEAD_DIM, LANES))] * reps, axis=0)
        bsf_ref[...] = jnp.concatenate(
            [jnp.broadcast_to(_as_column(bsp_ref[g:g + 1, :]), (CHUNK, GROUP_DIM))
             for g in range(N_GROUPS)], axis=1)
        table = table_ref[...]
        rel = jnp.abs(table - table[N_BUCKETS - 1:N_BUCKETS, :])
        bound_ref[...] = (
            HEAD_DIM * BF16_SLACK * jnp.max(jnp.abs(qg), axis=(0, 1), keepdims=True)
            * jnp.max(jnp.abs(kg), axis=(0, 1), keepdims=True)
            + LOG2E * jnp.max(rel, axis=(0, 1), keepdims=True))
        lam_ref[...] = jnp.concatenate(
            [lq1_ref[...], lk1_ref[...], lq2_ref[...], lk2_ref[...]], axis=0)
        subs_ref[...] = subg_ref[...] * (1.0 - LAM_INIT)


def _adaln_mod(c, w_ada, b_ada, q_norm_g, k_norm_g, b_spatial, rel_bias_table,
               lam_q1, lam_k1, lam_q2, lam_k2, subln_g):
    n = w_ada.shape[1]
    whole = lambda shape: pl.BlockSpec(shape, lambda j: (0,) * len(shape))
    return pl.pallas_call(
        _mod_kernel,
        grid=(n // (MOD_ROWS_PER_STEP * D_MODEL),),
        in_specs=[
            whole((BATCH, D_MODEL)),
            pl.BlockSpec((D_MODEL, MOD_ROWS_PER_STEP * D_MODEL), lambda j: (0, j)),
            pl.BlockSpec((1, MOD_ROWS_PER_STEP * D_MODEL), lambda j: (0, j)),
            whole((1, HEAD_DIM)), whole((1, HEAD_DIM)),
            whole((N_GROUPS, CHUNK)), whole((N_BUCKETS, N_HEADS)),
            whole((1, HEAD_DIM)), whole((1, HEAD_DIM)), whole((1, HEAD_DIM)), whole((1, HEAD_DIM)),
            whole((1, V_DIM)),
        ],
        out_specs=[
            pl.BlockSpec((MOD_ROWS_PER_STEP, BATCH, D_MODEL), lambda j: (j, 0, 0)),
            whole((QK_WIDTH, LANES)), whole((QK_WIDTH, LANES)),
            whole((CHUNK, GMLP_WIDTH)), whole((1, 1)), whole((4, HEAD_DIM)), whole((1, V_DIM)),
        ],
        out_shape=[
            jax.ShapeDtypeStruct((n // D_MODEL, BATCH, D_MODEL), F32),
            jax.ShapeDtypeStruct((QK_WIDTH, LANES), F32),
            jax.ShapeDtypeStruct((QK_WIDTH, LANES), F32),
            jax.ShapeDtypeStruct((CHUNK, GMLP_WIDTH), F32),
            jax.ShapeDtypeStruct((1, 1), F32),
            jax.ShapeDtypeStruct((4, HEAD_DIM), F32),
            jax.ShapeDtypeStruct((1, V_DIM), F32),
        ],
        compiler_params=pltpu.CompilerParams(
            dimension_semantics=("arbitrary",), vmem_limit_bytes=VMEM_LIMIT),
        name="adaln_mod",
    )(c, w_ada, b_ada.reshape(1, n), q_norm_g, k_norm_g, b_spatial, rel_bias_table,
      lam_q1, lam_k1, lam_q2, lam_k2, subln_g)


def _mod_row(mod_ref, j, b):
    return mod_ref[j, pl.ds(b, 1), :]


def _ffn_kernel(x_ref, mod_ref, wg_ref, wu_ref, wd_ref, o_ref, wg_s, wu_s, wd_s, *, sub):
    i = pl.program_id(0)

    @pl.when(i < W_STEPS)
    def _():
        _stash_rows(wg_s, wg_ref, i)
        _stash_rows(wu_s, wu_ref, i)
        _stash_rows(wd_s, wd_ref, i)

    @pl.when(i >= W_STEPS)
    def _():
        b = lax.div(i - W_STEPS, SEQ // TM_FFN)
        shift = _mod_row(mod_ref, 3 * sub + 0, b)
        scale = _mod_row(mod_ref, 3 * sub + 1, b)
        gate = _mod_row(mod_ref, 3 * sub + 2, b)
        for r in range(0, TM_FFN, FFN_CHUNK):
            x = x_ref[r:r + FFN_CHUNK, :]
            h = _modulated_rmsnorm(x, shift, scale).astype(BF16)
            g = jnp.dot(h, wg_s[...], preferred_element_type=F32)
            u = jnp.dot(h, wu_s[...], preferred_element_type=F32)
            a = (_silu(g) * u).astype(BF16)
            y = jnp.dot(a, wd_s[...], preferred_element_type=F32)
            o_ref[r:r + FFN_CHUNK, :] = x + (0.5 * gate) * y


def _ffn(x2d, mod, wg, wu, wd, *, sub):
    t = x2d.shape[0]
    tile = lambda i: jnp.maximum(i - W_STEPS, 0)
    return pl.pallas_call(
        functools.partial(_ffn_kernel, sub=sub),
        grid=(W_STEPS + t // TM_FFN,),
        in_specs=[
            pl.BlockSpec((TM_FFN, D_MODEL), lambda i: (tile(i), 0)),
            _const_spec((N_MOD, BATCH, D_MODEL)),
            _weight_rows_spec(wg.shape, W_STEPS),
            _weight_rows_spec(wu.shape, W_STEPS),
            _weight_rows_spec(wd.shape, W_STEPS),
        ],
        out_specs=pl.BlockSpec((TM_FFN, D_MODEL), lambda i: (tile(i), 0)),
        out_shape=jax.ShapeDtypeStruct((t, D_MODEL), F32),
        scratch_shapes=[pltpu.VMEM(wg.shape, BF16), pltpu.VMEM(wu.shape, BF16),
                        pltpu.VMEM(wd.shape, BF16)],
        compiler_params=pltpu.CompilerParams(
            dimension_semantics=("arbitrary",), vmem_limit_bytes=VMEM_LIMIT),
        name=f"ffn{sub}",
    )(x2d, mod, wg, wu, wd)


def _chunk_rmsnorm_rows(xt, gain_rows):
    width, tm = xt.shape
    x3 = xt.reshape(width // HEAD_DIM, HEAD_DIM, tm)
    ms = jnp.mean(x3 * x3, axis=1, keepdims=True)
    xn = (x3 * lax.rsqrt(ms + EPS)).reshape(width, tm)
    return jnp.concatenate(
        [xn[:, j * LANES:(j + 1) * LANES] * gain_rows for j in range(tm // LANES)], axis=1)


def _inproj_tile(x_ref, shift, scale, wqkvt_ref, wrest_ref, qg_ref, kg_ref,
                 lng_ref, lnb_ref, ws_ref, bs_ref,
                 qt_ref, k_ref, vt_ref, ub_ref, ga_ref, gb_ref):
    row = lax.broadcasted_iota(jnp.int32, (CHUNK, CHUNK), 0)
    col = lax.broadcasted_iota(jnp.int32, (CHUNK, CHUNK), 1)
    w_spatial = [jnp.where(row >= col, ws_ref[g], 0.0).astype(BF16)
                 for g in range(N_GROUPS)]
    for c in range(TM_PROJ // TQ):
        rows = slice(c * TQ, (c + 1) * TQ)
        x = x_ref[rows, :]
        h = _modulated_rmsnorm(x, shift, scale).astype(BF16)

        pt = lax.dot_general(wqkvt_ref[...], h, (((1,), (1,)), ((), ())),
                             preferred_element_type=F32)
        qt = _chunk_rmsnorm_rows(pt[0:QK_WIDTH], qg_ref[...])
        qt_ref[0, c] = qt.astype(BF16)
        kt = _chunk_rmsnorm_rows(pt[QK_WIDTH:2 * QK_WIDTH], kg_ref[...])
        k_ref[0, rows, :] = kt.T.astype(BF16)
        vt_ref[0, c] = pt[2 * QK_WIDTH:QKV_COLS].astype(BF16)

        rest = jnp.dot(h, wrest_ref[...], preferred_element_type=F32)
        u = _gelu(rest[:, 0:GMLP_WIDTH])
        gv = _gelu(rest[:, GMLP_WIDTH:2 * GMLP_WIDTH])
        ga_ref[0, rows, :] = jax.nn.sigmoid(
            rest[:, 2 * GMLP_WIDTH:2 * GMLP_WIDTH + D_MODEL]).astype(BF16)
        gb_ref[0, rows, :] = jax.nn.sigmoid(
            rest[:, 2 * GMLP_WIDTH + D_MODEL:REST_COLS]).astype(BF16)

        mu = jnp.mean(gv, axis=-1, keepdims=True)
        cen = gv - mu
        var = jnp.mean(cen * cen, axis=-1, keepdims=True)
        vln = ((cen * lax.rsqrt(var + EPS)) * lng_ref[...] + lnb_ref[...]).astype(BF16)

        for g in range(N_GROUPS):
            cs = slice(g * GROUP_DIM, (g + 1) * GROUP_DIM)
            for n in range(TQ // CHUNK):
                rs = slice(n * CHUNK, (n + 1) * CHUNK)
                f = jnp.dot(w_spatial[g], vln[rs, cs], preferred_element_type=F32) + bs_ref[:, cs]
                ub_ref[0, c * TQ + n * CHUNK:c * TQ + (n + 1) * CHUNK, cs] = (
                    u[rs, cs] * f).astype(BF16)


def _inproj_kernel(x_ref, mod_ref, win_ref, qg_ref, kg_ref, lng_ref, lnb_ref, ws_ref, bs_ref,
                   qt_ref, k_ref, vt_ref, ub_ref, ga_ref, gb_ref, wqkvt_s, wrest_s):
    i = pl.program_id(0)

    @pl.when(i < W_STEPS_PROJ)
    def _():
        rows = win_ref.shape[0]
        r = pl.multiple_of(i * rows, rows)
        chunk = win_ref[...]
        wqkvt_s[:, pl.ds(r, rows)] = chunk[:, 0:QKV_COLS].T.astype(BF16)
        wrest_s[pl.ds(r, rows), :] = chunk[:, QKV_COLS:QKV_COLS + REST_COLS].astype(BF16)

    @pl.when(i >= W_STEPS_PROJ)
    def _():
        b = lax.div(i - W_STEPS_PROJ, SEQ // TM_PROJ)
        _inproj_tile(x_ref, _mod_row(mod_ref, 3, b), _mod_row(mod_ref, 4, b), wqkvt_s, wrest_s,
                     qg_ref, kg_ref, lng_ref, lnb_ref, ws_ref, bs_ref,
                     qt_ref, k_ref, vt_ref, ub_ref, ga_ref, gb_ref)


def _inproj(x3d, mod, w_in, qg_rows, kg_rows, ln_g, ln_b, w_spatial, bs_full):
    nt = SEQ // TM_PROJ
    tile = lambda i: jnp.maximum(i - W_STEPS_PROJ, 0)
    tok_spec = lambda width: pl.BlockSpec(
        (1, TM_PROJ, width), lambda i: (tile(i) // nt, tile(i) % nt, 0))
    feat_spec = lambda width: pl.BlockSpec(
        (1, TM_PROJ // TQ, width, TQ), lambda i: (tile(i) // nt, tile(i) % nt, 0, 0))
    return pl.pallas_call(
        _inproj_kernel,
        grid=(W_STEPS_PROJ + BATCH * nt,),
        in_specs=[
            pl.BlockSpec((None, TM_PROJ, D_MODEL), lambda i: (tile(i) // nt, tile(i) % nt, 0)),
            _const_spec((N_MOD, BATCH, D_MODEL)),
            _weight_rows_spec(w_in.shape, W_STEPS_PROJ),
            _const_spec((QK_WIDTH, LANES)),
            _const_spec((QK_WIDTH, LANES)),
            _const_spec((1, GMLP_WIDTH)),
            _const_spec((1, GMLP_WIDTH)),
            _const_spec((N_GROUPS, CHUNK, CHUNK)),
            _const_spec((CHUNK, GMLP_WIDTH)),
        ],
        out_specs=[
            feat_spec(QK_WIDTH), tok_spec(QK_WIDTH), feat_spec(ATTN_WIDTH),
            tok_spec(GMLP_WIDTH), tok_spec(D_MODEL), tok_spec(D_MODEL),
        ],
        out_shape=[
            jax.ShapeDtypeStruct((BATCH, SEQ // TQ, QK_WIDTH, TQ), BF16),
            jax.ShapeDtypeStruct((BATCH, SEQ, QK_WIDTH), BF16),
            jax.ShapeDtypeStruct((BATCH, SEQ // TQ, ATTN_WIDTH, TQ), BF16),
            jax.ShapeDtypeStruct((BATCH, SEQ, GMLP_WIDTH), BF16),
            jax.ShapeDtypeStruct((BATCH, SEQ, D_MODEL), BF16),
            jax.ShapeDtypeStruct((BATCH, SEQ, D_MODEL), BF16),
        ],
        scratch_shapes=[pltpu.VMEM((QKV_COLS, D_MODEL), BF16),
                        pltpu.VMEM((D_MODEL, REST_COLS), BF16)],
        compiler_params=pltpu.CompilerParams(
            dimension_semantics=("arbitrary",), vmem_limit_bytes=VMEM_LIMIT),
        name="inproj",
    )(x3d, mod, w_in, qg_rows, kg_rows, ln_g, ln_b, w_spatial, bs_full)


def _rel_buckets(dist):
    n = np.maximum(dist, 0)
    max_exact = N_BUCKETS // 2
    nf = np.maximum(n, 1).astype(np.float32)
    large = max_exact + (np.log(nf / np.float32(max_exact))
                         / np.float32(math.log(MAX_DISTANCE / max_exact))
                         * np.float32(N_BUCKETS - max_exact)).astype(np.int32)
    large = np.minimum(large, N_BUCKETS - 1)
    return np.where(n < max_exact, n, large).astype(np.int32)


def _attn_kernel(table_ref, bound_ref, lam_ref, qt_ref, k_ref, vt_ref, bkt_ref, subg_ref,
                 o_ref, bias_ref, s_scr):
    b = pl.program_id(1)

    @pl.when(b == 0)
    def _():
        heads = [pl.program_id(0) * HEADS_PER_STEP + hh for hh in range(HEADS_PER_STEP)]

        def lookup(bkt):
            tiles = [jnp.zeros(bkt.shape, F32)] * HEADS_PER_STEP
            for bucket in range(N_BUCKETS - 1):
                hit = bkt == bucket
                tiles = [jnp.where(hit, (table_ref[bucket, h] - table_ref[N_BUCKETS - 1, h]) * LOG2E, t)
                         for h, t in zip(heads, tiles)]
            return tiles

        key = lax.broadcasted_iota(jnp.int32, (TQ, TQ), 0)
        qry = lax.broadcasted_iota(jnp.int32, (TQ, TQ), 1)
        half = TQ // 2
        z = jnp.zeros((half, half), F32)
        diag = lookup(bkt_ref[0])
        quad = lookup(bkt_ref[1, half:TQ, 0:half])
        for hh in range(HEADS_PER_STEP):
            tile = jnp.where(key <= qry, diag[hh], -jnp.inf)
            bias_ref[hh, 0] = jnp.concatenate([tile, tile], axis=1)
            tile = jnp.concatenate([jnp.concatenate([z, z], axis=1),
                                    jnp.concatenate([quad[hh], z], axis=1)], axis=0)
            bias_ref[hh, 1] = jnp.concatenate([tile, tile], axis=1)

    @pl.when(bound_ref[0, 0] <= MAX_SAFE_LOG2)
    def _():
        _attn_program(lam_ref, subg_ref, qt_ref, k_ref, vt_ref, o_ref, bias_ref, s_scr,
                      track_max=False)

    @pl.when(jnp.logical_not(bound_ref[0, 0] <= MAX_SAFE_LOG2))
    def _():
        _attn_program(lam_ref, subg_ref, qt_ref, k_ref, vt_ref, o_ref, bias_ref, s_scr,
                      track_max=True)


def _attn_program(lam_ref, subg_ref, qt_ref, k_ref, vt_ref, o_ref, bias_ref, s_scr, *, track_max):
    lp = lam_ref[...]
    lam = (jnp.exp(jnp.sum(lp[0:1] * lp[1:2], axis=1, keepdims=True))
           - jnp.exp(jnp.sum(lp[2:3] * lp[3:4], axis=1, keepdims=True)) + LAM_INIT)
    subg = subg_ref[...]
    zeros = jnp.zeros((HEAD_DIM, TQ), BF16)
    ones = jnp.ones((SUM_ROWS, TQ), BF16)
    n_strips = 2 * TQ // LANES
    nq = SEQ // TQ
    blocks = [(hh, qi, j) for hh in range(HEADS_PER_STEP) for qi in range(nq) for j in range(qi + 1)]
    qpads = {}

    def head_cols(hh, width):
        return slice(hh * width, (hh + 1) * width)

    def keys(hh, j):
        return k_ref[0, j * TQ:(j + 1) * TQ, head_cols(hh, 2 * HEAD_DIM)]

    def qpad(hh, qi):
        if (hh, qi) not in qpads:
            qt = qt_ref[0, qi, head_cols(hh, 2 * HEAD_DIM), :]
            qpads[hh, qi] = jnp.concatenate(
                [jnp.concatenate([qt[0:HEAD_DIM], zeros], axis=0),
                 jnp.concatenate([zeros, qt[HEAD_DIM:2 * HEAD_DIM]], axis=0)], axis=1)
        return qpads[hh, qi]

    def scores(hh, qi, j):
        return jnp.dot(keys(hh, j), qpad(hh, qi), preferred_element_type=F32)

    def probs_tracked(s, hh, qi, j, m):
        p, alpha, m_out = [], [], []
        for c in range(n_strips):
            sc = s[:, c * LANES:(c + 1) * LANES]
            if j == qi:
                sc = sc + bias_ref[hh, 0, :, c * LANES:(c + 1) * LANES]
            elif j == qi - 1:
                sc = sc + bias_ref[hh, 1, :, c * LANES:(c + 1) * LANES]
            blk_max = jnp.max(sc, axis=0, keepdims=True)
            m_new = blk_max if j == 0 else jnp.maximum(m[c], blk_max)
            p.append(jnp.exp2(sc - m_new).astype(BF16))
            alpha.append(None if j == 0 else jnp.exp2(m[c] - m_new))
            m_out.append(m_new)
        return jnp.concatenate(p, axis=1), alpha, m_out

    half = TQ // 2
    hi_cols = (slice(half, TQ), slice(TQ + half, 2 * TQ))

    def issue_bounded(n):
        hh, qi, j = blocks[n]
        slot = s_scr.at[n % S_SLOTS]
        if j < qi:
            slot[...] = scores(hh, qi, j)
            return
        kb = keys(hh, j)
        qp = qpad(hh, qi)
        slot[0:half, :] = jnp.dot(kb[0:half], qp, preferred_element_type=F32)
        s_hi = jnp.dot(kb[half:TQ], jnp.concatenate([qp[:, c] for c in hi_cols], axis=1),
                       preferred_element_type=F32)
        for t, c in enumerate(hi_cols):
            slot[half:TQ, c] = s_hi[:, t * half:(t + 1) * half]

    def probs_bounded(n):
        hh, qi, j = blocks[n]
        slot = s_scr.at[n % S_SLOTS]
        if j < qi - 1:
            return jnp.exp2(slot[...])
        if j == qi - 1:
            return jnp.exp2(slot[...] + bias_ref[hh, 1])
        lo = jnp.exp2(slot[0:half, :] + bias_ref[hh, 0, 0:half, :])
        z = jnp.zeros((half, half), F32)
        hi = [jnp.exp2(slot[half:TQ, c] + bias_ref[hh, 0, half:TQ, c]) for c in hi_cols]
        return jnp.concatenate([lo, jnp.concatenate([z, hi[0], z, hi[1]], axis=1)], axis=0)

    ahead = QK_AHEAD_TRACKED if track_max else QK_AHEAD_BOUNDED

    issue = (lambda n: scores(*blocks[n])) if track_max else issue_bounded
    pending = [issue(n) for n in range(ahead)]
    m = acc = psum = None
    for n, (hh, qi, j) in enumerate(blocks):
        s = pending.pop(0)
        if n + ahead < len(blocks):
            pending.append(issue(n + ahead))
        vt = vt_ref[0, j, head_cols(hh, V_DIM), :]
        if track_max:
            p, alpha, m = probs_tracked(s, hh, qi, j, m)
            vt_aug = jnp.concatenate([vt, ones], axis=0)
            pv = jnp.dot(vt_aug, p, preferred_element_type=F32)[0:V_DIM + 8]
            if j == 0:
                acc = pv
            else:
                acc = jnp.concatenate(
                    [alpha[c] * acc[:, c * LANES:(c + 1) * LANES] + pv[:, c * LANES:(c + 1) * LANES]
                     for c in range(n_strips)], axis=1)
        else:
            p = probs_bounded(n)
            blk_sum = jnp.sum(p, axis=0, keepdims=True)
            pv = jnp.dot(vt, p.astype(BF16), preferred_element_type=F32)
            acc = pv if j == 0 else acc + pv
            psum = blk_sum if j == 0 else psum + blk_sum
        if j == qi:
            inv = 1.0 / (acc[V_DIM:V_DIM + 1] if track_max else psum)
            ot = (acc[0:V_DIM, 0:TQ] * inv[:, 0:TQ]
                  - acc[0:V_DIM, TQ:2 * TQ] * (lam * inv[:, TQ:2 * TQ]))
            ms = jnp.mean(ot * ot, axis=0, keepdims=True)
            on = (ot * lax.rsqrt(ms + EPS)).T
            o_ref[0, qi * TQ:(qi + 1) * TQ, head_cols(hh, V_DIM)] = (on * subg).astype(BF16)


def _attention(table, bound, lam_rows, qt, k, vt, buckets, sub_gain):
    return pl.pallas_call(
        _attn_kernel,
        grid=(N_HEADS // HEADS_PER_STEP, BATCH),
        in_specs=[
            pl.BlockSpec(memory_space=pltpu.SMEM),
            pl.BlockSpec(memory_space=pltpu.SMEM),
            _const_spec((4, HEAD_DIM)),
            pl.BlockSpec((1, SEQ // TQ, HEADS_PER_STEP * 2 * HEAD_DIM, TQ),
                         lambda g, b: (b, 0, g, 0)),
            pl.BlockSpec((1, SEQ, HEADS_PER_STEP * 2 * HEAD_DIM), lambda g, b: (b, 0, g)),
            pl.BlockSpec((1, SEQ // TQ, HEADS_PER_STEP * V_DIM, TQ), lambda g, b: (b, 0, g, 0)),
            _const_spec((2, TQ, TQ)),
            _const_spec((1, V_DIM)),
        ],
        out_specs=pl.BlockSpec((1, SEQ, HEADS_PER_STEP * V_DIM), lambda g, b: (b, 0, g)),
        out_shape=jax.ShapeDtypeStruct((BATCH, SEQ, ATTN_WIDTH), BF16),
        scratch_shapes=[pltpu.VMEM((HEADS_PER_STEP, 2, TQ, 2 * TQ), F32),
                        pltpu.VMEM((S_SLOTS, TQ, 2 * TQ), F32)],
        compiler_params=pltpu.CompilerParams(
            dimension_semantics=("arbitrary", "arbitrary"),
            vmem_limit_bytes=VMEM_LIMIT),
        name="diff_attn",
    )(table, bound, lam_rows, qt, k, vt, buckets, sub_gain)


def _mix_kernel(x_ref, mod_ref, o_ref, ub_ref, ga_ref, gb_ref, wa_ref, wb_ref, wo_ref,
                out_ref, wa_s, wb_s, wo_s):
    i = pl.program_id(0)

    @pl.when(i < W_STEPS_MIX)
    def _():
        _stash_rows(wa_s, wa_ref, i)
        _stash_rows(wb_s, wb_ref, i)
        _stash_rows(wo_s, wo_ref, i)

    @pl.when(i >= W_STEPS_MIX)
    def _():
        gate = _mod_row(mod_ref, 5, lax.div(i - W_STEPS_MIX, SEQ // TM_MIX))
        for r in range(0, TM_MIX, MIX_CHUNK):
            rows = slice(r, r + MIX_CHUNK)
            ya = jnp.dot(o_ref[rows, :], wa_s[...], preferred_element_type=F32)
            yb = jnp.dot(ub_ref[rows, :], wb_s[...], preferred_element_type=F32)
            merged = (ga_ref[rows, :].astype(F32) * ya
                      + gb_ref[rows, :].astype(F32) * yb).astype(BF16)
            z = jnp.dot(merged, wo_s[...], preferred_element_type=F32)
            out_ref[rows, :] = x_ref[rows, :] + gate * z


def _mix(x2d, mod, o2d, ub2d, ga2d, gb2d, wa, wb, wo):
    t = x2d.shape[0]
    tile = lambda i: jnp.maximum(i - W_STEPS_MIX, 0)
    tok = lambda: pl.BlockSpec((TM_MIX, D_MODEL), lambda i: (tile(i), 0))
    return pl.pallas_call(
        _mix_kernel,
        grid=(W_STEPS_MIX + t // TM_MIX,),
        in_specs=[
            tok(),
            _const_spec((N_MOD, BATCH, D_MODEL)),
            tok(), tok(), tok(), tok(),
            _weight_rows_spec(wa.shape, W_STEPS_MIX),
            _weight_rows_spec(wb.shape, W_STEPS_MIX),
            _weight_rows_spec(wo.shape, W_STEPS_MIX),
        ],
        out_specs=tok(),
        out_shape=jax.ShapeDtypeStruct((t, D_MODEL), F32),
        scratch_shapes=[pltpu.VMEM(wa.shape, BF16), pltpu.VMEM(wb.shape, BF16),
                        pltpu.VMEM(wo.shape, BF16)],
        compiler_params=pltpu.CompilerParams(
            dimension_semantics=("arbitrary",), vmem_limit_bytes=VMEM_LIMIT),
        name="mix_out",
    )(x2d, mod, o2d, ub2d, ga2d, gb2d, wa, wb, wo)


def kernel(x, c, w_ada, b_ada, w_ffn1_gate, w_ffn1_up, w_ffn1_down, w_in, q_norm_g, k_norm_g, lam_q1, lam_k1, lam_q2, lam_k2, subln_g, rel_bias_table, gmlp_ln_g, gmlp_ln_b, w_spatial, b_spatial, w_a_proj, w_b_proj, w_o, w_ffn2_gate, w_ffn2_up, w_ffn2_down):
    b, s, d = x.shape
    t = b * s

    mod, qg_rows, kg_rows, bs_full, score_bound, lam_rows, sub_gain = _adaln_mod(
        c, w_ada[0], b_ada[0], q_norm_g, k_norm_g, b_spatial[0], rel_bias_table,
        lam_q1, lam_k1, lam_q2, lam_k2, subln_g)

    x1 = _ffn(x.reshape(t, d), mod, w_ffn1_gate[0], w_ffn1_up[0], w_ffn1_down[0], sub=0)

    qt, k, vt, ub, ga, gb = _inproj(
        x1.reshape(b, s, d), mod, w_in[0], qg_rows, kg_rows, gmlp_ln_g, gmlp_ln_b,
        w_spatial[0], bs_full)

    kk = np.arange(TQ)[:, None]
    qq = np.arange(TQ)[None, :]
    near = _rel_buckets(TQ + qq - kk)
    far_only = near == N_BUCKETS - 1
    far_only[TQ // 2:, :TQ // 2] = True
    assert far_only.all(), "previous-block bias must be confined to one quadrant"
    buckets = jnp.asarray(np.stack([_rel_buckets(qq - kk), near]))
    o = _attention(rel_bias_table, score_bound, lam_rows, qt, k, vt, buckets, sub_gain)

    x2 = _mix(x1, mod, o.reshape(t, ATTN_WIDTH), ub.reshape(t, GMLP_WIDTH),
              ga.reshape(t, d), gb.reshape(t, d), w_a_proj[0], w_b_proj[0], w_o[0])

    x3 = _ffn(x2, mod, w_ffn2_gate[0], w_ffn2_up[0], w_ffn2_down[0], sub=2)
    return x3.reshape(b, s, d)
```

```python
import functools
import math

import numpy as np
import jax
import jax.numpy as jnp
from jax import lax
from jax.experimental import pallas as pl
from jax.experimental.pallas import tpu as pltpu

F32 = jnp.float32
BF16 = jnp.bfloat16

D_MODEL = 1024
BATCH = 8
SEQ = 2048
N_HEADS = 8
HEAD_DIM = 64
V_DIM = 2 * HEAD_DIM
QK_WIDTH = N_HEADS * 2 * HEAD_DIM
ATTN_WIDTH = N_HEADS * V_DIM
N_GROUPS = 8
CHUNK = 128
GMLP_WIDTH = 1024
GROUP_DIM = GMLP_WIDTH // N_GROUPS
N_BUCKETS = 32
MAX_DISTANCE = 128
D_FF = 2816
N_MOD = 9
EPS = 1e-6
LAM_INIT = 0.8 - 0.6 * math.exp(-0.3 * 0)
LOG2E = math.log2(math.e)
QKV_COLS = 2 * QK_WIDTH + ATTN_WIDTH
REST_COLS = 2 * GMLP_WIDTH + 2 * D_MODEL

LANES = 128
VMEM_LIMIT = 56 * 1024 * 1024

FFN_CHUNK = 256
TM_FFN = 1024
TM_PROJ = 512
MIX_CHUNK = 256
TM_MIX = 1024
TQ = 256
SUM_ROWS = 16
W_STEPS = 4
W_STEPS_PROJ = 4
W_STEPS_MIX = 2
MAX_SAFE_LOG2 = 64.0
BF16_SLACK = 1.02
QK_AHEAD_TRACKED = 2
QK_AHEAD_BOUNDED = 3
S_SLOTS = QK_AHEAD_BOUNDED + 1
HEADS_PER_STEP = 2
MOD_ROWS_PER_STEP = 3


def _const_spec(shape):
    nd = len(shape)
    return pl.BlockSpec(shape, lambda *_: (0,) * nd, pipeline_mode=pl.Buffered(1))


def _weight_rows_spec(shape, steps):
    rows, cols = shape
    return pl.BlockSpec((rows // steps, cols), lambda i: (jnp.minimum(i, steps - 1), 0))


def _stash_rows(dst_ref, chunk_ref, i):
    rows = chunk_ref.shape[0]
    dst_ref[pl.ds(pl.multiple_of(i * rows, rows), rows), :] = chunk_ref[...].astype(BF16)


def _silu(x):
    return x * jax.nn.sigmoid(x)


def _gelu(x):
    return 0.5 * x * (1.0 + lax.erf(x * (1.0 / math.sqrt(2.0))))


def _modulated_rmsnorm(x, shift, scale):
    ms = jnp.mean(x * x, axis=-1, keepdims=True)
    return (x * lax.rsqrt(ms + EPS)) * (1.0 + scale) + shift


def _as_column(row):
    n = row.shape[1]
    r = lax.broadcasted_iota(jnp.int32, (n, n), 0)
    c = lax.broadcasted_iota(jnp.int32, (n, n), 1)
    return jnp.sum(jnp.where(r == c, row, 0.0), axis=1, keepdims=True)


def _mod_kernel(c_ref, w_ref, b_ref, qg_ref, kg_ref, bsp_ref, table_ref,
                lq1_ref, lk1_ref, lq2_ref, lk2_ref, subg_ref,
                o_ref, qgr_ref, kgr_ref, bsf_ref, bound_ref, lam_ref, subs_ref):
    a = _silu(c_ref[...]).astype(BF16)
    m = jnp.dot(a, w_ref[...].astype(BF16), preferred_element_type=F32) + b_ref[...]
    for r in range(MOD_ROWS_PER_STEP):
        o_ref[r] = m[:, r * D_MODEL:(r + 1) * D_MODEL]

    @pl.when(pl.program_id(0) == 0)
    def _():
        qg = qg_ref[...] * (HEAD_DIM ** -0.5 * LOG2E)
        kg = kg_ref[...]
        reps = QK_WIDTH // HEAD_DIM
        qgr_ref[...] = jnp.concatenate(
            [jnp.broadcast_to(_as_column(qg), (HEAD_DIM, LANES))] * reps, axis=0)
        kgr_ref[...] = jnp.concatenate(
            [jnp.broadcast_to(_as_column(kg), (HEAD_DIM, LANES))] * reps, axis=0)
        bsf_ref[...] = jnp.concatenate(
            [jnp.broadcast_to(_as_column(bsp_ref[g:g + 1, :]), (CHUNK, GROUP_DIM))
             for g in range(N_GROUPS)], axis=1)
        table = table_ref[...]
        rel = jnp.abs(table - table[N_BUCKETS - 1:N_BUCKETS, :])
        bound_ref[...] = (
            HEAD_DIM * BF16_SLACK * jnp.max(jnp.abs(qg), axis=(0, 1), keepdims=True)
            * jnp.max(jnp.abs(kg), axis=(0, 1), keepdims=True)
            + LOG2E * jnp.max(rel, axis=(0, 1), keepdims=True))
        lam_ref[...] = jnp.concatenate(
            [lq1_ref[...], lk1_ref[...], lq2_ref[...], lk2_ref[...]], axis=0)
        subs_ref[...] = subg_ref[...] * (1.0 - LAM_INIT)


def _adaln_mod(c, w_ada, b_ada, q_norm_g, k_norm_g, b_spatial, rel_bias_table,
               lam_q1, lam_k1, lam_q2, lam_k2, subln_g):
    n = w_ada.shape[1]
    whole = lambda shape: pl.BlockSpec(shape, lambda j: (0,) * len(shape))
    return pl.pallas_call(
        _mod_kernel,
        grid=(n // (MOD_ROWS_PER_STEP * D_MODEL),),
        in_specs=[
            whole((BATCH, D_MODEL)),
            pl.BlockSpec((D_MODEL, MOD_ROWS_PER_STEP * D_MODEL), lambda j: (0, j)),
            pl.BlockSpec((1, MOD_ROWS_PER_STEP * D_MODEL), lambda j: (0, j)),
            whole((1, HEAD_DIM)), whole((1, HEAD_DIM)),
            whole((N_GROUPS, CHUNK)), whole((N_BUCKETS, N_HEADS)),
            whole((1, HEAD_DIM)), whole((1, HEAD_DIM)), whole((1, HEAD_DIM)), whole((1, HEAD_DIM)),
            whole((1, V_DIM)),
        ],
        out_specs=[
            pl.BlockSpec((MOD_ROWS_PER_STEP, BATCH, D_MODEL), lambda j: (j, 0, 0)),
            whole((QK_WIDTH, LANES)), whole((QK_WIDTH, LANES)),
            whole((CHUNK, GMLP_WIDTH)), whole((1, 1)), whole((4, HEAD_DIM)), whole((1, V_DIM)),
        ],
        out_shape=[
            jax.ShapeDtypeStruct((n // D_MODEL, BATCH, D_MODEL), F32),
            jax.ShapeDtypeStruct((QK_WIDTH, LANES), F32),
            jax.ShapeDtypeStruct((QK_WIDTH, LANES), F32),
            jax.ShapeDtypeStruct((CHUNK, GMLP_WIDTH), F32),
            jax.ShapeDtypeStruct((1, 1), F32),
            jax.ShapeDtypeStruct((4, HEAD_DIM), F32),
            jax.ShapeDtypeStruct((1, V_DIM), F32),
        ],
        compiler_params=pltpu.CompilerParams(
            dimension_semantics=("arbitrary",), vmem_limit_bytes=VMEM_LIMIT),
        name="adaln_mod",
    )(c, w_ada, b_ada.reshape(1, n), q_norm_g, k_norm_g, b_spatial, rel_bias_table,
      lam_q1, lam_k1, lam_q2, lam_k2, subln_g)


def _mod_row(mod_ref, j, b):
    return mod_ref[j, pl.ds(b, 1), :]


def _ffn_kernel(x_ref, mod_ref, wg_ref, wu_ref, wd_ref, o_ref, wg_s, wu_s, wd_s, *, sub):
    i = pl.program_id(0)

    @pl.when(i < W_STEPS)
    def _():
        _stash_rows(wg_s, wg_ref, i)
        _stash_rows(wu_s, wu_ref, i)
        _stash_rows(wd_s, wd_ref, i)

    @pl.when(i >= W_STEPS)
    def _():
        b = lax.div(i - W_STEPS, SEQ // TM_FFN)
        shift = _mod_row(mod_ref, 3 * sub + 0, b)
        scale = _mod_row(mod_ref, 3 * sub + 1, b)
        gate = _mod_row(mod_ref, 3 * sub + 2, b)
        for r in range(0, TM_FFN, FFN_CHUNK):
            x = x_ref[r:r + FFN_CHUNK, :]
            h = _modulated_rmsnorm(x, shift, scale).astype(BF16)
            g = jnp.dot(h, wg_s[...], preferred_element_type=F32)
            u = jnp.dot(h, wu_s[...], preferred_element_type=F32)
            a = (_silu(g) * u).astype(BF16)
            y = jnp.dot(a, wd_s[...], preferred_element_type=F32)
            o_ref[r:r + FFN_CHUNK, :] = x + (0.5 * gate) * y


def _ffn(x2d, mod, wg, wu, wd, *, sub):
    t = x2d.shape[0]
    tile = lambda i: jnp.maximum(i - W_STEPS, 0)
    return pl.pallas_call(
        functools.partial(_ffn_kernel, sub=sub),
        grid=(W_STEPS + t // TM_FFN,),
        in_specs=[
            pl.BlockSpec((TM_FFN, D_MODEL), lambda i: (tile(i), 0)),
            _const_spec((N_MOD, BATCH, D_MODEL)),
            _weight_rows_spec(wg.shape, W_STEPS),
            _weight_rows_spec(wu.shape, W_STEPS),
            _weight_rows_spec(wd.shape, W_STEPS),
        ],
        out_specs=pl.BlockSpec((TM_FFN, D_MODEL), lambda i: (tile(i), 0)),
        out_shape=jax.ShapeDtypeStruct((t, D_MODEL), F32),
        scratch_shapes=[pltpu.VMEM(wg.shape, BF16), pltpu.VMEM(wu.shape, BF16),
                        pltpu.VMEM(wd.shape, BF16)],
        compiler_params=pltpu.CompilerParams(
            dimension_semantics=("arbitrary",), vmem_limit_bytes=VMEM_LIMIT),
        name=f"ffn{sub}",
    )(x2d, mod, wg, wu, wd)


def _chunk_rmsnorm_rows(xt, gain_rows):
    width, tm = xt.shape
    x3 = xt.reshape(width // HEAD_DIM, HEAD_DIM, tm)
    ms = jnp.mean(x3 * x3, axis=1, keepdims=True)
    xn = (x3 * lax.rsqrt(ms + EPS)).reshape(width, tm)
    return jnp.concatenate(
        [xn[:, j * LANES:(j + 1) * LANES] * gain_rows for j in range(tm // LANES)], axis=1)


def _inproj_tile(x_ref, shift, scale, wqkvt_ref, wrest_ref, qg_ref, kg_ref,
                 lng_ref, lnb_ref, ws_ref, bs_ref,
                 qt_ref, k_ref, vt_ref, ub_ref, ga_ref, gb_ref):
    row = lax.broadcasted_iota(jnp.int32, (CHUNK, CHUNK), 0)
    col = lax.broadcasted_iota(jnp.int32, (CHUNK, CHUNK), 1)
    w_spatial = [jnp.where(row >= col, ws_ref[g], 0.0).astype(BF16)
                 for g in range(N_GROUPS)]
    for c in range(TM_PROJ // TQ):
        rows = slice(c * TQ, (c + 1) * TQ)
        x = x_ref[rows, :]
        h = _modulated_rmsnorm(x, shift, scale).astype(BF16)

        pt = lax.dot_general(wqkvt_ref[...], h, (((1,), (1,)), ((), ())),
                             preferred_element_type=F32)
        qt = _chunk_rmsnorm_rows(pt[0:QK_WIDTH], qg_ref[...])
        qt_ref[0, c] = qt.astype(BF16)
        kt = _chunk_rmsnorm_rows(pt[QK_WIDTH:2 * QK_WIDTH], kg_ref[...])
        k_ref[0, rows, :] = kt.T.astype(BF16)
        vt_ref[0, c] = pt[2 * QK_WIDTH:QKV_COLS].astype(BF16)

        rest = jnp.dot(h, wrest_ref[...], preferred_element_type=F32)
        u = _gelu(rest[:, 0:GMLP_WIDTH])
        gv = _gelu(rest[:, GMLP_WIDTH:2 * GMLP_WIDTH])
        ga_ref[0, rows, :] = jax.nn.sigmoid(
            rest[:, 2 * GMLP_WIDTH:2 * GMLP_WIDTH + D_MODEL]).astype(BF16)
        gb_ref[0, rows, :] = jax.nn.sigmoid(
            rest[:, 2 * GMLP_WIDTH + D_MODEL:REST_COLS]).astype(BF16)

        mu = jnp.mean(gv, axis=-1, keepdims=True)
        cen = gv - mu
        var = jnp.mean(cen * cen, axis=-1, keepdims=True)
        vln = ((cen * lax.rsqrt(var + EPS)) * lng_ref[...] + lnb_ref[...]).astype(BF16)

        for g in range(N_GROUPS):
            cs = slice(g * GROUP_DIM, (g + 1) * GROUP_DIM)
            for n in range(TQ // CHUNK):
                rs = slice(n * CHUNK, (n + 1) * CHUNK)
                f = jnp.dot(w_spatial[g], vln[rs, cs], preferred_element_type=F32) + bs_ref[:, cs]
                ub_ref[0, c * TQ + n * CHUNK:c * TQ + (n + 1) * CHUNK, cs] = (
                    u[rs, cs] * f).astype(BF16)


def _inproj_kernel(x_ref, mod_ref, win_ref, qg_ref, kg_ref, lng_ref, lnb_ref, ws_ref, bs_ref,
                   qt_ref, k_ref, vt_ref, ub_ref, ga_ref, gb_ref, wqkvt_s, wrest_s):
    i = pl.program_id(0)

    @pl.when(i < W_STEPS_PROJ)
    def _():
        rows = win_ref.shape[0]
        r = pl.multiple_of(i * rows, rows)
        chunk = win_ref[...]
        wqkvt_s[:, pl.ds(r, rows)] = chunk[:, 0:QKV_COLS].T.astype(BF16)
        wrest_s[pl.ds(r, rows), :] = chunk[:, QKV_COLS:QKV_COLS + REST_COLS].astype(BF16)

    @pl.when(i >= W_STEPS_PROJ)
    def _():
        b = lax.div(i - W_STEPS_PROJ, SEQ // TM_PROJ)
        _inproj_tile(x_ref, _mod_row(mod_ref, 3, b), _mod_row(mod_ref, 4, b), wqkvt_s, wrest_s,
                     qg_ref, kg_ref, lng_ref, lnb_ref, ws_ref, bs_ref,
                     qt_ref, k_ref, vt_ref, ub_ref, ga_ref, gb_ref)


def _inproj(x3d, mod, w_in, qg_rows, kg_rows, ln_g, ln_b, w_spatial, bs_full):
    nt = SEQ // TM_PROJ
    tile = lambda i: jnp.maximum(i - W_STEPS_PROJ, 0)
    tok_spec = lambda width: pl.BlockSpec(
        (1, TM_PROJ, width), lambda i: (tile(i) // nt, tile(i) % nt, 0))
    feat_spec = lambda width: pl.BlockSpec(
        (1, TM_PROJ // TQ, width, TQ), lambda i: (tile(i) // nt, tile(i) % nt, 0, 0))
    return pl.pallas_call(
        _inproj_kernel,
        grid=(W_STEPS_PROJ + BATCH * nt,),
        in_specs=[
            pl.BlockSpec((None, TM_PROJ, D_MODEL), lambda i: (tile(i) // nt, tile(i) % nt, 0)),
            _const_spec((N_MOD, BATCH, D_MODEL)),
            _weight_rows_spec(w_in.shape, W_STEPS_PROJ),
            _const_spec((QK_WIDTH, LANES)),
            _const_spec((QK_WIDTH, LANES)),
            _const_spec((1, GMLP_WIDTH)),
            _const_spec((1, GMLP_WIDTH)),
            _const_spec((N_GROUPS, CHUNK, CHUNK)),
            _const_spec((CHUNK, GMLP_WIDTH)),
        ],
        out_specs=[
            feat_spec(QK_WIDTH), tok_spec(QK_WIDTH), feat_spec(ATTN_WIDTH),
            tok_spec(GMLP_WIDTH), tok_spec(D_MODEL), tok_spec(D_MODEL),
        ],
        out_shape=[
            jax.ShapeDtypeStruct((BATCH, SEQ // TQ, QK_WIDTH, TQ), BF16),
            jax.ShapeDtypeStruct((BATCH, SEQ, QK_WIDTH), BF16),
            jax.ShapeDtypeStruct((BATCH, SEQ // TQ, ATTN_WIDTH, TQ), BF16),
            jax.ShapeDtypeStruct((BATCH, SEQ, GMLP_WIDTH), BF16),
            jax.ShapeDtypeStruct((BATCH, SEQ, D_MODEL), BF16),
            jax.ShapeDtypeStruct((BATCH, SEQ, D_MODEL), BF16),
        ],
        scratch_shapes=[pltpu.VMEM((QKV_COLS, D_MODEL), BF16),
                        pltpu.VMEM((D_MODEL, REST_COLS), BF16)],
        compiler_params=pltpu.CompilerParams(
            dimension_semantics=("arbitrary",), vmem_limit_bytes=VMEM_LIMIT),
        name="inproj",
    )(x3d, mod, w_in, qg_rows, kg_rows, ln_g, ln_b, w_spatial, bs_full)


def _rel_buckets(dist):
    n = np.maximum(dist, 0)
    max_exact = N_BUCKETS // 2
    nf = np.maximum(n, 1).astype(np.float32)
    large = max_exact + (np.log(nf / np.float32(max_exact))
                         / np.float32(math.log(MAX_DISTANCE / max_exact))
                         * np.float32(N_BUCKETS - max_exact)).astype(np.int32)
    large = np.minimum(large, N_BUCKETS - 1)
    return np.where(n < max_exact, n, large).astype(np.int32)


def _attn_kernel(table_ref, bound_ref, lam_ref, qt_ref, k_ref, vt_ref, bkt_ref, subg_ref,
                 o_ref, bias_ref, s_scr):
    b = pl.program_id(1)

    @pl.when(b == 0)
    def _():
        heads = [pl.program_id(0) * HEADS_PER_STEP + hh for hh in range(HEADS_PER_STEP)]

        def lookup(bkt):
            tiles = [jnp.zeros(bkt.shape, F32)] * HEADS_PER_STEP
            for bucket in range(N_BUCKETS - 1):
                hit = bkt == bucket
                tiles = [jnp.where(hit, (table_ref[bucket, h] - table_ref[N_BUCKETS - 1, h]) * LOG2E, t)
                         for h, t in zip(heads, tiles)]
            return tiles

        key = lax.broadcasted_iota(jnp.int32, (TQ, TQ), 0)
        qry = lax.broadcasted_iota(jnp.int32, (TQ, TQ), 1)
        half = TQ // 2
        z = jnp.zeros((half, half), F32)
        diag = lookup(bkt_ref[0])
        quad = lookup(bkt_ref[1, half:TQ, 0:half])
        for hh in range(HEADS_PER_STEP):
            tile = jnp.where(key <= qry, diag[hh], -jnp.inf)
            bias_ref[hh, 0] = jnp.concatenate([tile, tile], axis=1)
            tile = jnp.concatenate([jnp.concatenate([z, z], axis=1),
                                    jnp.concatenate([quad[hh], z], axis=1)], axis=0)
            bias_ref[hh, 1] = jnp.concatenate([tile, tile], axis=1)

    @pl.when(bound_ref[0, 0] <= MAX_SAFE_LOG2)
    def _():
        _attn_program(lam_ref, subg_ref, qt_ref, k_ref, vt_ref, o_ref, bias_ref, s_scr,
                      track_max=False)

    @pl.when(jnp.logical_not(bound_ref[0, 0] <= MAX_SAFE_LOG2))
    def _():
        _attn_program(lam_ref, subg_ref, qt_ref, k_ref, vt_ref, o_ref, bias_ref, s_scr,
                      track_max=True)


def _attn_program(lam_ref, subg_ref, qt_ref, k_ref, vt_ref, o_ref, bias_ref, s_scr, *, track_max):
    lp = lam_ref[...]
    lam = (jnp.exp(jnp.sum(lp[0:1] * lp[1:2], axis=1, keepdims=True))
           - jnp.exp(jnp.sum(lp[2:3] * lp[3:4], axis=1, keepdims=True)) + LAM_INIT)
    subg = subg_ref[...]
    zeros = jnp.zeros((HEAD_DIM, TQ), BF16)
    ones = jnp.ones((SUM_ROWS, TQ), BF16)
    n_strips = 2 * TQ // LANES
    nq = SEQ // TQ
    blocks = [(hh, qi, j) for hh in range(HEADS_PER_STEP) for qi in range(nq) for j in range(qi + 1)]
    qpads = {}

    def head_cols(hh, width):
        return slice(hh * width, (hh + 1) * width)

    def keys(hh, j):
        return k_ref[0, j * TQ:(j + 1) * TQ, head_cols(hh, 2 * HEAD_DIM)]

    def qpad(hh, qi):
        if (hh, qi) not in qpads:
            qt = qt_ref[0, qi, head_cols(hh, 2 * HEAD_DIM), :]
            qpads[hh, qi] = jnp.concatenate(
                [jnp.concatenate([qt[0:HEAD_DIM], zeros], axis=0),
                 jnp.concatenate([zeros, qt[HEAD_DIM:2 * HEAD_DIM]], axis=0)], axis=1)
        return qpads[hh, qi]

    def scores(hh, qi, j):
        return jnp.dot(keys(hh, j), qpad(hh, qi), preferred_element_type=F32)

    def probs_tracked(s, hh, qi, j, m):
        p, alpha, m_out = [], [], []
        for c in range(n_strips):
            sc = s[:, c * LANES:(c + 1) * LANES]
            if j == qi:
                sc = sc + bias_ref[hh, 0, :, c * LANES:(c + 1) * LANES]
            elif j == qi - 1:
                sc = sc + bias_ref[hh, 1, :, c * LANES:(c + 1) * LANES]
            blk_max = jnp.max(sc, axis=0, keepdims=True)
            m_new = blk_max if j == 0 else jnp.maximum(m[c], blk_max)
            p.append(jnp.exp2(sc - m_new).astype(BF16))
            alpha.append(None if j == 0 else jnp.exp2(m[c] - m_new))
            m_out.append(m_new)
        return jnp.concatenate(p, axis=1), alpha, m_out

    half = TQ // 2
    hi_cols = (slice(half, TQ), slice(TQ + half, 2 * TQ))

    def issue_bounded(n):
        hh, qi, j = blocks[n]
        slot = s_scr.at[n % S_SLOTS]
        if j < qi:
            slot[...] = scores(hh, qi, j)
            return
        kb = keys(hh, j)
        qp = qpad(hh, qi)
        slot[0:half, :] = jnp.dot(kb[0:half], qp, preferred_element_type=F32)
        s_hi = jnp.dot(kb[half:TQ], jnp.concatenate([qp[:, c] for c in hi_cols], axis=1),
                       preferred_element_type=F32)
        for t, c in enumerate(hi_cols):
            slot[half:TQ, c] = s_hi[:, t * half:(t + 1) * half]

    def probs_bounded(n):
        hh, qi, j = blocks[n]
        slot = s_scr.at[n % S_SLOTS]
        if j < qi - 1:
            return jnp.exp2(slot[...])
        if j == qi - 1:
            return jnp.exp2(slot[...] + bias_ref[hh, 1])
        lo = jnp.exp2(slot[0:half, :] + bias_ref[hh, 0, 0:half, :])
        z = jnp.zeros((half, half), F32)
        hi = [jnp.exp2(slot[half:TQ, c] + bias_ref[hh, 0, half:TQ, c]) for c in hi_cols]
        return jnp.concatenate([lo, jnp.concatenate([z, hi[0], z, hi[1]], axis=1)], axis=0)

    ahead = QK_AHEAD_TRACKED if track_max else QK_AHEAD_BOUNDED

    issue = (lambda n: scores(*blocks[n])) if track_max else issue_bounded
    pending = [issue(n) for n in range(ahead)]
    m = acc = psum = None
    for n, (hh, qi, j) in enumerate(blocks):
        s = pending.pop(0)
        if n + ahead < len(blocks):
            pending.append(issue(n + ahead))
        vt = vt_ref[0, j, head_cols(hh, V_DIM), :]
        if track_max:
            p, alpha, m = probs_tracked(s, hh, qi, j, m)
            vt_aug = jnp.concatenate([vt, ones], axis=0)
            pv = jnp.dot(vt_aug, p, preferred_element_type=F32)[0:V_DIM + 8]
            if j == 0:
                acc = pv
            else:
                acc = jnp.concatenate(
                    [alpha[c] * acc[:, c * LANES:(c + 1) * LANES] + pv[:, c * LANES:(c + 1) * LANES]
                     for c in range(n_strips)], axis=1)
        else:
            p = probs_bounded(n)
            blk_sum = jnp.sum(p, axis=0, keepdims=True)
            pv = jnp.dot(vt, p.astype(BF16), preferred_element_type=F32)
            acc = pv if j == 0 else acc + pv
            psum = blk_sum if j == 0 else psum + blk_sum
        if j == qi:
            inv = 1.0 / (acc[V_DIM:V_DIM + 1] if track_max else psum)
            ot = (acc[0:V_DIM, 0:TQ] * inv[:, 0:TQ]
                  - acc[0:V_DIM, TQ:2 * TQ] * (lam * inv[:, TQ:2 * TQ]))
            ms = jnp.mean(ot * ot, axis=0, keepdims=True)
            on = (ot * lax.rsqrt(ms + EPS)).T
            o_ref[0, qi * TQ:(qi + 1) * TQ, head_cols(hh, V_DIM)] = (on * subg).astype(BF16)


def _attention(table, bound, lam_rows, qt, k, vt, buckets, sub_gain):
    return pl.pallas_call(
        _attn_kernel,
        grid=(N_HEADS // HEADS_PER_STEP, BATCH),
        in_specs=[
            pl.BlockSpec(memory_space=pltpu.SMEM),
            pl.BlockSpec(memory_space=pltpu.SMEM),
            _const_spec((4, HEAD_DIM)),
            pl.BlockSpec((1, SEQ // TQ, HEADS_PER_STEP * 2 * HEAD_DIM, TQ),
                         lambda g, b: (b, 0, g, 0)),
            pl.BlockSpec((1, SEQ, HEADS_PER_STEP * 2 * HEAD_DIM), lambda g, b: (b, 0, g)),
            pl.BlockSpec((1, SEQ // TQ, HEADS_PER_STEP * V_DIM, TQ), lambda g, b: (b, 0, g, 0)),
            _const_spec((2, TQ, TQ)),
            _const_spec((1, V_DIM)),
        ],
        out_specs=pl.BlockSpec((1, SEQ, HEADS_PER_STEP * V_DIM), lambda g, b: (b, 0, g)),
        out_shape=jax.ShapeDtypeStruct((BATCH, SEQ, ATTN_WIDTH), BF16),
        scratch_shapes=[pltpu.VMEM((HEADS_PER_STEP, 2, TQ, 2 * TQ), F32),
                        pltpu.VMEM((S_SLOTS, TQ, 2 * TQ), F32)],
        compiler_params=pltpu.CompilerParams(
            dimension_semantics=("arbitrary", "arbitrary"),
            vmem_limit_bytes=VMEM_LIMIT),
        name="diff_attn",
    )(table, bound, lam_rows, qt, k, vt, buckets, sub_gain)


def _mix_kernel(x_ref, mod_ref, o_ref, ub_ref, ga_ref, gb_ref, wa_ref, wb_ref, wo_ref,
                out_ref, wa_s, wb_s, wo_s):
    i = pl.program_id(0)

    @pl.when(i < W_STEPS_MIX)
    def _():
        _stash_rows(wa_s, wa_ref, i)
        _stash_rows(wb_s, wb_ref, i)
        _stash_rows(wo_s, wo_ref, i)

    @pl.when(i >= W_STEPS_MIX)
    def _():
        gate = _mod_row(mod_ref, 5, lax.div(i - W_STEPS_MIX, SEQ // TM_MIX))
        for r in range(0, TM_MIX, MIX_CHUNK):
            rows = slice(r, r + MIX_CHUNK)
            ya = jnp.dot(o_ref[rows, :], wa_s[...], preferred_element_type=F32)
            yb = jnp.dot(ub_ref[rows, :], wb_s[...], preferred_element_type=F32)
            merged = (ga_ref[rows, :].astype(F32) * ya
                      + gb_ref[rows, :].astype(F32) * yb).astype(BF16)
            z = jnp.dot(merged, wo_s[...], preferred_element_type=F32)
            out_ref[rows, :] = x_ref[rows, :] + gate * z


def _mix(x2d, mod, o2d, ub2d, ga2d, gb2d, wa, wb, wo):
    t = x2d.shape[0]
    tile = lambda i: jnp.maximum(i - W_STEPS_MIX, 0)
    tok = lambda: pl.BlockSpec((TM_MIX, D_MODEL), lambda i: (tile(i), 0))
    return pl.pallas_call(
        _mix_kernel,
        grid=(W_STEPS_MIX + t // TM_MIX,),
        in_specs=[
            tok(),
            _const_spec((N_MOD, BATCH, D_MODEL)),
            tok(), tok(), tok(), tok(),
            _weight_rows_spec(wa.shape, W_STEPS_MIX),
            _weight_rows_spec(wb.shape, W_STEPS_MIX),
            _weight_rows_spec(wo.shape, W_STEPS_MIX),
        ],
        out_specs=tok(),
        out_shape=jax.ShapeDtypeStruct((t, D_MODEL), F32),
        scratch_shapes=[pltpu.VMEM(wa.shape, BF16), pltpu.VMEM(wb.shape, BF16),
                        pltpu.VMEM(wo.shape, BF16)],
        compiler_params=pltpu.CompilerParams(
            dimension_semantics=("arbitrary",), vmem_limit_bytes=VMEM_LIMIT),
        name="mix_out",
    )(x2d, mod, o2d, ub2d, ga2d, gb2d, wa, wb, wo)


def kernel(x, c, w_ada, b_ada, w_ffn1_gate, w_ffn1_up, w_ffn1_down, w_in, q_norm_g, k_norm_g, lam_q1, lam_k1, lam_q2, lam_k2, subln_g, rel_bias_table, gmlp_ln_g, gmlp_ln_b, w_spatial, b_spatial, w_a_proj, w_b_proj, w_o, w_ffn2_gate, w_ffn2_up, w_ffn2_down):
    b, s, d = x.shape
    t = b * s

    mod, qg_rows, kg_rows, bs_full, score_bound, lam_rows, sub_gain = _adaln_mod(
        c, w_ada[0], b_ada[0], q_norm_g, k_norm_g, b_spatial[0], rel_bias_table,
        lam_q1, lam_k1, lam_q2, lam_k2, subln_g)

    x1 = _ffn(x.reshape(t, d), mod, w_ffn1_gate[0], w_ffn1_up[0], w_ffn1_down[0], sub=0)

    qt, k, vt, ub, ga, gb = _inproj(
        x1.reshape(b, s, d), mod, w_in[0], qg_rows, kg_rows, gmlp_ln_g, gmlp_ln_b,
        w_spatial[0], bs_full)

    kk = np.arange(TQ)[:, None]
    qq = np.arange(TQ)[None, :]
    near = _rel_buckets(TQ + qq - kk)
    far_only = near == N_BUCKETS - 1
    far_only[TQ // 2:, :TQ // 2] = True
    assert far_only.all(), "previous-block bias must be confined to one quadrant"
    buckets = jnp.asarray(np.stack([_rel_buckets(qq - kk), near]))
    o = _attention(rel_bias_table, score_bound, lam_rows, qt, k, vt, buckets, sub_gain)

    x2 = _mix(x1, mod, o.reshape(t, ATTN_WIDTH), ub.reshape(t, GMLP_WIDTH),
              ga.reshape(t, d), gb.reshape(t, d), w_a_proj[0], w_b_proj[0], w_o[0])

    x3 = _ffn(x2, mod, w_ffn2_gate[0], w_ffn2_up[0], w_ffn2_down[0], sub=2)
    return x3.reshape(b, s, d)
```

```python
import functools
import math

import numpy as np
import jax
import jax.numpy as jnp
from jax import lax
from jax.experimental import pallas as pl
from jax.experimental.pallas import tpu as pltpu

F32 = jnp.float32
BF16 = jnp.bfloat16

D_MODEL = 1024
BATCH = 8
SEQ = 2048
N_HEADS = 8
HEAD_DIM = 64
V_DIM = 2 * HEAD_DIM
QK_WIDTH = N_HEADS * 2 * HEAD_DIM
ATTN_WIDTH = N_HEADS * V_DIM
N_GROUPS = 8
CHUNK = 128
GMLP_WIDTH = 1024
GROUP_DIM = GMLP_WIDTH // N_GROUPS
N_BUCKETS = 32
MAX_DISTANCE = 128
D_FF = 2816
N_MOD = 9
EPS = 1e-6
LAM_INIT = 0.8 - 0.6 * math.exp(-0.3 * 0)
LOG2E = math.log2(math.e)
QKV_COLS = 2 * QK_WIDTH + ATTN_WIDTH
REST_COLS = 2 * GMLP_WIDTH + 2 * D_MODEL

LANES = 128
VMEM_LIMIT = 56 * 1024 * 1024

FFN_CHUNK = 256
TM_FFN = 1024
TM_PROJ = 512
MIX_CHUNK = 256
TM_MIX = 1024
TQ = 256
SUM_ROWS = 16
W_STEPS = 8
W_STEPS_PROJ = 4
W_STEPS_MIX = 2
MAX_SAFE_LOG2 = 64.0
BF16_SLACK = 1.02
QK_AHEAD_TRACKED = 2
QK_AHEAD_BOUNDED = 3
S_SLOTS = QK_AHEAD_BOUNDED + 1
HEADS_PER_STEP = 2
MOD_ROWS_PER_STEP = 3


def _const_spec(shape):
    nd = len(shape)
    return pl.BlockSpec(shape, lambda *_: (0,) * nd, pipeline_mode=pl.Buffered(1))


def _weight_rows_spec(shape, steps):
    rows, cols = shape
    return pl.BlockSpec((rows // steps, cols), lambda i: (jnp.minimum(i, steps - 1), 0))


def _stash_rows(dst_ref, chunk_ref, i):
    rows = chunk_ref.shape[0]
    dst_ref[pl.ds(pl.multiple_of(i * rows, rows), rows), :] = chunk_ref[...].astype(BF16)


def _silu(x):
    return x * jax.nn.sigmoid(x)


def _gelu(x):
    return 0.5 * x * (1.0 + lax.erf(x * (1.0 / math.sqrt(2.0))))


def _modulated_rmsnorm(x, shift, scale):
    ms = jnp.mean(x * x, axis=-1, keepdims=True)
    return (x * lax.rsqrt(ms + EPS)) * (1.0 + scale) + shift


def _as_column(row):
    n = row.shape[1]
    r = lax.broadcasted_iota(jnp.int32, (n, n), 0)
    c = lax.broadcasted_iota(jnp.int32, (n, n), 1)
    return jnp.sum(jnp.where(r == c, row, 0.0), axis=1, keepdims=True)


def _mod_kernel(c_ref, w_ref, b_ref, qg_ref, kg_ref, bsp_ref, table_ref,
                lq1_ref, lk1_ref, lq2_ref, lk2_ref, subg_ref,
                o_ref, qgr_ref, kgr_ref, bsf_ref, bound_ref, lam_ref, subs_ref):
    a = _silu(c_ref[...]).astype(BF16)
    m = jnp.dot(a, w_ref[...].astype(BF16), preferred_element_type=F32) + b_ref[...]
    for r in range(MOD_ROWS_PER_STEP):
        o_ref[r] = m[:, r * D_MODEL:(r + 1) * D_MODEL]

    @pl.when(pl.program_id(0) == 0)
    def _():
        qg = qg_ref[...] * (HEAD_DIM ** -0.5 * LOG2E)
        kg = kg_ref[...]
        reps = QK_WIDTH // HEAD_DIM
        qgr_ref[...] = jnp.concatenate(
            [jnp.broadcast_to(_as_column(qg), (HEAD_DIM, LANES))] * reps, axis=0)
        kgr_ref[...] = jnp.concatenate(
            [jnp.broadcast_to(_as_column(kg), (HEAD_DIM, LANES))] * reps, axis=0)
        bsf_ref[...] = jnp.concatenate(
            [jnp.broadcast_to(_as_column(bsp_ref[g:g + 1, :]), (CHUNK, GROUP_DIM))
             for g in range(N_GROUPS)], axis=1)
        rel = jnp.float32(0.0)
        for h in range(N_HEADS):
            for bucket in range(N_BUCKETS - 1):
                rel = jnp.maximum(
                    rel, jnp.abs(table_ref[bucket, h] - table_ref[N_BUCKETS - 1, h]))
        bound_ref[...] = (
            HEAD_DIM * BF16_SLACK * jnp.max(jnp.abs(qg), axis=(0, 1), keepdims=True)
            * jnp.max(jnp.abs(kg), axis=(0, 1), keepdims=True)
            + LOG2E * rel)
        lam_ref[...] = jnp.concatenate(
            [lq1_ref[...], lk1_ref[...], lq2_ref[...], lk2_ref[...]], axis=0)
        subs_ref[...] = subg_ref[...] * (1.0 - LAM_INIT)


def _adaln_mod(c, w_ada, b_ada, q_norm_g, k_norm_g, b_spatial, rel_bias_table,
               lam_q1, lam_k1, lam_q2, lam_k2, subln_g):
    n = w_ada.shape[1]
    whole = lambda shape: pl.BlockSpec(shape, lambda j: (0,) * len(shape))
    return pl.pallas_call(
        _mod_kernel,
        grid=(n // (MOD_ROWS_PER_STEP * D_MODEL),),
        in_specs=[
            whole((BATCH, D_MODEL)),
            pl.BlockSpec((D_MODEL, MOD_ROWS_PER_STEP * D_MODEL), lambda j: (0, j)),
            pl.BlockSpec((1, MOD_ROWS_PER_STEP * D_MODEL), lambda j: (0, j)),
            whole((1, HEAD_DIM)), whole((1, HEAD_DIM)),
            whole((N_GROUPS, CHUNK)), pl.BlockSpec(memory_space=pltpu.SMEM),
            whole((1, HEAD_DIM)), whole((1, HEAD_DIM)), whole((1, HEAD_DIM)), whole((1, HEAD_DIM)),
            whole((1, V_DIM)),
        ],
        out_specs=[
            pl.BlockSpec((MOD_ROWS_PER_STEP, BATCH, D_MODEL), lambda j: (j, 0, 0)),
            whole((QK_WIDTH, LANES)), whole((QK_WIDTH, LANES)),
            whole((CHUNK, GMLP_WIDTH)), whole((1, 1)), whole((4, HEAD_DIM)), whole((1, V_DIM)),
        ],
        out_shape=[
            jax.ShapeDtypeStruct((n // D_MODEL, BATCH, D_MODEL), F32),
            jax.ShapeDtypeStruct((QK_WIDTH, LANES), F32),
            jax.ShapeDtypeStruct((QK_WIDTH, LANES), F32),
            jax.ShapeDtypeStruct((CHUNK, GMLP_WIDTH), F32),
            jax.ShapeDtypeStruct((1, 1), F32),
            jax.ShapeDtypeStruct((4, HEAD_DIM), F32),
            jax.ShapeDtypeStruct((1, V_DIM), F32),
        ],
        compiler_params=pltpu.CompilerParams(
            dimension_semantics=("arbitrary",), vmem_limit_bytes=VMEM_LIMIT),
        name="adaln_mod",
    )(c, w_ada, b_ada, q_norm_g, k_norm_g, b_spatial, rel_bias_table,
      lam_q1, lam_k1, lam_q2, lam_k2, subln_g)


def _mod_row(mod_ref, j, b):
    return mod_ref[j, pl.ds(b, 1), :]


def _ffn_kernel(x_ref, mod_ref, wg_ref, wu_ref, wd_ref, o_ref, wg_s, wu_s, wd_s, *, sub):
    i = pl.program_id(0)

    @pl.when(i < W_STEPS)
    def _():
        _stash_rows(wg_s, wg_ref, i)
        _stash_rows(wu_s, wu_ref, i)
        _stash_rows(wd_s, wd_ref, i)

    @pl.when(i >= W_STEPS)
    def _():
        b = lax.div(i - W_STEPS, SEQ // TM_FFN)
        shift = _mod_row(mod_ref, 3 * sub + 0, b)
        scale = _mod_row(mod_ref, 3 * sub + 1, b)
        gate = _mod_row(mod_ref, 3 * sub + 2, b)
        for r in range(0, TM_FFN, FFN_CHUNK):
            x = x_ref[r:r + FFN_CHUNK, :]
            h = _modulated_rmsnorm(x, shift, scale).astype(BF16)
            g = jnp.dot(h, wg_s[...], preferred_element_type=F32)
            u = jnp.dot(h, wu_s[...], preferred_element_type=F32)
            a = (_silu(g) * u).astype(BF16)
            y = jnp.dot(a, wd_s[...], preferred_element_type=F32)
            o_ref[r:r + FFN_CHUNK, :] = x + (0.5 * gate) * y


def _ffn(x2d, mod, wg, wu, wd, *, sub):
    t = x2d.shape[0]
    tile = lambda i: jnp.maximum(i - W_STEPS, 0)
    return pl.pallas_call(
        functools.partial(_ffn_kernel, sub=sub),
        grid=(W_STEPS + t // TM_FFN,),
        in_specs=[
            pl.BlockSpec((TM_FFN, D_MODEL), lambda i: (tile(i), 0)),
            _const_spec((N_MOD, BATCH, D_MODEL)),
            _weight_rows_spec(wg.shape, W_STEPS),
            _weight_rows_spec(wu.shape, W_STEPS),
            _weight_rows_spec(wd.shape, W_STEPS),
        ],
        out_specs=pl.BlockSpec((TM_FFN, D_MODEL), lambda i: (tile(i), 0)),
        out_shape=jax.ShapeDtypeStruct((t, D_MODEL), F32),
        scratch_shapes=[pltpu.VMEM(wg.shape, BF16), pltpu.VMEM(wu.shape, BF16),
                        pltpu.VMEM(wd.shape, BF16)],
        compiler_params=pltpu.CompilerParams(
            dimension_semantics=("arbitrary",), vmem_limit_bytes=VMEM_LIMIT),
        name=f"ffn{sub}",
    )(x2d, mod, wg, wu, wd)


def _chunk_rmsnorm_rows(xt, gain_rows):
    width, tm = xt.shape
    x3 = xt.reshape(width // HEAD_DIM, HEAD_DIM, tm)
    ms = jnp.mean(x3 * x3, axis=1, keepdims=True)
    xn = (x3 * lax.rsqrt(ms + EPS)).reshape(width, tm)
    return jnp.concatenate(
        [xn[:, j * LANES:(j + 1) * LANES] * gain_rows for j in range(tm // LANES)], axis=1)


def _inproj_tile(x_ref, shift, scale, wqkvt_ref, wrest_ref, qg_ref, kg_ref,
                 lng_ref, lnb_ref, ws_ref, bs_ref,
                 qt_ref, k_ref, vt_ref, ub_ref, ga_ref, gb_ref):
    row = lax.broadcasted_iota(jnp.int32, (CHUNK, CHUNK), 0)
    col = lax.broadcasted_iota(jnp.int32, (CHUNK, CHUNK), 1)
    w_spatial = [jnp.where(row >= col, ws_ref[g], 0.0).astype(BF16)
                 for g in range(N_GROUPS)]
    for c in range(TM_PROJ // TQ):
        rows = slice(c * TQ, (c + 1) * TQ)
        x = x_ref[rows, :]
        h = _modulated_rmsnorm(x, shift, scale).astype(BF16)

        pt = lax.dot_general(wqkvt_ref[...], h, (((1,), (1,)), ((), ())),
                             preferred_element_type=F32)
        qt = _chunk_rmsnorm_rows(pt[0:QK_WIDTH], qg_ref[...])
        qt_ref[0, c] = qt.astype(BF16)
        kt = _chunk_rmsnorm_rows(pt[QK_WIDTH:2 * QK_WIDTH], kg_ref[...])
        k_ref[0, rows, :] = kt.T.astype(BF16)
        vt_ref[0, c] = pt[2 * QK_WIDTH:QKV_COLS].astype(BF16)

        rest = jnp.dot(h, wrest_ref[...], preferred_element_type=F32)
        u = _gelu(rest[:, 0:GMLP_WIDTH])
        gv = _gelu(rest[:, GMLP_WIDTH:2 * GMLP_WIDTH])
        ga_ref[0, rows, :] = jax.nn.sigmoid(
            rest[:, 2 * GMLP_WIDTH:2 * GMLP_WIDTH + D_MODEL]).astype(BF16)
        gb_ref[0, rows, :] = jax.nn.sigmoid(
            rest[:, 2 * GMLP_WIDTH + D_MODEL:REST_COLS]).astype(BF16)

        mu = jnp.mean(gv, axis=-1, keepdims=True)
        cen = gv - mu
        var = jnp.mean(cen * cen, axis=-1, keepdims=True)
        vln = ((cen * lax.rsqrt(var + EPS)) * lng_ref[...] + lnb_ref[...]).astype(BF16)

        for g in range(N_GROUPS):
            cs = slice(g * GROUP_DIM, (g + 1) * GROUP_DIM)
            for n in range(TQ // CHUNK):
                rs = slice(n * CHUNK, (n + 1) * CHUNK)
                f = jnp.dot(w_spatial[g], vln[rs, cs], preferred_element_type=F32) + bs_ref[:, cs]
                ub_ref[0, c * TQ + n * CHUNK:c * TQ + (n + 1) * CHUNK, cs] = (
                    u[rs, cs] * f).astype(BF16)


def _inproj_kernel(x_ref, mod_ref, win_ref, qg_ref, kg_ref, lng_ref, lnb_ref, ws_ref, bs_ref,
                   qt_ref, k_ref, vt_ref, ub_ref, ga_ref, gb_ref, wqkvt_s, wrest_s):
    i = pl.program_id(0)

    @pl.when(i < W_STEPS_PROJ)
    def _():
        rows = win_ref.shape[0]
        r = pl.multiple_of(i * rows, rows)
        chunk = win_ref[...]
        wqkvt_s[:, pl.ds(r, rows)] = chunk[:, 0:QKV_COLS].T.astype(BF16)
        wrest_s[pl.ds(r, rows), :] = chunk[:, QKV_COLS:QKV_COLS + REST_COLS].astype(BF16)

    @pl.when(i >= W_STEPS_PROJ)
    def _():
        b = lax.div(i - W_STEPS_PROJ, SEQ // TM_PROJ)
        _inproj_tile(x_ref, _mod_row(mod_ref, 3, b), _mod_row(mod_ref, 4, b), wqkvt_s, wrest_s,
                     qg_ref, kg_ref, lng_ref, lnb_ref, ws_ref, bs_ref,
                     qt_ref, k_ref, vt_ref, ub_ref, ga_ref, gb_ref)


def _inproj(x3d, mod, w_in, qg_rows, kg_rows, ln_g, ln_b, w_spatial, bs_full):
    nt = SEQ // TM_PROJ
    tile = lambda i: jnp.maximum(i - W_STEPS_PROJ, 0)
    tok_spec = lambda width: pl.BlockSpec(
        (1, TM_PROJ, width), lambda i: (tile(i) // nt, tile(i) % nt, 0))
    feat_spec = lambda width: pl.BlockSpec(
        (1, TM_PROJ // TQ, width, TQ), lambda i: (tile(i) // nt, tile(i) % nt, 0, 0))
    return pl.pallas_call(
        _inproj_kernel,
        grid=(W_STEPS_PROJ + BATCH * nt,),
        in_specs=[
            pl.BlockSpec((None, TM_PROJ, D_MODEL), lambda i: (tile(i) // nt, tile(i) % nt, 0)),
            _const_spec((N_MOD, BATCH, D_MODEL)),
            _weight_rows_spec(w_in.shape, W_STEPS_PROJ),
            _const_spec((QK_WIDTH, LANES)),
            _const_spec((QK_WIDTH, LANES)),
            _const_spec((1, GMLP_WIDTH)),
            _const_spec((1, GMLP_WIDTH)),
            _const_spec((N_GROUPS, CHUNK, CHUNK)),
            _const_spec((CHUNK, GMLP_WIDTH)),
        ],
        out_specs=[
            feat_spec(QK_WIDTH), tok_spec(QK_WIDTH), feat_spec(ATTN_WIDTH),
            tok_spec(GMLP_WIDTH), tok_spec(D_MODEL), tok_spec(D_MODEL),
        ],
        out_shape=[
            jax.ShapeDtypeStruct((BATCH, SEQ // TQ, QK_WIDTH, TQ), BF16),
            jax.ShapeDtypeStruct((BATCH, SEQ, QK_WIDTH), BF16),
            jax.ShapeDtypeStruct((BATCH, SEQ // TQ, ATTN_WIDTH, TQ), BF16),
            jax.ShapeDtypeStruct((BATCH, SEQ, GMLP_WIDTH), BF16),
            jax.ShapeDtypeStruct((BATCH, SEQ, D_MODEL), BF16),
            jax.ShapeDtypeStruct((BATCH, SEQ, D_MODEL), BF16),
        ],
        scratch_shapes=[pltpu.VMEM((QKV_COLS, D_MODEL), BF16),
                        pltpu.VMEM((D_MODEL, REST_COLS), BF16)],
        compiler_params=pltpu.CompilerParams(
            dimension_semantics=("arbitrary",), vmem_limit_bytes=VMEM_LIMIT),
        name="inproj",
    )(x3d, mod, w_in, qg_rows, kg_rows, ln_g, ln_b, w_spatial, bs_full)


def _rel_buckets(dist):
    n = np.maximum(dist, 0)
    max_exact = N_BUCKETS // 2
    nf = np.maximum(n, 1).astype(np.float32)
    large = max_exact + (np.log(nf / np.float32(max_exact))
                         / np.float32(math.log(MAX_DISTANCE / max_exact))
                         * np.float32(N_BUCKETS - max_exact)).astype(np.int32)
    large = np.minimum(large, N_BUCKETS - 1)
    return np.where(n < max_exact, n, large).astype(np.int32)


def _attn_kernel(table_ref, bound_ref, lam_ref, qt_ref, k_ref, vt_ref, bkt_ref, subg_ref,
                 o_ref, bias_ref, s_scr):
    b = pl.program_id(1)

    @pl.when(b == 0)
    def _():
        heads = [pl.program_id(0) * HEADS_PER_STEP + hh for hh in range(HEADS_PER_STEP)]

        def lookup(bkt):
            tiles = [jnp.zeros(bkt.shape, F32)] * HEADS_PER_STEP
            for bucket in range(N_BUCKETS - 1):
                hit = bkt == bucket
                tiles = [jnp.where(hit, (table_ref[bucket, h] - table_ref[N_BUCKETS - 1, h]) * LOG2E, t)
                         for h, t in zip(heads, tiles)]
            return tiles

        key = lax.broadcasted_iota(jnp.int32, (TQ, TQ), 0)
        qry = lax.broadcasted_iota(jnp.int32, (TQ, TQ), 1)
        half = TQ // 2
        z = jnp.zeros((half, half), F32)
        diag = lookup(bkt_ref[0])
        quad = lookup(bkt_ref[1, half:TQ, 0:half])
        for hh in range(HEADS_PER_STEP):
            tile = jnp.where(key <= qry, diag[hh], -jnp.inf)
            bias_ref[hh, 0] = jnp.concatenate([tile, tile], axis=1)
            tile = jnp.concatenate([jnp.concatenate([z, z], axis=1),
                                    jnp.concatenate([quad[hh], z], axis=1)], axis=0)
            bias_ref[hh, 1] = jnp.concatenate([tile, tile], axis=1)

    @pl.when(bound_ref[0, 0] <= MAX_SAFE_LOG2)
    def _():
        _attn_program(lam_ref, subg_ref, qt_ref, k_ref, vt_ref, o_ref, bias_ref, s_scr,
                      track_max=False)

    @pl.when(jnp.logical_not(bound_ref[0, 0] <= MAX_SAFE_LOG2))
    def _():
        _attn_program(lam_ref, subg_ref, qt_ref, k_ref, vt_ref, o_ref, bias_ref, s_scr,
                      track_max=True)


def _attn_program(lam_ref, subg_ref, qt_ref, k_ref, vt_ref, o_ref, bias_ref, s_scr, *, track_max):
    lp = lam_ref[...]
    lam = (jnp.exp(jnp.sum(lp[0:1] * lp[1:2], axis=1, keepdims=True))
           - jnp.exp(jnp.sum(lp[2:3] * lp[3:4], axis=1, keepdims=True)) + LAM_INIT)
    subg = subg_ref[...]
    zeros = jnp.zeros((HEAD_DIM, TQ), BF16)
    ones = jnp.ones((SUM_ROWS, TQ), BF16)
    n_strips = 2 * TQ // LANES
    nq = SEQ // TQ
    blocks = [(hh, qi, j) for hh in range(HEADS_PER_STEP) for qi in range(nq) for j in range(qi + 1)]
    qpads = {}

    def head_cols(hh, width):
        return slice(hh * width, (hh + 1) * width)

    def keys(hh, j):
        return k_ref[0, j * TQ:(j + 1) * TQ, head_cols(hh, 2 * HEAD_DIM)]

    def qpad(hh, qi):
        if (hh, qi) not in qpads:
            qt = qt_ref[0, qi, head_cols(hh, 2 * HEAD_DIM), :]
            qpads[hh, qi] = jnp.concatenate(
                [jnp.concatenate([qt[0:HEAD_DIM], zeros], axis=0),
                 jnp.concatenate([zeros, qt[HEAD_DIM:2 * HEAD_DIM]], axis=0)], axis=1)
        return qpads[hh, qi]

    def scores(hh, qi, j):
        return jnp.dot(keys(hh, j), qpad(hh, qi), preferred_element_type=F32)

    def probs_tracked(s, hh, qi, j, m):
        p, alpha, m_out = [], [], []
        for c in range(n_strips):
            sc = s[:, c * LANES:(c + 1) * LANES]
            if j == qi:
                sc = sc + bias_ref[hh, 0, :, c * LANES:(c + 1) * LANES]
            elif j == qi - 1:
                sc = sc + bias_ref[hh, 1, :, c * LANES:(c + 1) * LANES]
            blk_max = jnp.max(sc, axis=0, keepdims=True)
            m_new = blk_max if j == 0 else jnp.maximum(m[c], blk_max)
            p.append(jnp.exp2(sc - m_new).astype(BF16))
            alpha.append(None if j == 0 else jnp.exp2(m[c] - m_new))
            m_out.append(m_new)
        return jnp.concatenate(p, axis=1), alpha, m_out

    half = TQ // 2
    hi_cols = (slice(half, TQ), slice(TQ + half, 2 * TQ))

    def issue_bounded(n):
        hh, qi, j = blocks[n]
        slot = s_scr.at[n % S_SLOTS]
        if j < qi:
            slot[...] = scores(hh, qi, j)
            return
        kb = keys(hh, j)
        qp = qpad(hh, qi)
        slot[0:half, :] = jnp.dot(kb[0:half], qp, preferred_element_type=F32)
        s_hi = jnp.dot(kb[half:TQ], jnp.concatenate([qp[:, c] for c in hi_cols], axis=1),
                       preferred_element_type=F32)
        for t, c in enumerate(hi_cols):
            slot[half:TQ, c] = s_hi[:, t * half:(t + 1) * half]

    def probs_bounded(n):
        hh, qi, j = blocks[n]
        slot = s_scr.at[n % S_SLOTS]
        if j < qi - 1:
            return jnp.exp2(slot[...])
        if j == qi - 1:
            return jnp.exp2(slot[...] + bias_ref[hh, 1])
        lo = jnp.exp2(slot[0:half, :] + bias_ref[hh, 0, 0:half, :])
        z = jnp.zeros((half, half), F32)
        hi = [jnp.exp2(slot[half:TQ, c] + bias_ref[hh, 0, half:TQ, c]) for c in hi_cols]
        return jnp.concatenate([lo, jnp.concatenate([z, hi[0], z, hi[1]], axis=1)], axis=0)

    ahead = QK_AHEAD_TRACKED if track_max else QK_AHEAD_BOUNDED

    issue = (lambda n: scores(*blocks[n])) if track_max else issue_bounded
    pending = [issue(n) for n in range(ahead)]
    m = acc = psum = None
    for n, (hh, qi, j) in enumerate(blocks):
        s = pending.pop(0)
        if n + ahead < len(blocks):
            pending.append(issue(n + ahead))
        vt = vt_ref[0, j, head_cols(hh, V_DIM), :]
        if track_max:
            p, alpha, m = probs_tracked(s, hh, qi, j, m)
            vt_aug = jnp.concatenate([vt, ones], axis=0)
            pv = jnp.dot(vt_aug, p, preferred_element_type=F32)[0:V_DIM + 8]
            if j == 0:
                acc = pv
            else:
                acc = jnp.concatenate(
                    [alpha[c] * acc[:, c * LANES:(c + 1) * LANES] + pv[:, c * LANES:(c + 1) * LANES]
                     for c in range(n_strips)], axis=1)
        else:
            p = probs_bounded(n)
            blk_sum = jnp.sum(p, axis=0, keepdims=True)
            pv = jnp.dot(vt, p.astype(BF16), preferred_element_type=F32)
            acc = pv if j == 0 else acc + pv
            psum = blk_sum if j == 0 else psum + blk_sum
        if j == qi:
            inv = 1.0 / (acc[V_DIM:V_DIM + 1] if track_max else psum)
            ot = (acc[0:V_DIM, 0:TQ] * inv[:, 0:TQ]
                  - acc[0:V_DIM, TQ:2 * TQ] * (lam * inv[:, TQ:2 * TQ]))
            ms = jnp.mean(ot * ot, axis=0, keepdims=True)
            on = (ot * lax.rsqrt(ms + EPS)).T
            o_ref[0, qi * TQ:(qi + 1) * TQ, head_cols(hh, V_DIM)] = (on * subg).astype(BF16)


def _attention(table, bound, lam_rows, qt, k, vt, buckets, sub_gain):
    return pl.pallas_call(
        _attn_kernel,
        grid=(N_HEADS // HEADS_PER_STEP, BATCH),
        in_specs=[
            pl.BlockSpec(memory_space=pltpu.SMEM),
            pl.BlockSpec(memory_space=pltpu.SMEM),
            _const_spec((4, HEAD_DIM)),
            pl.BlockSpec((1, SEQ // TQ, HEADS_PER_STEP * 2 * HEAD_DIM, TQ),
                         lambda g, b: (b, 0, g, 0)),
            pl.BlockSpec((1, SEQ, HEADS_PER_STEP * 2 * HEAD_DIM), lambda g, b: (b, 0, g)),
            pl.BlockSpec((1, SEQ // TQ, HEADS_PER_STEP * V_DIM, TQ), lambda g, b: (b, 0, g, 0)),
            _const_spec((2, TQ, TQ)),
            _const_spec((1, V_DIM)),
        ],
        out_specs=pl.BlockSpec((1, SEQ, HEADS_PER_STEP * V_DIM), lambda g, b: (b, 0, g)),
        out_shape=jax.ShapeDtypeStruct((BATCH, SEQ, ATTN_WIDTH), BF16),
        scratch_shapes=[pltpu.VMEM((HEADS_PER_STEP, 2, TQ, 2 * TQ), F32),
                        pltpu.VMEM((S_SLOTS, TQ, 2 * TQ), F32)],
        compiler_params=pltpu.CompilerParams(
            dimension_semantics=("arbitrary", "arbitrary"),
            vmem_limit_bytes=VMEM_LIMIT),
        name="diff_attn",
    )(table, bound, lam_rows, qt, k, vt, buckets, sub_gain)


def _mix_kernel(x_ref, mod_ref, o_ref, ub_ref, ga_ref, gb_ref, wa_ref, wb_ref, wo_ref,
                out_ref, wa_s, wb_s, wo_s):
    i = pl.program_id(0)

    @pl.when(i < W_STEPS_MIX)
    def _():
        _stash_rows(wa_s, wa_ref, i)
        _stash_rows(wb_s, wb_ref, i)
        _stash_rows(wo_s, wo_ref, i)

    @pl.when(i >= W_STEPS_MIX)
    def _():
        gate = _mod_row(mod_ref, 5, lax.div(i - W_STEPS_MIX, SEQ // TM_MIX))
        for r in range(0, TM_MIX, MIX_CHUNK):
            rows = slice(r, r + MIX_CHUNK)
            ya = jnp.dot(o_ref[rows, :], wa_s[...], preferred_element_type=F32)
            yb = jnp.dot(ub_ref[rows, :], wb_s[...], preferred_element_type=F32)
            merged = (ga_ref[rows, :].astype(F32) * ya
                      + gb_ref[rows, :].astype(F32) * yb).astype(BF16)
            z = jnp.dot(merged, wo_s[...], preferred_element_type=F32)
            out_ref[rows, :] = x_ref[rows, :] + gate * z


def _mix(x2d, mod, o2d, ub2d, ga2d, gb2d, wa, wb, wo):
    t = x2d.shape[0]
    tile = lambda i: jnp.maximum(i - W_STEPS_MIX, 0)
    tok = lambda: pl.BlockSpec((TM_MIX, D_MODEL), lambda i: (tile(i), 0))
    return pl.pallas_call(
        _mix_kernel,
        grid=(W_STEPS_MIX + t // TM_MIX,),
        in_specs=[
            tok(),
            _const_spec((N_MOD, BATCH, D_MODEL)),
            tok(), tok(), tok(), tok(),
            _weight_rows_spec(wa.shape, W_STEPS_MIX),
            _weight_rows_spec(wb.shape, W_STEPS_MIX),
            _weight_rows_spec(wo.shape, W_STEPS_MIX),
        ],
        out_specs=tok(),
        out_shape=jax.ShapeDtypeStruct((t, D_MODEL), F32),
        scratch_shapes=[pltpu.VMEM(wa.shape, BF16), pltpu.VMEM(wb.shape, BF16),
                        pltpu.VMEM(wo.shape, BF16)],
        compiler_params=pltpu.CompilerParams(
            dimension_semantics=("arbitrary",), vmem_limit_bytes=VMEM_LIMIT),
        name="mix_out",
    )(x2d, mod, o2d, ub2d, ga2d, gb2d, wa, wb, wo)


def kernel(x, c, w_ada, b_ada, w_ffn1_gate, w_ffn1_up, w_ffn1_down, w_in, q_norm_g, k_norm_g, lam_q1, lam_k1, lam_q2, lam_k2, subln_g, rel_bias_table, gmlp_ln_g, gmlp_ln_b, w_spatial, b_spatial, w_a_proj, w_b_proj, w_o, w_ffn2_gate, w_ffn2_up, w_ffn2_down):
    b, s, d = x.shape
    t = b * s

    mod, qg_rows, kg_rows, bs_full, score_bound, lam_rows, sub_gain = _adaln_mod(
        c, w_ada[0], b_ada, q_norm_g, k_norm_g, b_spatial[0], rel_bias_table,
        lam_q1, lam_k1, lam_q2, lam_k2, subln_g)

    x1 = _ffn(x.reshape(t, d), mod, w_ffn1_gate[0], w_ffn1_up[0], w_ffn1_down[0], sub=0)

    qt, k, vt, ub, ga, gb = _inproj(
        x1.reshape(b, s, d), mod, w_in[0], qg_rows, kg_rows, gmlp_ln_g, gmlp_ln_b,
        w_spatial[0], bs_full)

    kk = np.arange(TQ)[:, None]
    qq = np.arange(TQ)[None, :]
    near = _rel_buckets(TQ + qq - kk)
    far_only = near == N_BUCKETS - 1
    far_only[TQ // 2:, :TQ // 2] = True
    assert far_only.all(), "previous-block bias must be confined to one quadrant"
    buckets = jnp.asarray(np.stack([_rel_buckets(qq - kk), near]))
    o = _attention(rel_bias_table, score_bound, lam_rows, qt, k, vt, buckets, sub_gain)

    x2 = _mix(x1, mod, o.reshape(t, ATTN_WIDTH), ub.reshape(t, GMLP_WIDTH),
              ga.reshape(t, d), gb.reshape(t, d), w_a_proj[0], w_b_proj[0], w_o[0])

    x3 = _ffn(x2, mod, w_ffn2_gate[0], w_ffn2_up[0], w_ffn2_down[0], sub=2)
    return x3.reshape(b, s, d)
```

```python
import functools
import math

import numpy as np
import jax
import jax.numpy as jnp
from jax import lax
from jax.experimental import pallas as pl
from jax.experimental.pallas import tpu as pltpu

F32 = jnp.float32
BF16 = jnp.bfloat16

D_MODEL = 1024
BATCH = 8
SEQ = 2048
N_HEADS = 8
HEAD_DIM = 64
V_DIM = 2 * HEAD_DIM
QK_WIDTH = N_HEADS * 2 * HEAD_DIM
ATTN_WIDTH = N_HEADS * V_DIM
N_GROUPS = 8
CHUNK = 128
GMLP_WIDTH = 1024
GROUP_DIM = GMLP_WIDTH // N_GROUPS
N_BUCKETS = 32
MAX_DISTANCE = 128
D_FF = 2816
N_MOD = 9
EPS = 1e-6
LAM_INIT = 0.8 - 0.6 * math.exp(-0.3 * 0)
LOG2E = math.log2(math.e)
QKV_COLS = 2 * QK_WIDTH + ATTN_WIDTH
REST_COLS = 2 * GMLP_WIDTH + 2 * D_MODEL

LANES = 128
VMEM_LIMIT = 56 * 1024 * 1024

FFN_CHUNK = 256
TM_FFN = 1024
TM_PROJ = 512
MIX_CHUNK = 256
TM_MIX = 1024
TQ = 256
SUM_ROWS = 16
W_STEPS = 8
W_STEPS_PROJ = 4
W_STEPS_MIX = 2
MAX_SAFE_LOG2 = 64.0
BF16_SLACK = 1.02
QK_AHEAD_TRACKED = 2
QK_AHEAD_BOUNDED = 3
S_SLOTS = QK_AHEAD_BOUNDED + 1
HEADS_PER_STEP = 2
MOD_ROWS_PER_STEP = 3


def _const_spec(shape):
    nd = len(shape)
    return pl.BlockSpec(shape, lambda *_: (0,) * nd, pipeline_mode=pl.Buffered(1))


def _weight_rows_spec(shape, steps):
    rows, cols = shape
    return pl.BlockSpec((rows // steps, cols), lambda i: (jnp.minimum(i, steps - 1), 0))


def _stash_rows(dst_ref, chunk_ref, i):
    rows = chunk_ref.shape[0]
    dst_ref[pl.ds(pl.multiple_of(i * rows, rows), rows), :] = chunk_ref[...].astype(BF16)


def _silu(x):
    return x * jax.nn.sigmoid(x)


def _gelu(x):
    return 0.5 * x * (1.0 + lax.erf(x * (1.0 / math.sqrt(2.0))))


def _modulated_rmsnorm(x, shift, scale):
    ms = jnp.mean(x * x, axis=-1, keepdims=True)
    return (x * lax.rsqrt(ms + EPS)) * (1.0 + scale) + shift


def _as_column(row):
    n = row.shape[1]
    r = lax.broadcasted_iota(jnp.int32, (n, n), 0)
    c = lax.broadcasted_iota(jnp.int32, (n, n), 1)
    return jnp.sum(jnp.where(r == c, row, 0.0), axis=1, keepdims=True)


def _mod_kernel(c_ref, w_ref, b_ref, qg_ref, kg_ref, bsp_ref, table_ref,
                lq1_ref, lk1_ref, lq2_ref, lk2_ref, subg_ref, lng_ref, lnb_ref,
                o_ref, qgr_ref, kgr_ref, bsf_ref, bound_ref, lam_ref, subs_ref):
    a = _silu(c_ref[...]).astype(BF16)
    m = jnp.dot(a, w_ref[...].astype(BF16), preferred_element_type=F32) + b_ref[...]
    for r in range(MOD_ROWS_PER_STEP):
        o_ref[r] = m[:, r * D_MODEL:(r + 1) * D_MODEL]

    @pl.when(pl.program_id(0) == 0)
    def _():
        qg = qg_ref[...] * (HEAD_DIM ** -0.5 * LOG2E)
        kg = kg_ref[...]
        reps = QK_WIDTH // HEAD_DIM
        qgr_ref[...] = jnp.concatenate(
            [jnp.broadcast_to(_as_column(qg), (HEAD_DIM, LANES))] * reps, axis=0)
        kgr_ref[...] = jnp.concatenate(
            [jnp.broadcast_to(_as_column(kg), (HEAD_DIM, LANES))] * reps, axis=0)
        bsf_ref[0:CHUNK, :] = jnp.concatenate(
            [jnp.broadcast_to(_as_column(bsp_ref[g:g + 1, :]), (CHUNK, GROUP_DIM))
             for g in range(N_GROUPS)], axis=1)
        bsf_ref[CHUNK:CHUNK + 8, :] = jnp.concatenate(
            [lng_ref[...], lnb_ref[...], jnp.zeros((6, GMLP_WIDTH), F32)], axis=0)
        rel = jnp.float32(0.0)
        for h in range(N_HEADS):
            for bucket in range(N_BUCKETS - 1):
                rel = jnp.maximum(
                    rel, jnp.abs(table_ref[bucket, h] - table_ref[N_BUCKETS - 1, h]))
        bound_ref[...] = (
            HEAD_DIM * BF16_SLACK * jnp.max(jnp.abs(qg), axis=(0, 1), keepdims=True)
            * jnp.max(jnp.abs(kg), axis=(0, 1), keepdims=True)
            + LOG2E * rel)
        lam_ref[...] = jnp.concatenate(
            [lq1_ref[...], lk1_ref[...], lq2_ref[...], lk2_ref[...]], axis=0)
        subs_ref[...] = subg_ref[...] * (1.0 - LAM_INIT)


def _adaln_mod(c, w_ada, b_ada, q_norm_g, k_norm_g, b_spatial, rel_bias_table,
               lam_q1, lam_k1, lam_q2, lam_k2, subln_g, ln_g, ln_b):
    n = w_ada.shape[1]
    whole = lambda shape: pl.BlockSpec(shape, lambda j: (0,) * len(shape))
    return pl.pallas_call(
        _mod_kernel,
        grid=(n // (MOD_ROWS_PER_STEP * D_MODEL),),
        in_specs=[
            whole((BATCH, D_MODEL)),
            pl.BlockSpec((D_MODEL, MOD_ROWS_PER_STEP * D_MODEL), lambda j: (0, j)),
            pl.BlockSpec((1, MOD_ROWS_PER_STEP * D_MODEL), lambda j: (0, j)),
            whole((1, HEAD_DIM)), whole((1, HEAD_DIM)),
            whole((N_GROUPS, CHUNK)), pl.BlockSpec(memory_space=pltpu.SMEM),
            whole((1, HEAD_DIM)), whole((1, HEAD_DIM)), whole((1, HEAD_DIM)), whole((1, HEAD_DIM)),
            whole((1, V_DIM)), whole((1, GMLP_WIDTH)), whole((1, GMLP_WIDTH)),
        ],
        out_specs=[
            pl.BlockSpec((MOD_ROWS_PER_STEP, BATCH, D_MODEL), lambda j: (j, 0, 0)),
            whole((QK_WIDTH, LANES)), whole((QK_WIDTH, LANES)),
            whole((CHUNK + 8, GMLP_WIDTH)), whole((1, 1)), whole((4, HEAD_DIM)), whole((1, V_DIM)),
        ],
        out_shape=[
            jax.ShapeDtypeStruct((n // D_MODEL, BATCH, D_MODEL), F32),
            jax.ShapeDtypeStruct((QK_WIDTH, LANES), F32),
            jax.ShapeDtypeStruct((QK_WIDTH, LANES), F32),
            jax.ShapeDtypeStruct((CHUNK + 8, GMLP_WIDTH), F32),
            jax.ShapeDtypeStruct((1, 1), F32),
            jax.ShapeDtypeStruct((4, HEAD_DIM), F32),
            jax.ShapeDtypeStruct((1, V_DIM), F32),
        ],
        compiler_params=pltpu.CompilerParams(
            dimension_semantics=("arbitrary",), vmem_limit_bytes=VMEM_LIMIT),
        name="adaln_mod",
    )(c, w_ada, b_ada, q_norm_g, k_norm_g, b_spatial, rel_bias_table,
      lam_q1, lam_k1, lam_q2, lam_k2, subln_g, ln_g, ln_b)


def _mod_row(mod_ref, j, b):
    return mod_ref[j, pl.ds(b, 1), :]


def _ffn_kernel(x_ref, mod_ref, wg_ref, wu_ref, wd_ref, o_ref, wg_s, wu_s, wd_s, *, sub):
    i = pl.program_id(0)

    @pl.when(i < W_STEPS)
    def _():
        _stash_rows(wg_s, wg_ref, i)
        _stash_rows(wu_s, wu_ref, i)
        _stash_rows(wd_s, wd_ref, i)

    @pl.when(i >= W_STEPS)
    def _():
        b = lax.div(i - W_STEPS, SEQ // TM_FFN)
        shift = _mod_row(mod_ref, 3 * sub + 0, b)
        scale = _mod_row(mod_ref, 3 * sub + 1, b)
        gate = _mod_row(mod_ref, 3 * sub + 2, b)
        for r in range(0, TM_FFN, FFN_CHUNK):
            x = x_ref[r:r + FFN_CHUNK, :]
            h = _modulated_rmsnorm(x, shift, scale).astype(BF16)
            g = jnp.dot(h, wg_s[...], preferred_element_type=F32)
            u = jnp.dot(h, wu_s[...], preferred_element_type=F32)
            a = (_silu(g) * u).astype(BF16)
            y = jnp.dot(a, wd_s[...], preferred_element_type=F32)
            o_ref[r:r + FFN_CHUNK, :] = x + (0.5 * gate) * y


def _ffn(x2d, mod, wg, wu, wd, *, sub):
    t = x2d.shape[0]
    tile = lambda i: jnp.maximum(i - W_STEPS, 0)
    return pl.pallas_call(
        functools.partial(_ffn_kernel, sub=sub),
        grid=(W_STEPS + t // TM_FFN,),
        in_specs=[
            pl.BlockSpec((TM_FFN, D_MODEL), lambda i: (tile(i), 0)),
            _const_spec((N_MOD, BATCH, D_MODEL)),
            _weight_rows_spec(wg.shape, W_STEPS),
            _weight_rows_spec(wu.shape, W_STEPS),
            _weight_rows_spec(wd.shape, W_STEPS),
        ],
        out_specs=pl.BlockSpec((TM_FFN, D_MODEL), lambda i: (tile(i), 0)),
        out_shape=jax.ShapeDtypeStruct((t, D_MODEL), F32),
        scratch_shapes=[pltpu.VMEM(wg.shape, BF16), pltpu.VMEM(wu.shape, BF16),
                        pltpu.VMEM(wd.shape, BF16)],
        compiler_params=pltpu.CompilerParams(
            dimension_semantics=("arbitrary",), vmem_limit_bytes=VMEM_LIMIT),
        name=f"ffn{sub}",
    )(x2d, mod, wg, wu, wd)


def _chunk_rmsnorm_rows(xt, gain_rows):
    width, tm = xt.shape
    x3 = xt.reshape(width // HEAD_DIM, HEAD_DIM, tm)
    ms = jnp.mean(x3 * x3, axis=1, keepdims=True)
    xn = (x3 * lax.rsqrt(ms + EPS)).reshape(width, tm)
    return jnp.concatenate(
        [xn[:, j * LANES:(j + 1) * LANES] * gain_rows for j in range(tm // LANES)], axis=1)


def _inproj_tile(x_ref, shift, scale, wqkvt_ref, wrest_ref, qg_ref, kg_ref,
                 ws_ref, bs_ref,
                 qt_ref, k_ref, vt_ref, ub_ref, ga_ref, gb_ref):
    row = lax.broadcasted_iota(jnp.int32, (CHUNK, CHUNK), 0)
    col = lax.broadcasted_iota(jnp.int32, (CHUNK, CHUNK), 1)
    w_spatial = [jnp.where(row >= col, ws_ref[g], 0.0).astype(BF16)
                 for g in range(N_GROUPS)]
    for c in range(TM_PROJ // TQ):
        rows = slice(c * TQ, (c + 1) * TQ)
        x = x_ref[rows, :]
        h = _modulated_rmsnorm(x, shift, scale).astype(BF16)

        pt = lax.dot_general(wqkvt_ref[...], h, (((1,), (1,)), ((), ())),
                             preferred_element_type=F32)
        qt = _chunk_rmsnorm_rows(pt[0:QK_WIDTH], qg_ref[...])
        qt_ref[0, c] = qt.astype(BF16)
        kt = _chunk_rmsnorm_rows(pt[QK_WIDTH:2 * QK_WIDTH], kg_ref[...])
        k_ref[0, rows, :] = kt.T.astype(BF16)
        vt_ref[0, c] = pt[2 * QK_WIDTH:QKV_COLS].astype(BF16)

        rest = jnp.dot(h, wrest_ref[...], preferred_element_type=F32)
        u = _gelu(rest[:, 0:GMLP_WIDTH])
        gv = _gelu(rest[:, GMLP_WIDTH:2 * GMLP_WIDTH])
        ga_ref[0, rows, :] = jax.nn.sigmoid(
            rest[:, 2 * GMLP_WIDTH:2 * GMLP_WIDTH + D_MODEL]).astype(BF16)
        gb_ref[0, rows, :] = jax.nn.sigmoid(
            rest[:, 2 * GMLP_WIDTH + D_MODEL:REST_COLS]).astype(BF16)

        mu = jnp.mean(gv, axis=-1, keepdims=True)
        cen = gv - mu
        var = jnp.mean(cen * cen, axis=-1, keepdims=True)
        vln = ((cen * lax.rsqrt(var + EPS)) * bs_ref[CHUNK:CHUNK + 1, :]
               + bs_ref[CHUNK + 1:CHUNK + 2, :]).astype(BF16)

        for g in range(N_GROUPS):
            cs = slice(g * GROUP_DIM, (g + 1) * GROUP_DIM)
            for n in range(TQ // CHUNK):
                rs = slice(n * CHUNK, (n + 1) * CHUNK)
                f = (jnp.dot(w_spatial[g], vln[rs, cs], preferred_element_type=F32)
                     + bs_ref[0:CHUNK, cs])
                ub_ref[0, c * TQ + n * CHUNK:c * TQ + (n + 1) * CHUNK, cs] = (
                    u[rs, cs] * f).astype(BF16)


def _inproj_kernel(x_ref, mod_ref, win_ref, qg_ref, kg_ref, ws_ref, bs_ref,
                   qt_ref, k_ref, vt_ref, ub_ref, ga_ref, gb_ref, wqkvt_s, wrest_s):
    i = pl.program_id(0)

    @pl.when(i < W_STEPS_PROJ)
    def _():
        rows = win_ref.shape[0]
        r = pl.multiple_of(i * rows, rows)
        chunk = win_ref[...]
        wqkvt_s[:, pl.ds(r, rows)] = chunk[:, 0:QKV_COLS].T.astype(BF16)
        wrest_s[pl.ds(r, rows), :] = chunk[:, QKV_COLS:QKV_COLS + REST_COLS].astype(BF16)

    @pl.when(i >= W_STEPS_PROJ)
    def _():
        b = lax.div(i - W_STEPS_PROJ, SEQ // TM_PROJ)
        _inproj_tile(x_ref, _mod_row(mod_ref, 3, b), _mod_row(mod_ref, 4, b), wqkvt_s, wrest_s,
                     qg_ref, kg_ref, ws_ref, bs_ref,
                     qt_ref, k_ref, vt_ref, ub_ref, ga_ref, gb_ref)


def _inproj(x3d, mod, w_in, qg_rows, kg_rows, w_spatial, bs_full):
    nt = SEQ // TM_PROJ
    tile = lambda i: jnp.maximum(i - W_STEPS_PROJ, 0)
    tok_spec = lambda width: pl.BlockSpec(
        (1, TM_PROJ, width), lambda i: (tile(i) // nt, tile(i) % nt, 0))
    feat_spec = lambda width: pl.BlockSpec(
        (1, TM_PROJ // TQ, width, TQ), lambda i: (tile(i) // nt, tile(i) % nt, 0, 0))
    return pl.pallas_call(
        _inproj_kernel,
        grid=(W_STEPS_PROJ + BATCH * nt,),
        in_specs=[
            pl.BlockSpec((None, TM_PROJ, D_MODEL), lambda i: (tile(i) // nt, tile(i) % nt, 0)),
            _const_spec((N_MOD, BATCH, D_MODEL)),
            _weight_rows_spec(w_in.shape, W_STEPS_PROJ),
            _const_spec((QK_WIDTH, LANES)),
            _const_spec((QK_WIDTH, LANES)),
            _const_spec((N_GROUPS, CHUNK, CHUNK)),
            _const_spec((CHUNK + 8, GMLP_WIDTH)),
        ],
        out_specs=[
            feat_spec(QK_WIDTH), tok_spec(QK_WIDTH), feat_spec(ATTN_WIDTH),
            tok_spec(GMLP_WIDTH), tok_spec(D_MODEL), tok_spec(D_MODEL),
        ],
        out_shape=[
            jax.ShapeDtypeStruct((BATCH, SEQ // TQ, QK_WIDTH, TQ), BF16),
            jax.ShapeDtypeStruct((BATCH, SEQ, QK_WIDTH), BF16),
            jax.ShapeDtypeStruct((BATCH, SEQ // TQ, ATTN_WIDTH, TQ), BF16),
            jax.ShapeDtypeStruct((BATCH, SEQ, GMLP_WIDTH), BF16),
            jax.ShapeDtypeStruct((BATCH, SEQ, D_MODEL), BF16),
            jax.ShapeDtypeStruct((BATCH, SEQ, D_MODEL), BF16),
        ],
        scratch_shapes=[pltpu.VMEM((QKV_COLS, D_MODEL), BF16),
                        pltpu.VMEM((D_MODEL, REST_COLS), BF16)],
        compiler_params=pltpu.CompilerParams(
            dimension_semantics=("arbitrary",), vmem_limit_bytes=VMEM_LIMIT),
        name="inproj",
    )(x3d, mod, w_in, qg_rows, kg_rows, w_spatial, bs_full)


def _rel_buckets(dist):
    n = np.maximum(dist, 0)
    max_exact = N_BUCKETS // 2
    nf = np.maximum(n, 1).astype(np.float32)
    large = max_exact + (np.log(nf / np.float32(max_exact))
                         / np.float32(math.log(MAX_DISTANCE / max_exact))
                         * np.float32(N_BUCKETS - max_exact)).astype(np.int32)
    large = np.minimum(large, N_BUCKETS - 1)
    return np.where(n < max_exact, n, large).astype(np.int32)


def _attn_kernel(table_ref, bound_ref, lam_ref, qt_ref, k_ref, vt_ref, bkt_ref, subg_ref,
                 o_ref, bias_ref, s_scr):
    b = pl.program_id(1)

    @pl.when(b == 0)
    def _():
        heads = [pl.program_id(0) * HEADS_PER_STEP + hh for hh in range(HEADS_PER_STEP)]

        def lookup(bkt):
            tiles = [jnp.zeros(bkt.shape, F32)] * HEADS_PER_STEP
            for bucket in range(N_BUCKETS - 1):
                hit = bkt == bucket
                tiles = [jnp.where(hit, (table_ref[bucket, h] - table_ref[N_BUCKETS - 1, h]) * LOG2E, t)
                         for h, t in zip(heads, tiles)]
            return tiles

        key = lax.broadcasted_iota(jnp.int32, (TQ, TQ), 0)
        qry = lax.broadcasted_iota(jnp.int32, (TQ, TQ), 1)
        half = TQ // 2
        z = jnp.zeros((half, half), F32)
        diag = lookup(bkt_ref[0])
        quad = lookup(bkt_ref[1, half:TQ, 0:half])
        for hh in range(HEADS_PER_STEP):
            tile = jnp.where(key <= qry, diag[hh], -jnp.inf)
            bias_ref[hh, 0] = jnp.concatenate([tile, tile], axis=1)
            tile = jnp.concatenate([jnp.concatenate([z, z], axis=1),
                                    jnp.concatenate([quad[hh], z], axis=1)], axis=0)
            bias_ref[hh, 1] = jnp.concatenate([tile, tile], axis=1)

    @pl.when(bound_ref[0, 0] <= MAX_SAFE_LOG2)
    def _():
        _attn_program(lam_ref, subg_ref, qt_ref, k_ref, vt_ref, o_ref, bias_ref, s_scr,
                      track_max=False)

    @pl.when(jnp.logical_not(bound_ref[0, 0] <= MAX_SAFE_LOG2))
    def _():
        _attn_program(lam_ref, subg_ref, qt_ref, k_ref, vt_ref, o_ref, bias_ref, s_scr,
                      track_max=True)


def _attn_program(lam_ref, subg_ref, qt_ref, k_ref, vt_ref, o_ref, bias_ref, s_scr, *, track_max):
    lp = lam_ref[...]
    lam = (jnp.exp(jnp.sum(lp[0:1] * lp[1:2], axis=1, keepdims=True))
           - jnp.exp(jnp.sum(lp[2:3] * lp[3:4], axis=1, keepdims=True)) + LAM_INIT)
    subg = subg_ref[...]
    zeros = jnp.zeros((HEAD_DIM, TQ), BF16)
    ones = jnp.ones((SUM_ROWS, TQ), BF16)
    n_strips = 2 * TQ // LANES
    nq = SEQ // TQ
    blocks = [(hh, qi, j) for hh in range(HEADS_PER_STEP) for qi in range(nq) for j in range(qi + 1)]
    qpads = {}

    def head_cols(hh, width):
        return slice(hh * width, (hh + 1) * width)

    def keys(hh, j):
        return k_ref[0, j * TQ:(j + 1) * TQ, head_cols(hh, 2 * HEAD_DIM)]

    def qpad(hh, qi):
        if (hh, qi) not in qpads:
            qt = qt_ref[0, qi, head_cols(hh, 2 * HEAD_DIM), :]
            qpads[hh, qi] = jnp.concatenate(
                [jnp.concatenate([qt[0:HEAD_DIM], zeros], axis=0),
                 jnp.concatenate([zeros, qt[HEAD_DIM:2 * HEAD_DIM]], axis=0)], axis=1)
        return qpads[hh, qi]

    def scores(hh, qi, j):
        return jnp.dot(keys(hh, j), qpad(hh, qi), preferred_element_type=F32)

    def probs_tracked(s, hh, qi, j, m):
        p, alpha, m_out = [], [], []
        for c in range(n_strips):
            sc = s[:, c * LANES:(c + 1) * LANES]
            if j == qi:
                sc = sc + bias_ref[hh, 0, :, c * LANES:(c + 1) * LANES]
            elif j == qi - 1:
                sc = sc + bias_ref[hh, 1, :, c * LANES:(c + 1) * LANES]
            blk_max = jnp.max(sc, axis=0, keepdims=True)
            m_new = blk_max if j == 0 else jnp.maximum(m[c], blk_max)
            p.append(jnp.exp2(sc - m_new).astype(BF16))
            alpha.append(None if j == 0 else jnp.exp2(m[c] - m_new))
            m_out.append(m_new)
        return jnp.concatenate(p, axis=1), alpha, m_out

    half = TQ // 2
    hi_cols = (slice(half, TQ), slice(TQ + half, 2 * TQ))

    def issue_bounded(n):
        hh, qi, j = blocks[n]
        slot = s_scr.at[n % S_SLOTS]
        if j < qi:
            slot[...] = scores(hh, qi, j)
            return
        kb = keys(hh, j)
        qp = qpad(hh, qi)
        slot[0:half, :] = jnp.dot(kb[0:half], qp, preferred_element_type=F32)
        s_hi = jnp.dot(kb[half:TQ], jnp.concatenate([qp[:, c] for c in hi_cols], axis=1),
                       preferred_element_type=F32)
        for t, c in enumerate(hi_cols):
            slot[half:TQ, c] = s_hi[:, t * half:(t + 1) * half]

    def probs_bounded(n):
        hh, qi, j = blocks[n]
        slot = s_scr.at[n % S_SLOTS]
        if j < qi - 1:
            return jnp.exp2(slot[...])
        if j == qi - 1:
            return jnp.exp2(slot[...] + bias_ref[hh, 1])
        lo = jnp.exp2(slot[0:half, :] + bias_ref[hh, 0, 0:half, :])
        z = jnp.zeros((half, half), F32)
        hi = [jnp.exp2(slot[half:TQ, c] + bias_ref[hh, 0, half:TQ, c]) for c in hi_cols]
        return jnp.concatenate([lo, jnp.concatenate([z, hi[0], z, hi[1]], axis=1)], axis=0)

    ahead = QK_AHEAD_TRACKED if track_max else QK_AHEAD_BOUNDED

    issue = (lambda n: scores(*blocks[n])) if track_max else issue_bounded
    pending = [issue(n) for n in range(ahead)]
    m = acc = psum = None
    for n, (hh, qi, j) in enumerate(blocks):
        s = pending.pop(0)
        if n + ahead < len(blocks):
            pending.append(issue(n + ahead))
        vt = vt_ref[0, j, head_cols(hh, V_DIM), :]
        if track_max:
            p, alpha, m = probs_tracked(s, hh, qi, j, m)
            vt_aug = jnp.concatenate([vt, ones], axis=0)
            pv = jnp.dot(vt_aug, p, preferred_element_type=F32)[0:V_DIM + 8]
            if j == 0:
                acc = pv
            else:
                acc = jnp.concatenate(
                    [alpha[c] * acc[:, c * LANES:(c + 1) * LANES] + pv[:, c * LANES:(c + 1) * LANES]
                     for c in range(n_strips)], axis=1)
        else:
            p = probs_bounded(n)
            blk_sum = jnp.sum(p, axis=0, keepdims=True)
            pv = jnp.dot(vt, p.astype(BF16), preferred_element_type=F32)
            acc = pv if j == 0 else acc + pv
            psum = blk_sum if j == 0 else psum + blk_sum
        if j == qi:
            inv = 1.0 / (acc[V_DIM:V_DIM + 1] if track_max else psum)
            ot = (acc[0:V_DIM, 0:TQ] * inv[:, 0:TQ]
                  - acc[0:V_DIM, TQ:2 * TQ] * (lam * inv[:, TQ:2 * TQ]))
            ms = jnp.mean(ot * ot, axis=0, keepdims=True)
            on = (ot * lax.rsqrt(ms + EPS)).T
            o_ref[0, qi * TQ:(qi + 1) * TQ, head_cols(hh, V_DIM)] = (on * subg).astype(BF16)


def _attention(table, bound, lam_rows, qt, k, vt, buckets, sub_gain):
    return pl.pallas_call(
        _attn_kernel,
        grid=(N_HEADS // HEADS_PER_STEP, BATCH),
        in_specs=[
            pl.BlockSpec(memory_space=pltpu.SMEM),
            pl.BlockSpec(memory_space=pltpu.SMEM),
            _const_spec((4, HEAD_DIM)),
            pl.BlockSpec((1, SEQ // TQ, HEADS_PER_STEP * 2 * HEAD_DIM, TQ),
                         lambda g, b: (b, 0, g, 0)),
            pl.BlockSpec((1, SEQ, HEADS_PER_STEP * 2 * HEAD_DIM), lambda g, b: (b, 0, g)),
            pl.BlockSpec((1, SEQ // TQ, HEADS_PER_STEP * V_DIM, TQ), lambda g, b: (b, 0, g, 0)),
            _const_spec((2, TQ, TQ)),
            _const_spec((1, V_DIM)),
        ],
        out_specs=pl.BlockSpec((1, SEQ, HEADS_PER_STEP * V_DIM), lambda g, b: (b, 0, g)),
        out_shape=jax.ShapeDtypeStruct((BATCH, SEQ, ATTN_WIDTH), BF16),
        scratch_shapes=[pltpu.VMEM((HEADS_PER_STEP, 2, TQ, 2 * TQ), F32),
                        pltpu.VMEM((S_SLOTS, TQ, 2 * TQ), F32)],
        compiler_params=pltpu.CompilerParams(
            dimension_semantics=("arbitrary", "arbitrary"),
            vmem_limit_bytes=VMEM_LIMIT),
        name="diff_attn",
    )(table, bound, lam_rows, qt, k, vt, buckets, sub_gain)


def _mix_kernel(x_ref, mod_ref, o_ref, ub_ref, ga_ref, gb_ref, wa_ref, wb_ref, wo_ref,
                out_ref, wa_s, wb_s, wo_s):
    i = pl.program_id(0)

    @pl.when(i < W_STEPS_MIX)
    def _():
        _stash_rows(wa_s, wa_ref, i)
        _stash_rows(wb_s, wb_ref, i)
        _stash_rows(wo_s, wo_ref, i)

    @pl.when(i >= W_STEPS_MIX)
    def _():
        gate = _mod_row(mod_ref, 5, lax.div(i - W_STEPS_MIX, SEQ // TM_MIX))
        for r in range(0, TM_MIX, MIX_CHUNK):
            rows = slice(r, r + MIX_CHUNK)
            ya = jnp.dot(o_ref[rows, :], wa_s[...], preferred_element_type=F32)
            yb = jnp.dot(ub_ref[rows, :], wb_s[...], preferred_element_type=F32)
            merged = (ga_ref[rows, :].astype(F32) * ya
                      + gb_ref[rows, :].astype(F32) * yb).astype(BF16)
            z = jnp.dot(merged, wo_s[...], preferred_element_type=F32)
            out_ref[rows, :] = x_ref[rows, :] + gate * z


def _mix(x2d, mod, o2d, ub2d, ga2d, gb2d, wa, wb, wo):
    t = x2d.shape[0]
    tile = lambda i: jnp.maximum(i - W_STEPS_MIX, 0)
    tok = lambda: pl.BlockSpec((TM_MIX, D_MODEL), lambda i: (tile(i), 0))
    return pl.pallas_call(
        _mix_kernel,
        grid=(W_STEPS_MIX + t // TM_MIX,),
        in_specs=[
            tok(),
            _const_spec((N_MOD, BATCH, D_MODEL)),
            tok(), tok(), tok(), tok(),
            _weight_rows_spec(wa.shape, W_STEPS_MIX),
            _weight_rows_spec(wb.shape, W_STEPS_MIX),
            _weight_rows_spec(wo.shape, W_STEPS_MIX),
        ],
        out_specs=tok(),
        out_shape=jax.ShapeDtypeStruct((t, D_MODEL), F32),
        scratch_shapes=[pltpu.VMEM(wa.shape, BF16), pltpu.VMEM(wb.shape, BF16),
                        pltpu.VMEM(wo.shape, BF16)],
        compiler_params=pltpu.CompilerParams(
            dimension_semantics=("arbitrary",), vmem_limit_bytes=VMEM_LIMIT),
        name="mix_out",
    )(x2d, mod, o2d, ub2d, ga2d, gb2d, wa, wb, wo)


def kernel(x, c, w_ada, b_ada, w_ffn1_gate, w_ffn1_up, w_ffn1_down, w_in, q_norm_g, k_norm_g, lam_q1, lam_k1, lam_q2, lam_k2, subln_g, rel_bias_table, gmlp_ln_g, gmlp_ln_b, w_spatial, b_spatial, w_a_proj, w_b_proj, w_o, w_ffn2_gate, w_ffn2_up, w_ffn2_down):
    b, s, d = x.shape
    t = b * s

    mod, qg_rows, kg_rows, bs_full, score_bound, lam_rows, sub_gain = _adaln_mod(
        c, w_ada[0], b_ada, q_norm_g, k_norm_g, b_spatial[0], rel_bias_table,
        lam_q1, lam_k1, lam_q2, lam_k2, subln_g, gmlp_ln_g, gmlp_ln_b)

    x1 = _ffn(x.reshape(t, d), mod, w_ffn1_gate[0], w_ffn1_up[0], w_ffn1_down[0], sub=0)

    qt, k, vt, ub, ga, gb = _inproj(
        x1.reshape(b, s, d), mod, w_in[0], qg_rows, kg_rows, w_spatial[0], bs_full)

    kk = np.arange(TQ)[:, None]
    qq = np.arange(TQ)[None, :]
    near = _rel_buckets(TQ + qq - kk)
    far_only = near == N_BUCKETS - 1
    far_only[TQ // 2:, :TQ // 2] = True
    assert far_only.all(), "previous-block bias must be confined to one quadrant"
    buckets = jnp.asarray(np.stack([_rel_buckets(qq - kk), near]))
    o = _attention(rel_bias_table, score_bound, lam_rows, qt, k, vt, buckets, sub_gain)

    x2 = _mix(x1, mod, o.reshape(t, ATTN_WIDTH), ub.reshape(t, GMLP_WIDTH),
              ga.reshape(t, d), gb.reshape(t, d), w_a_proj[0], w_b_proj[0], w_o[0])

    x3 = _ffn(x2, mod, w_ffn2_gate[0], w_ffn2_up[0], w_ffn2_down[0], sub=2)
    return x3.reshape(b, s, d)
```

```python
import functools
import math

import numpy as np
import jax
import jax.numpy as jnp
from jax import lax
from jax.experimental import pallas as pl
from jax.experimental.pallas import tpu as pltpu

F32 = jnp.float32
BF16 = jnp.bfloat16

D_MODEL = 1024
BATCH = 8
SEQ = 2048
N_HEADS = 8
HEAD_DIM = 64
V_DIM = 2 * HEAD_DIM
QK_WIDTH = N_HEADS * 2 * HEAD_DIM
ATTN_WIDTH = N_HEADS * V_DIM
N_GROUPS = 8
CHUNK = 128
GMLP_WIDTH = 1024
GROUP_DIM = GMLP_WIDTH // N_GROUPS
N_BUCKETS = 32
MAX_DISTANCE = 128
D_FF = 2816
N_MOD = 9
EPS = 1e-6
LAM_INIT = 0.8 - 0.6 * math.exp(-0.3 * 0)
LOG2E = math.log2(math.e)
QKV_COLS = 2 * QK_WIDTH + ATTN_WIDTH
REST_COLS = 2 * GMLP_WIDTH + 2 * D_MODEL

LANES = 128
VMEM_LIMIT = 56 * 1024 * 1024

FFN_CHUNK = 256
TM_FFN = 1024
TM_PROJ = 512
FUSED_CHUNK = 256
TM_FUSED = 512
TQ = 256
SUM_ROWS = 16
W_STEPS = 8
W_STEPS_PROJ = 4
W_STEPS_FUSED = 16
MAX_SAFE_LOG2 = 64.0
BF16_SLACK = 1.02
QK_AHEAD_TRACKED = 2
QK_AHEAD_BOUNDED = 3
S_SLOTS = QK_AHEAD_BOUNDED + 1
HEADS_PER_STEP = 2
MOD_ROWS_PER_STEP = 3


def _const_spec(shape):
    nd = len(shape)
    return pl.BlockSpec(shape, lambda *_: (0,) * nd, pipeline_mode=pl.Buffered(1))


def _weight_rows_spec(shape, steps):
    rows, cols = shape
    return pl.BlockSpec((rows // steps, cols), lambda i: (jnp.minimum(i, steps - 1), 0))


def _stash_rows(dst_ref, chunk_ref, i):
    rows = chunk_ref.shape[0]
    dst_ref[pl.ds(pl.multiple_of(i * rows, rows), rows), :] = chunk_ref[...].astype(BF16)


def _silu(x):
    return x * jax.nn.sigmoid(x)


def _gelu(x):
    return 0.5 * x * (1.0 + lax.erf(x * (1.0 / math.sqrt(2.0))))


def _modulated_rmsnorm(x, shift, scale):
    ms = jnp.mean(x * x, axis=-1, keepdims=True)
    return (x * lax.rsqrt(ms + EPS)) * (1.0 + scale) + shift


def _as_column(row):
    n = row.shape[1]
    r = lax.broadcasted_iota(jnp.int32, (n, n), 0)
    c = lax.broadcasted_iota(jnp.int32, (n, n), 1)
    return jnp.sum(jnp.where(r == c, row, 0.0), axis=1, keepdims=True)


def _mod_kernel(c_ref, w_ref, b_ref, qg_ref, kg_ref, bsp_ref, table_ref,
                lq1_ref, lk1_ref, lq2_ref, lk2_ref, subg_ref,
                o_ref, qgr_ref, kgr_ref, bsf_ref, bound_ref, lam_ref, subs_ref):
    a = _silu(c_ref[...]).astype(BF16)
    m = jnp.dot(a, w_ref[...].astype(BF16), preferred_element_type=F32) + b_ref[...]
    for r in range(MOD_ROWS_PER_STEP):
        o_ref[r] = m[:, r * D_MODEL:(r + 1) * D_MODEL]

    @pl.when(pl.program_id(0) == 0)
    def _():
        qg = qg_ref[...] * (HEAD_DIM ** -0.5 * LOG2E)
        kg = kg_ref[...]
        reps = QK_WIDTH // HEAD_DIM
        qgr_ref[...] = jnp.concatenate(
            [jnp.broadcast_to(_as_column(qg), (HEAD_DIM, LANES))] * reps, axis=0)
        kgr_ref[...] = jnp.concatenate(
            [jnp.broadcast_to(_as_column(kg), (HEAD_DIM, LANES))] * reps, axis=0)
        bsf_ref[...] = jnp.concatenate(
            [jnp.broadcast_to(_as_column(bsp_ref[g:g + 1, :]), (CHUNK, GROUP_DIM))
             for g in range(N_GROUPS)], axis=1)
        rel = jnp.float32(0.0)
        for h in range(N_HEADS):
            for bucket in range(N_BUCKETS - 1):
                rel = jnp.maximum(
                    rel, jnp.abs(table_ref[bucket, h] - table_ref[N_BUCKETS - 1, h]))
        bound_ref[...] = (
            HEAD_DIM * BF16_SLACK * jnp.max(jnp.abs(qg), axis=(0, 1), keepdims=True)
            * jnp.max(jnp.abs(kg), axis=(0, 1), keepdims=True)
            + LOG2E * rel)
        lam_ref[...] = jnp.concatenate(
            [lq1_ref[...], lk1_ref[...], lq2_ref[...], lk2_ref[...]], axis=0)
        subs_ref[...] = subg_ref[...] * (1.0 - LAM_INIT)


def _adaln_mod(c, w_ada, b_ada, q_norm_g, k_norm_g, b_spatial, rel_bias_table,
               lam_q1, lam_k1, lam_q2, lam_k2, subln_g):
    n = w_ada.shape[1]
    whole = lambda shape: pl.BlockSpec(shape, lambda j: (0,) * len(shape))
    return pl.pallas_call(
        _mod_kernel,
        grid=(n // (MOD_ROWS_PER_STEP * D_MODEL),),
        in_specs=[
            whole((BATCH, D_MODEL)),
            pl.BlockSpec((D_MODEL, MOD_ROWS_PER_STEP * D_MODEL), lambda j: (0, j)),
            pl.BlockSpec((1, MOD_ROWS_PER_STEP * D_MODEL), lambda j: (0, j)),
            whole((1, HEAD_DIM)), whole((1, HEAD_DIM)),
            whole((N_GROUPS, CHUNK)), pl.BlockSpec(memory_space=pltpu.SMEM),
            whole((1, HEAD_DIM)), whole((1, HEAD_DIM)), whole((1, HEAD_DIM)), whole((1, HEAD_DIM)),
            whole((1, V_DIM)),
        ],
        out_specs=[
            pl.BlockSpec((MOD_ROWS_PER_STEP, BATCH, D_MODEL), lambda j: (j, 0, 0)),
            whole((QK_WIDTH, LANES)), whole((QK_WIDTH, LANES)),
            whole((CHUNK, GMLP_WIDTH)), whole((1, 1)), whole((4, HEAD_DIM)), whole((1, V_DIM)),
        ],
        out_shape=[
            jax.ShapeDtypeStruct((n // D_MODEL, BATCH, D_MODEL), F32),
            jax.ShapeDtypeStruct((QK_WIDTH, LANES), F32),
            jax.ShapeDtypeStruct((QK_WIDTH, LANES), F32),
            jax.ShapeDtypeStruct((CHUNK, GMLP_WIDTH), F32),
            jax.ShapeDtypeStruct((1, 1), F32),
            jax.ShapeDtypeStruct((4, HEAD_DIM), F32),
            jax.ShapeDtypeStruct((1, V_DIM), F32),
        ],
        compiler_params=pltpu.CompilerParams(
            dimension_semantics=("arbitrary",), vmem_limit_bytes=VMEM_LIMIT),
        name="adaln_mod",
    )(c, w_ada, b_ada, q_norm_g, k_norm_g, b_spatial, rel_bias_table,
      lam_q1, lam_k1, lam_q2, lam_k2, subln_g)


def _mod_row(mod_ref, j, b):
    return mod_ref[j, pl.ds(b, 1), :]


def _ffn_kernel(x_ref, mod_ref, wg_ref, wu_ref, wd_ref, o_ref, wg_s, wu_s, wd_s, *, sub):
    i = pl.program_id(0)

    @pl.when(i < W_STEPS)
    def _():
        _stash_rows(wg_s, wg_ref, i)
        _stash_rows(wu_s, wu_ref, i)
        _stash_rows(wd_s, wd_ref, i)

    @pl.when(i >= W_STEPS)
    def _():
        b = lax.div(i - W_STEPS, SEQ // TM_FFN)
        shift = _mod_row(mod_ref, 3 * sub + 0, b)
        scale = _mod_row(mod_ref, 3 * sub + 1, b)
        gate = _mod_row(mod_ref, 3 * sub + 2, b)
        for r in range(0, TM_FFN, FFN_CHUNK):
            x = x_ref[r:r + FFN_CHUNK, :]
            h = _modulated_rmsnorm(x, shift, scale).astype(BF16)
            g = jnp.dot(h, wg_s[...], preferred_element_type=F32)
            u = jnp.dot(h, wu_s[...], preferred_element_type=F32)
            a = (_silu(g) * u).astype(BF16)
            y = jnp.dot(a, wd_s[...], preferred_element_type=F32)
            o_ref[r:r + FFN_CHUNK, :] = x + (0.5 * gate) * y


def _ffn(x2d, mod, wg, wu, wd, *, sub):
    t = x2d.shape[0]
    tile = lambda i: jnp.maximum(i - W_STEPS, 0)
    return pl.pallas_call(
        functools.partial(_ffn_kernel, sub=sub),
        grid=(W_STEPS + t // TM_FFN,),
        in_specs=[
            pl.BlockSpec((TM_FFN, D_MODEL), lambda i: (tile(i), 0)),
            _const_spec((N_MOD, BATCH, D_MODEL)),
            _weight_rows_spec(wg.shape, W_STEPS),
            _weight_rows_spec(wu.shape, W_STEPS),
            _weight_rows_spec(wd.shape, W_STEPS),
        ],
        out_specs=pl.BlockSpec((TM_FFN, D_MODEL), lambda i: (tile(i), 0)),
        out_shape=jax.ShapeDtypeStruct((t, D_MODEL), F32),
        scratch_shapes=[pltpu.VMEM(wg.shape, BF16), pltpu.VMEM(wu.shape, BF16),
                        pltpu.VMEM(wd.shape, BF16)],
        compiler_params=pltpu.CompilerParams(
            dimension_semantics=("arbitrary",), vmem_limit_bytes=VMEM_LIMIT),
        name=f"ffn{sub}",
    )(x2d, mod, wg, wu, wd)


def _chunk_rmsnorm_rows(xt, gain_rows):
    width, tm = xt.shape
    x3 = xt.reshape(width // HEAD_DIM, HEAD_DIM, tm)
    ms = jnp.mean(x3 * x3, axis=1, keepdims=True)
    xn = (x3 * lax.rsqrt(ms + EPS)).reshape(width, tm)
    return jnp.concatenate(
        [xn[:, j * LANES:(j + 1) * LANES] * gain_rows for j in range(tm // LANES)], axis=1)


def _inproj_tile(x_ref, shift, scale, wqkvt_ref, wrest_ref, qg_ref, kg_ref,
                 lng_ref, lnb_ref, ws_ref, bs_ref,
                 qt_ref, k_ref, vt_ref, ub_ref, ga_ref, gb_ref):
    row = lax.broadcasted_iota(jnp.int32, (CHUNK, CHUNK), 0)
    col = lax.broadcasted_iota(jnp.int32, (CHUNK, CHUNK), 1)
    w_spatial = [jnp.where(row >= col, ws_ref[g], 0.0).astype(BF16)
                 for g in range(N_GROUPS)]
    for c in range(TM_PROJ // TQ):
        rows = slice(c * TQ, (c + 1) * TQ)
        x = x_ref[rows, :]
        h = _modulated_rmsnorm(x, shift, scale).astype(BF16)

        pt = lax.dot_general(wqkvt_ref[...], h, (((1,), (1,)), ((), ())),
                             preferred_element_type=F32)
        qt = _chunk_rmsnorm_rows(pt[0:QK_WIDTH], qg_ref[...])
        qt_ref[0, c] = qt.astype(BF16)
        kt = _chunk_rmsnorm_rows(pt[QK_WIDTH:2 * QK_WIDTH], kg_ref[...])
        k_ref[0, rows, :] = kt.T.astype(BF16)
        vt_ref[0, c] = pt[2 * QK_WIDTH:QKV_COLS].astype(BF16)

        rest = jnp.dot(h, wrest_ref[...], preferred_element_type=F32)
        u = _gelu(rest[:, 0:GMLP_WIDTH])
        gv = _gelu(rest[:, GMLP_WIDTH:2 * GMLP_WIDTH])
        ga_ref[0, rows, :] = jax.nn.sigmoid(
            rest[:, 2 * GMLP_WIDTH:2 * GMLP_WIDTH + D_MODEL]).astype(BF16)
        gb_ref[0, rows, :] = jax.nn.sigmoid(
            rest[:, 2 * GMLP_WIDTH + D_MODEL:REST_COLS]).astype(BF16)

        mu = jnp.mean(gv, axis=-1, keepdims=True)
        cen = gv - mu
        var = jnp.mean(cen * cen, axis=-1, keepdims=True)
        vln = ((cen * lax.rsqrt(var + EPS)) * lng_ref[...] + lnb_ref[...]).astype(BF16)

        for g in range(N_GROUPS):
            cs = slice(g * GROUP_DIM, (g + 1) * GROUP_DIM)
            for n in range(TQ // CHUNK):
                rs = slice(n * CHUNK, (n + 1) * CHUNK)
                f = jnp.dot(w_spatial[g], vln[rs, cs], preferred_element_type=F32) + bs_ref[:, cs]
                ub_ref[0, c * TQ + n * CHUNK:c * TQ + (n + 1) * CHUNK, cs] = (
                    u[rs, cs] * f).astype(BF16)


def _inproj_kernel(x_ref, mod_ref, win_ref, qg_ref, kg_ref, lng_ref, lnb_ref, ws_ref, bs_ref,
                   qt_ref, k_ref, vt_ref, ub_ref, ga_ref, gb_ref, wqkvt_s, wrest_s):
    i = pl.program_id(0)

    @pl.when(i < W_STEPS_PROJ)
    def _():
        rows = win_ref.shape[0]
        r = pl.multiple_of(i * rows, rows)
        chunk = win_ref[...]
        wqkvt_s[:, pl.ds(r, rows)] = chunk[:, 0:QKV_COLS].T.astype(BF16)
        wrest_s[pl.ds(r, rows), :] = chunk[:, QKV_COLS:QKV_COLS + REST_COLS].astype(BF16)

    @pl.when(i >= W_STEPS_PROJ)
    def _():
        b = lax.div(i - W_STEPS_PROJ, SEQ // TM_PROJ)
        _inproj_tile(x_ref, _mod_row(mod_ref, 3, b), _mod_row(mod_ref, 4, b), wqkvt_s, wrest_s,
                     qg_ref, kg_ref, lng_ref, lnb_ref, ws_ref, bs_ref,
                     qt_ref, k_ref, vt_ref, ub_ref, ga_ref, gb_ref)


def _inproj(x3d, mod, w_in, qg_rows, kg_rows, ln_g, ln_b, w_spatial, bs_full):
    nt = SEQ // TM_PROJ
    tile = lambda i: jnp.maximum(i - W_STEPS_PROJ, 0)
    tok_spec = lambda width: pl.BlockSpec(
        (1, TM_PROJ, width), lambda i: (tile(i) // nt, tile(i) % nt, 0))
    feat_spec = lambda width: pl.BlockSpec(
        (1, TM_PROJ // TQ, width, TQ), lambda i: (tile(i) // nt, tile(i) % nt, 0, 0))
    return pl.pallas_call(
        _inproj_kernel,
        grid=(W_STEPS_PROJ + BATCH * nt,),
        in_specs=[
            pl.BlockSpec((None, TM_PROJ, D_MODEL), lambda i: (tile(i) // nt, tile(i) % nt, 0)),
            _const_spec((N_MOD, BATCH, D_MODEL)),
            _weight_rows_spec(w_in.shape, W_STEPS_PROJ),
            _const_spec((QK_WIDTH, LANES)),
            _const_spec((QK_WIDTH, LANES)),
            _const_spec((1, GMLP_WIDTH)),
            _const_spec((1, GMLP_WIDTH)),
            _const_spec((N_GROUPS, CHUNK, CHUNK)),
            _const_spec((CHUNK, GMLP_WIDTH)),
        ],
        out_specs=[
            feat_spec(QK_WIDTH), tok_spec(QK_WIDTH), feat_spec(ATTN_WIDTH),
            tok_spec(GMLP_WIDTH), tok_spec(D_MODEL), tok_spec(D_MODEL),
        ],
        out_shape=[
            jax.ShapeDtypeStruct((BATCH, SEQ // TQ, QK_WIDTH, TQ), BF16),
            jax.ShapeDtypeStruct((BATCH, SEQ, QK_WIDTH), BF16),
            jax.ShapeDtypeStruct((BATCH, SEQ // TQ, ATTN_WIDTH, TQ), BF16),
            jax.ShapeDtypeStruct((BATCH, SEQ, GMLP_WIDTH), BF16),
            jax.ShapeDtypeStruct((BATCH, SEQ, D_MODEL), BF16),
            jax.ShapeDtypeStruct((BATCH, SEQ, D_MODEL), BF16),
        ],
        scratch_shapes=[pltpu.VMEM((QKV_COLS, D_MODEL), BF16),
                        pltpu.VMEM((D_MODEL, REST_COLS), BF16)],
        compiler_params=pltpu.CompilerParams(
            dimension_semantics=("arbitrary",), vmem_limit_bytes=VMEM_LIMIT),
        name="inproj",
    )(x3d, mod, w_in, qg_rows, kg_rows, ln_g, ln_b, w_spatial, bs_full)


def _rel_buckets(dist):
    n = np.maximum(dist, 0)
    max_exact = N_BUCKETS // 2
    nf = np.maximum(n, 1).astype(np.float32)
    large = max_exact + (np.log(nf / np.float32(max_exact))
                         / np.float32(math.log(MAX_DISTANCE / max_exact))
                         * np.float32(N_BUCKETS - max_exact)).astype(np.int32)
    large = np.minimum(large, N_BUCKETS - 1)
    return np.where(n < max_exact, n, large).astype(np.int32)


def _attn_kernel(table_ref, bound_ref, lam_ref, qt_ref, k_ref, vt_ref, bkt_ref, subg_ref,
                 o_ref, bias_ref, s_scr):
    b = pl.program_id(1)

    @pl.when(b == 0)
    def _():
        heads = [pl.program_id(0) * HEADS_PER_STEP + hh for hh in range(HEADS_PER_STEP)]

        def lookup(bkt):
            tiles = [jnp.zeros(bkt.shape, F32)] * HEADS_PER_STEP
            for bucket in range(N_BUCKETS - 1):
                hit = bkt == bucket
                tiles = [jnp.where(hit, (table_ref[bucket, h] - table_ref[N_BUCKETS - 1, h]) * LOG2E, t)
                         for h, t in zip(heads, tiles)]
            return tiles

        key = lax.broadcasted_iota(jnp.int32, (TQ, TQ), 0)
        qry = lax.broadcasted_iota(jnp.int32, (TQ, TQ), 1)
        half = TQ // 2
        z = jnp.zeros((half, half), F32)
        diag = lookup(bkt_ref[0])
        quad = lookup(bkt_ref[1, half:TQ, 0:half])
        for hh in range(HEADS_PER_STEP):
            tile = jnp.where(key <= qry, diag[hh], -jnp.inf)
            bias_ref[hh, 0] = jnp.concatenate([tile, tile], axis=1)
            tile = jnp.concatenate([jnp.concatenate([z, z], axis=1),
                                    jnp.concatenate([quad[hh], z], axis=1)], axis=0)
            bias_ref[hh, 1] = jnp.concatenate([tile, tile], axis=1)

    @pl.when(bound_ref[0, 0] <= MAX_SAFE_LOG2)
    def _():
        _attn_program(lam_ref, subg_ref, qt_ref, k_ref, vt_ref, o_ref, bias_ref, s_scr,
                      track_max=False)

    @pl.when(jnp.logical_not(bound_ref[0, 0] <= MAX_SAFE_LOG2))
    def _():
        _attn_program(lam_ref, subg_ref, qt_ref, k_ref, vt_ref, o_ref, bias_ref, s_scr,
                      track_max=True)


def _attn_program(lam_ref, subg_ref, qt_ref, k_ref, vt_ref, o_ref, bias_ref, s_scr, *, track_max):
    lp = lam_ref[...]
    lam = (jnp.exp(jnp.sum(lp[0:1] * lp[1:2], axis=1, keepdims=True))
           - jnp.exp(jnp.sum(lp[2:3] * lp[3:4], axis=1, keepdims=True)) + LAM_INIT)
    subg = subg_ref[...]
    zeros = jnp.zeros((HEAD_DIM, TQ), BF16)
    ones = jnp.ones((SUM_ROWS, TQ), BF16)
    n_strips = 2 * TQ // LANES
    nq = SEQ // TQ
    blocks = [(hh, qi, j) for hh in range(HEADS_PER_STEP) for qi in range(nq) for j in range(qi + 1)]
    qpads = {}

    def head_cols(hh, width):
        return slice(hh * width, (hh + 1) * width)

    def keys(hh, j):
        return k_ref[0, j * TQ:(j + 1) * TQ, head_cols(hh, 2 * HEAD_DIM)]

    def qpad(hh, qi):
        if (hh, qi) not in qpads:
            qt = qt_ref[0, qi, head_cols(hh, 2 * HEAD_DIM), :]
            qpads[hh, qi] = jnp.concatenate(
                [jnp.concatenate([qt[0:HEAD_DIM], zeros], axis=0),
                 jnp.concatenate([zeros, qt[HEAD_DIM:2 * HEAD_DIM]], axis=0)], axis=1)
        return qpads[hh, qi]

    def scores(hh, qi, j):
        return jnp.dot(keys(hh, j), qpad(hh, qi), preferred_element_type=F32)

    def probs_tracked(s, hh, qi, j, m):
        p, alpha, m_out = [], [], []
        for c in range(n_strips):
            sc = s[:, c * LANES:(c + 1) * LANES]
            if j == qi:
                sc = sc + bias_ref[hh, 0, :, c * LANES:(c + 1) * LANES]
            elif j == qi - 1:
                sc = sc + bias_ref[hh, 1, :, c * LANES:(c + 1) * LANES]
            blk_max = jnp.max(sc, axis=0, keepdims=True)
            m_new = blk_max if j == 0 else jnp.maximum(m[c], blk_max)
            p.append(jnp.exp2(sc - m_new).astype(BF16))
            alpha.append(None if j == 0 else jnp.exp2(m[c] - m_new))
            m_out.append(m_new)
        return jnp.concatenate(p, axis=1), alpha, m_out

    half = TQ // 2
    hi_cols = (slice(half, TQ), slice(TQ + half, 2 * TQ))

    def issue_bounded(n):
        hh, qi, j = blocks[n]
        slot = s_scr.at[n % S_SLOTS]
        if j < qi:
            slot[...] = scores(hh, qi, j)
            return
        kb = keys(hh, j)
        qp = qpad(hh, qi)
        slot[0:half, :] = jnp.dot(kb[0:half], qp, preferred_element_type=F32)
        s_hi = jnp.dot(kb[half:TQ], jnp.concatenate([qp[:, c] for c in hi_cols], axis=1),
                       preferred_element_type=F32)
        for t, c in enumerate(hi_cols):
            slot[half:TQ, c] = s_hi[:, t * half:(t + 1) * half]

    def probs_bounded(n):
        hh, qi, j = blocks[n]
        slot = s_scr.at[n % S_SLOTS]
        if j < qi - 1:
            return jnp.exp2(slot[...])
        if j == qi - 1:
            return jnp.exp2(slot[...] + bias_ref[hh, 1])
        lo = jnp.exp2(slot[0:half, :] + bias_ref[hh, 0, 0:half, :])
        z = jnp.zeros((half, half), F32)
        hi = [jnp.exp2(slot[half:TQ, c] + bias_ref[hh, 0, half:TQ, c]) for c in hi_cols]
        return jnp.concatenate([lo, jnp.concatenate([z, hi[0], z, hi[1]], axis=1)], axis=0)

    ahead = QK_AHEAD_TRACKED if track_max else QK_AHEAD_BOUNDED

    issue = (lambda n: scores(*blocks[n])) if track_max else issue_bounded
    pending = [issue(n) for n in range(ahead)]
    m = acc = psum = None
    for n, (hh, qi, j) in enumerate(blocks):
        s = pending.pop(0)
        if n + ahead < len(blocks):
            pending.append(issue(n + ahead))
        vt = vt_ref[0, j, head_cols(hh, V_DIM), :]
        if track_max:
            p, alpha, m = probs_tracked(s, hh, qi, j, m)
            vt_aug = jnp.concatenate([vt, ones], axis=0)
            pv = jnp.dot(vt_aug, p, preferred_element_type=F32)[0:V_DIM + 8]
            if j == 0:
                acc = pv
            else:
                acc = jnp.concatenate(
                    [alpha[c] * acc[:, c * LANES:(c + 1) * LANES] + pv[:, c * LANES:(c + 1) * LANES]
                     for c in range(n_strips)], axis=1)
        else:
            p = probs_bounded(n)
            blk_sum = jnp.sum(p, axis=0, keepdims=True)
            pv = jnp.dot(vt, p.astype(BF16), preferred_element_type=F32)
            acc = pv if j == 0 else acc + pv
            psum = blk_sum if j == 0 else psum + blk_sum
        if j == qi:
            inv = 1.0 / (acc[V_DIM:V_DIM + 1] if track_max else psum)
            ot = (acc[0:V_DIM, 0:TQ] * inv[:, 0:TQ]
                  - acc[0:V_DIM, TQ:2 * TQ] * (lam * inv[:, TQ:2 * TQ]))
            ms = jnp.mean(ot * ot, axis=0, keepdims=True)
            on = (ot * lax.rsqrt(ms + EPS)).T
            o_ref[0, qi * TQ:(qi + 1) * TQ, head_cols(hh, V_DIM)] = (on * subg).astype(BF16)


def _attention(table, bound, lam_rows, qt, k, vt, buckets, sub_gain):
    return pl.pallas_call(
        _attn_kernel,
        grid=(N_HEADS // HEADS_PER_STEP, BATCH),
        in_specs=[
            pl.BlockSpec(memory_space=pltpu.SMEM),
            pl.BlockSpec(memory_space=pltpu.SMEM),
            _const_spec((4, HEAD_DIM)),
            pl.BlockSpec((1, SEQ // TQ, HEADS_PER_STEP * 2 * HEAD_DIM, TQ),
                         lambda g, b: (b, 0, g, 0)),
            pl.BlockSpec((1, SEQ, HEADS_PER_STEP * 2 * HEAD_DIM), lambda g, b: (b, 0, g)),
            pl.BlockSpec((1, SEQ // TQ, HEADS_PER_STEP * V_DIM, TQ), lambda g, b: (b, 0, g, 0)),
            _const_spec((2, TQ, TQ)),
            _const_spec((1, V_DIM)),
        ],
        out_specs=pl.BlockSpec((1, SEQ, HEADS_PER_STEP * V_DIM), lambda g, b: (b, 0, g)),
        out_shape=jax.ShapeDtypeStruct((BATCH, SEQ, ATTN_WIDTH), BF16),
        scratch_shapes=[pltpu.VMEM((HEADS_PER_STEP, 2, TQ, 2 * TQ), F32),
                        pltpu.VMEM((S_SLOTS, TQ, 2 * TQ), F32)],
        compiler_params=pltpu.CompilerParams(
            dimension_semantics=("arbitrary", "arbitrary"),
            vmem_limit_bytes=VMEM_LIMIT),
        name="diff_attn",
    )(table, bound, lam_rows, qt, k, vt, buckets, sub_gain)


def _mixffn_kernel(x_ref, mod_ref, o_ref, ub_ref, ga_ref, gb_ref,
                   wa_ref, wb_ref, wo_ref, wg_ref, wu_ref, wd_ref, out_ref,
                   wa_s, wb_s, wo_s, wg_s, wu_s, wd_s):
    i = pl.program_id(0)

    @pl.when(i < W_STEPS_FUSED)
    def _():
        for dst, chunk in ((wa_s, wa_ref), (wb_s, wb_ref), (wo_s, wo_ref),
                           (wg_s, wg_ref), (wu_s, wu_ref), (wd_s, wd_ref)):
            _stash_rows(dst, chunk, i)

    @pl.when(i >= W_STEPS_FUSED)
    def _():
        b = lax.div(i - W_STEPS_FUSED, SEQ // TM_FUSED)
        gate1 = _mod_row(mod_ref, 5, b)
        shift, scale, gate2 = (_mod_row(mod_ref, j, b) for j in (6, 7, 8))
        chunks = [slice(r, r + FUSED_CHUNK) for r in range(0, TM_FUSED, FUSED_CHUNK)]
        x2 = []
        for rows in chunks:
            ya = jnp.dot(o_ref[rows, :], wa_s[...], preferred_element_type=F32)
            yb = jnp.dot(ub_ref[rows, :], wb_s[...], preferred_element_type=F32)
            merged = (ga_ref[rows, :].astype(F32) * ya
                      + gb_ref[rows, :].astype(F32) * yb).astype(BF16)
            z = jnp.dot(merged, wo_s[...], preferred_element_type=F32)
            x2.append(x_ref[rows, :] + gate1 * z)
        for rows, x in zip(chunks, x2):
            h = _modulated_rmsnorm(x, shift, scale).astype(BF16)
            g = jnp.dot(h, wg_s[...], preferred_element_type=F32)
            u = jnp.dot(h, wu_s[...], preferred_element_type=F32)
            a = (_silu(g) * u).astype(BF16)
            y = jnp.dot(a, wd_s[...], preferred_element_type=F32)
            out_ref[rows, :] = x + (0.5 * gate2) * y


def _mixffn(x2d, mod, o2d, ub2d, ga2d, gb2d, wa, wb, wo, wg, wu, wd):
    t = x2d.shape[0]
    tile = lambda i: jnp.maximum(i - W_STEPS_FUSED, 0)
    tok = lambda: pl.BlockSpec((TM_FUSED, D_MODEL), lambda i: (tile(i), 0))
    weights = (wa, wb, wo, wg, wu, wd)
    return pl.pallas_call(
        _mixffn_kernel,
        grid=(W_STEPS_FUSED + t // TM_FUSED,),
        in_specs=[
            tok(),
            _const_spec((N_MOD, BATCH, D_MODEL)),
            tok(), tok(), tok(), tok(),
        ] + [_weight_rows_spec(w.shape, W_STEPS_FUSED) for w in weights],
        out_specs=tok(),
        out_shape=jax.ShapeDtypeStruct((t, D_MODEL), F32),
        scratch_shapes=[pltpu.VMEM(w.shape, BF16) for w in weights],
        compiler_params=pltpu.CompilerParams(
            dimension_semantics=("arbitrary",), vmem_limit_bytes=VMEM_LIMIT),
        name="mix_ffn2",
    )(x2d, mod, o2d, ub2d, ga2d, gb2d, *weights)


def kernel(x, c, w_ada, b_ada, w_ffn1_gate, w_ffn1_up, w_ffn1_down, w_in, q_norm_g, k_norm_g, lam_q1, lam_k1, lam_q2, lam_k2, subln_g, rel_bias_table, gmlp_ln_g, gmlp_ln_b, w_spatial, b_spatial, w_a_proj, w_b_proj, w_o, w_ffn2_gate, w_ffn2_up, w_ffn2_down):
    b, s, d = x.shape
    t = b * s

    mod, qg_rows, kg_rows, bs_full, score_bound, lam_rows, sub_gain = _adaln_mod(
        c, w_ada[0], b_ada, q_norm_g, k_norm_g, b_spatial[0], rel_bias_table,
        lam_q1, lam_k1, lam_q2, lam_k2, subln_g)

    x1 = _ffn(x.reshape(t, d), mod, w_ffn1_gate[0], w_ffn1_up[0], w_ffn1_down[0], sub=0)

    qt, k, vt, ub, ga, gb = _inproj(
        x1.reshape(b, s, d), mod, w_in[0], qg_rows, kg_rows, gmlp_ln_g, gmlp_ln_b,
        w_spatial[0], bs_full)

    kk = np.arange(TQ)[:, None]
    qq = np.arange(TQ)[None, :]
    near = _rel_buckets(TQ + qq - kk)
    far_only = near == N_BUCKETS - 1
    far_only[TQ // 2:, :TQ // 2] = True
    assert far_only.all(), "previous-block bias must be confined to one quadrant"
    buckets = jnp.asarray(np.stack([_rel_buckets(qq - kk), near]))
    o = _attention(rel_bias_table, score_bound, lam_rows, qt, k, vt, buckets, sub_gain)

    x3 = _mixffn(x1, mod, o.reshape(t, ATTN_WIDTH), ub.reshape(t, GMLP_WIDTH),
                 ga.reshape(t, d), gb.reshape(t, d), w_a_proj[0], w_b_proj[0], w_o[0],
                 w_ffn2_gate[0], w_ffn2_up[0], w_ffn2_down[0])
    return x3.reshape(b, s, d)
```
